```python
import jax, jax.numpy as jnp
from jax import lax
import numpy as np

D_MODEL = 1024
BATCH = 4
SEQ = 4096
DEPTH = 2

GRID_W = 64
CTX_LEN = 256
NORM_EPS = 1e-6
N_MOD = 6

MLA_HEADS = 8
MLA_NOPE = 64
MLA_ROPE = 32
MLA_V = 64
Q_RANK = 256
KV_RANK = 128
ROPE_BASE = 10000.0
Q_BLOCK = 128

RWKV_HEADS = 8
RWKV_HEAD = 64
RWKV_WIDTH = RWKV_HEADS * RWKV_HEAD
DECAY_LORA = 64
ICLR_LORA = 64
GATE_LORA = 160
N_DIR = 2
LNX_EPS = 64e-5

MLA_WIDTH = MLA_HEADS * MLA_V
MIX_WIDTH = MLA_WIDTH + RWKV_WIDTH

MLA_COLS = Q_RANK + KV_RANK + MLA_ROPE
RWKV_COLS = 3 * RWKV_WIDTH + N_DIR * DECAY_LORA + N_DIR * ICLR_LORA + GATE_LORA
IN_COLS = MLA_COLS + RWKV_COLS
RWKV_SPLITS = (RWKV_WIDTH, 2 * RWKV_WIDTH, 3 * RWKV_WIDTH,
               3 * RWKV_WIDTH + N_DIR * DECAY_LORA,
               3 * RWKV_WIDTH + N_DIR * (DECAY_LORA + ICLR_LORA))

N_EXPERTS = 16
N_GROUPS = 4
EXPERTS_PER_GROUP = N_EXPERTS // N_GROUPS
TOP_K = 2
EXPERT_HIDDEN = 256
SHARED_HIDDEN = 256

kernel_name = 'hybrid_mla_rwkv7_grouped_moe_dit'


def rmsnorm(x, g):
    xf = x.astype(jnp.float32)
    y = xf * lax.rsqrt(jnp.mean(xf * xf, axis=-1, keepdims=True) + NORM_EPS)
    return (y * g.astype(jnp.float32)).astype(x.dtype)


def modulate(h, shift, scale):
    return h * (1 + scale[:, None, :]) + shift[:, None, :]


def axial_rope_tables(rows):
    row = jnp.repeat(jnp.arange(rows, dtype=jnp.float32), GRID_W)
    col = jnp.tile(jnp.arange(GRID_W, dtype=jnp.float32), rows)
    axis_dim = MLA_ROPE // 2
    inv_freq = ROPE_BASE ** (-jnp.arange(0, axis_dim, 2, dtype=jnp.float32) / axis_dim)
    ang = jnp.concatenate([row[:, None] * inv_freq, col[:, None] * inv_freq], axis=-1)
    return jnp.cos(ang), jnp.sin(ang)


def apply_rope(x, cos, sin):
    half = x.shape[-1] // 2
    x1, x2 = x[..., :half], x[..., half:]
    c = cos[None, :, None, :].astype(x.dtype)
    s = sin[None, :, None, :].astype(x.dtype)
    return jnp.concatenate([x1 * c - x2 * s, x1 * s + x2 * c], axis=-1)


def mla_qkv(p, q_norm_g, kv_norm_g, w_uq, w_ukv, rope):
    b, t, _ = p.shape
    c_q = p[..., :Q_RANK]
    c_kv = p[..., Q_RANK:Q_RANK + KV_RANK]
    k_rope = p[..., Q_RANK + KV_RANK:MLA_COLS][:, :, None, :]
    q = (rmsnorm(c_q, q_norm_g) @ w_uq).reshape(b, t, MLA_HEADS, MLA_NOPE + MLA_ROPE)
    kv = (rmsnorm(c_kv, kv_norm_g) @ w_ukv).reshape(b, t, MLA_HEADS, MLA_NOPE + MLA_V)
    q_nope, q_rope = q[..., :MLA_NOPE], q[..., MLA_NOPE:]
    k_nope, v = kv[..., :MLA_NOPE], kv[..., MLA_NOPE:]
    if rope is not None:
        q_rope = apply_rope(q_rope, *rope)
        k_rope = apply_rope(k_rope, *rope)
    q = jnp.concatenate([q_nope, q_rope], axis=-1)
    k = jnp.concatenate([k_nope, jnp.broadcast_to(k_rope, (b, t, MLA_HEADS, MLA_ROPE))], axis=-1)
    return q, k, v


def blocked_attention(q, k, v):
    b, t, h, dq = q.shape
    scale = dq ** -0.5
    qb = q.reshape(b, t // Q_BLOCK, Q_BLOCK, h, dq).transpose(1, 0, 2, 3, 4)

    def one_block(q_blk):
        s = jnp.einsum('bqhd,bkhd->bhqk', q_blk, k).astype(jnp.float32) * scale
        p = jax.nn.softmax(s, axis=-1).astype(v.dtype)
        return jnp.einsum('bhqk,bkhd->bqhd', p, v)

    out = lax.map(one_block, qb)
    return out.transpose(1, 0, 2, 3, 4).reshape(b, t, h * v.shape[-1])


def centred_token_shift(p, mu):
    prev = jnp.pad(p, ((0, 0), (1, 0), (0, 0)))[:, :-1]
    nxt = jnp.pad(p, ((0, 0), (0, 1), (0, 0)))[:, 1:]
    return p + mu[0] * (prev - p) + mu[1] * (nxt - p)


def rwkv_prepare(p, shift_mu, w0, w2, a0, a2, g2, k_k, k_a):
    b, t, _ = p.shape
    p = centred_token_shift(p, shift_mu)
    r, k, v, wl, al, gl = jnp.split(p, RWKV_SPLITS, axis=-1)
    heads = lambda z: z.reshape(*z.shape[:-1], RWKV_HEADS, RWKV_HEAD)
    g = jax.nn.sigmoid(gl) @ g2
    wl = wl.reshape(b, t, N_DIR, DECAY_LORA)
    al = al.reshape(b, t, N_DIR, ICLR_LORA)
    z = (w0 + jnp.einsum('btdr,drc->btdc', jnp.tanh(wl), w2)).astype(jnp.float32)
    decay = jnp.exp(-jnp.exp(-jax.nn.softplus(-z) - 0.5))
    a = jax.nn.sigmoid(a0 + jnp.einsum('btdr,drc->btdc', al, a2))
    kk = heads(k * k_k).astype(jnp.float32)
    kk = kk * lax.rsqrt(jnp.sum(kk * kk, axis=-1, keepdims=True) + 1e-12)
    k_eff = k[:, :, None, :] * (1 + (a - 1) * k_a)
    return heads(r), heads(v), kk, heads(k_eff), heads(decay), heads(a), g


def rwkv7_scan(state0, r, decay, k, v, kk, a, reverse):
    def step(S, inp):
        r_t, w_t, k_t, v_t, kk_t, a_t = inp
        s_kk = jnp.einsum('bhvk,bhk->bhv', S, kk_t)
        S = (S * w_t[:, :, None, :]
             - s_kk[..., None] * (kk_t * a_t)[:, :, None, :]
             + v_t[..., None] * k_t[:, :, None, :])
        return S, jnp.einsum('bhvk,bhk->bhv', S, r_t)

    xs = tuple(jnp.moveaxis(z.astype(jnp.float32), 1, 0) for z in (r, decay, k, v, kk, a))
    s_last, ys = lax.scan(step, state0, xs, reverse=reverse)
    return jnp.moveaxis(ys, 0, 1), s_last


def rwkv_output(y, r, v, k_eff, g, r_k, lnx_g, lnx_b):
    b, t = y.shape[:2]
    mu = jnp.mean(y, axis=-1, keepdims=True)
    var = jnp.mean(jnp.square(y - mu), axis=-1, keepdims=True)
    yn = ((y - mu) * lax.rsqrt(var + LNX_EPS)).reshape(b, t, RWKV_WIDTH) * lnx_g + lnx_b
    bonus = jnp.sum(r[:, :, None] * k_eff * r_k, axis=(2, 4))
    bonus = (bonus[..., None] * v).reshape(b, t, RWKV_WIDTH)
    return ((yn + bonus) * g).astype(g.dtype)


def rwkv_time_mix(p_lat, p_ctx, shift_mu, w0, w2, a0, a2, g2, k_k, k_a, r_k, lnx_g, lnx_b, ctx_out):
    lat = rwkv_prepare(p_lat, shift_mu, w0, w2, a0, a2, g2, k_k, k_a)
    ctx = rwkv_prepare(p_ctx, shift_mu, w0, w2, a0, a2, g2, k_k, k_a)
    s0 = jnp.zeros((p_lat.shape[0], RWKV_HEADS, RWKV_HEAD, RWKV_HEAD), jnp.float32)

    def scan_dir(prep, d, state0, reverse):
        r, v, kk, k_eff, decay, a, _ = prep
        return rwkv7_scan(state0, r, decay[:, :, d], k_eff[:, :, d], v, kk, a[:, :, d], reverse)

    ys_lat, ys_ctx = [], []
    for d, reverse in enumerate((False, True)):
        y_c, s_ctx = scan_dir(ctx, d, s0, reverse)
        y_l, _ = scan_dir(lat, d, s_ctx, reverse)
        ys_ctx.append(y_c)
        ys_lat.append(y_l)
    out_lat = rwkv_output(ys_lat[0] + ys_lat[1], lat[0], lat[1], lat[3], lat[6], r_k, lnx_g, lnx_b)
    if not ctx_out:
        return out_lat, None
    out_ctx = rwkv_output(ys_ctx[0] + ys_ctx[1], ctx[0], ctx[1], ctx[3], ctx[6], r_k, lnx_g, lnx_b)
    return out_lat, out_ctx


def parallel_mixer(h, hc, rope, w_in, q_norm_g, kv_norm_g, w_uq, w_ukv, shift_mu, w0, w2, a0, a2,
                   g2, k_k, k_a, r_k, lnx_g, lnx_b, w_out, ctx_out):
    p = h @ w_in
    pc = hc @ w_in
    q, k, v = mla_qkv(p[..., :MLA_COLS], q_norm_g, kv_norm_g, w_uq, w_ukv, rope)
    qc, kc, vc = mla_qkv(pc[..., :MLA_COLS], q_norm_g, kv_norm_g, w_uq, w_ukv, None)
    att = blocked_attention(q, jnp.concatenate([k, kc], axis=1), jnp.concatenate([v, vc], axis=1))
    rw, rwc = rwkv_time_mix(p[..., MLA_COLS:], pc[..., MLA_COLS:], shift_mu, w0, w2, a0, a2, g2,
                            k_k, k_a, r_k, lnx_g, lnx_b, ctx_out)
    y = jnp.concatenate([att, rw], axis=-1) @ w_out
    if not ctx_out:
        return y, None
    att_c = blocked_attention(qc, kc, vc)
    yc = jnp.concatenate([att_c, rwc], axis=-1) @ w_out
    return y, yc


def route_shared(h, router_w, router_bias):
    scores = jax.nn.sigmoid((h @ router_w).astype(jnp.float32))
    biased = scores + router_bias.astype(jnp.float32)
    grouped = biased.reshape(*biased.shape[:-1], N_GROUPS, EXPERTS_PER_GROUP)
    group_score = jnp.sum(lax.top_k(grouped, TOP_K)[0], axis=-1)
    best_group = jnp.argmax(group_score, axis=-1)
    expert_group = jnp.arange(N_EXPERTS) // EXPERTS_PER_GROUP
    masked = jnp.where(expert_group == best_group[..., None], biased, -jnp.inf)
    _, idx = lax.top_k(masked, TOP_K)
    w = jnp.take_along_axis(scores, idx, axis=-1)
    w = w / jnp.sum(w, axis=-1, keepdims=True)
    gates = jnp.sum(jax.nn.one_hot(idx, N_EXPERTS, dtype=jnp.float32) * w[..., None], axis=-2)
    return gates.astype(h.dtype)


def swiglu(h, wg, wu, wd):
    return (jax.nn.silu(h @ wg) * (h @ wu)) @ wd


def grouped_moe(h, router_w, router_bias, e_gate, e_up, e_down, s_gate, s_up, s_down):
    gates = route_shared(h, router_w, router_bias)
    out = swiglu(h, s_gate, s_up, s_down)
    for e in range(N_EXPERTS):
        out = out + gates[..., e:e + 1] * swiglu(h, e_gate[e], e_up[e], e_down[e])
    return out


def setup_inputs(seed: int = 0) -> dict:
    key = jax.random.key(seed)
    ks = iter(jax.random.split(key, 48))
    nrm = lambda shape, s: jax.random.normal(next(ks), shape, jnp.float32) * s
    gain = lambda shape: 1.0 + nrm(shape, 0.1)
    L, D = DEPTH, D_MODEL
    return {
        'x': nrm((BATCH, SEQ, D), 1.0),
        'c': nrm((BATCH, D), 1.0),
        'ctx': nrm((BATCH, CTX_LEN, D), 1.0),
        'c_ctx': nrm((D,), 1.0),
        'ada_w': nrm((L, D, N_MOD * D), 0.5 * D ** -0.5),
        'ada_b': nrm((L, N_MOD * D), 0.02),
        'mix_pre_g': gain((L, D)),
        'mix_post_g': gain((L, D)),
        'ffn_pre_g': gain((L, D)),
        'ffn_post_g': gain((L, D)),
        'w_in': nrm((L, D, IN_COLS), D ** -0.5),
        'q_norm_g': gain((L, Q_RANK)),
        'kv_norm_g': gain((L, KV_RANK)),
        'w_uq': nrm((L, Q_RANK, MLA_HEADS * (MLA_NOPE + MLA_ROPE)), Q_RANK ** -0.5),
        'w_ukv': nrm((L, KV_RANK, MLA_HEADS * (MLA_NOPE + MLA_V)), KV_RANK ** -0.5),
        'shift_mu': jax.random.uniform(next(ks), (L, 2, RWKV_COLS), jnp.float32, 0.0, 0.5),
        'decay_w0': nrm((L, N_DIR, RWKV_WIDTH), 0.5),
        'decay_w2': nrm((L, N_DIR, DECAY_LORA, RWKV_WIDTH), 0.5 * DECAY_LORA ** -0.5),
        'iclr_a0': nrm((L, N_DIR, RWKV_WIDTH), 0.5),
        'iclr_a2': nrm((L, N_DIR, ICLR_LORA, RWKV_WIDTH), 0.5 * ICLR_LORA ** -0.5),
        'gate_g2': nrm((L, GATE_LORA, RWKV_WIDTH), GATE_LORA ** -0.5),
        'k_k': 0.85 + nrm((L, RWKV_WIDTH), 0.05),
        'k_a': 1.0 + nrm((L, RWKV_WIDTH), 0.05),
        'r_k': nrm((L, RWKV_HEADS, RWKV_HEAD), 0.1),
        'lnx_g': gain((L, RWKV_WIDTH)),
        'lnx_b': nrm((L, RWKV_WIDTH), 0.02),
        'w_out': nrm((L, MIX_WIDTH, D), MIX_WIDTH ** -0.5),
        'router_w': nrm((D, N_EXPERTS), D ** -0.5),
        'router_bias': nrm((N_EXPERTS,), 0.01),
        'exp_w_gate': nrm((L, N_EXPERTS, D, EXPERT_HIDDEN), D ** -0.5),
        'exp_w_up': nrm((L, N_EXPERTS, D, EXPERT_HIDDEN), D ** -0.5),
        'exp_w_down': nrm((L, N_EXPERTS, EXPERT_HIDDEN, D), EXPERT_HIDDEN ** -0.5),
        'sh_w_gate': nrm((L, D, SHARED_HIDDEN), D ** -0.5),
        'sh_w_up': nrm((L, D, SHARED_HIDDEN), D ** -0.5),
        'sh_w_down': nrm((L, SHARED_HIDDEN, D), SHARED_HIDDEN ** -0.5),
    }


def reference(x, c, ctx, c_ctx, ada_w, ada_b, mix_pre_g, mix_post_g, ffn_pre_g, ffn_post_g,
              w_in, q_norm_g, kv_norm_g, w_uq, w_ukv, shift_mu, decay_w0, decay_w2, iclr_a0, iclr_a2,
              gate_g2, k_k, k_a, r_k, lnx_g, lnx_b, w_out, router_w, router_bias,
              exp_w_gate, exp_w_up, exp_w_down, sh_w_gate, sh_w_up, sh_w_down):
    rows = x.shape[1] // GRID_W
    rope = axial_rope_tables(rows)
    xc = ctx
    cond = jax.nn.silu(c)
    cond_ctx = jax.nn.silu(c_ctx)[None]
    for l in range(DEPTH):
        ctx_out = l < DEPTH - 1
        mod = jnp.split(cond @ ada_w[l] + ada_b[l], N_MOD, axis=-1)
        mod_c = jnp.split(cond_ctx @ ada_w[l] + ada_b[l], N_MOD, axis=-1)
        h = modulate(rmsnorm(x, mix_pre_g[l]), mod[0], mod[1])
        hc = modulate(rmsnorm(xc, mix_pre_g[l]), mod_c[0], mod_c[1])
        y, yc = parallel_mixer(h, hc, rope, w_in[l], q_norm_g[l], kv_norm_g[l], w_uq[l], w_ukv[l],
                               shift_mu[l], decay_w0[l], decay_w2[l], iclr_a0[l], iclr_a2[l], gate_g2[l],
                               k_k[l], k_a[l], r_k[l], lnx_g[l], lnx_b[l], w_out[l], ctx_out)
        x = x + mod[2][:, None] * rmsnorm(y, mix_post_g[l])
        h = modulate(rmsnorm(x, ffn_pre_g[l]), mod[3], mod[4])
        f = grouped_moe(h, router_w, router_bias, exp_w_gate[l], exp_w_up[l], exp_w_down[l],
                        sh_w_gate[l], sh_w_up[l], sh_w_down[l])
        x = x + mod[5][:, None] * rmsnorm(f, ffn_post_g[l])
        if ctx_out:
            xc = xc + mod_c[2][:, None] * rmsnorm(yc, mix_post_g[l])
            hc = modulate(rmsnorm(xc, ffn_pre_g[l]), mod_c[3], mod_c[4])
            fc = grouped_moe(hc, router_w, router_bias, exp_w_gate[l], exp_w_up[l], exp_w_down[l],
                             sh_w_gate[l], sh_w_up[l], sh_w_down[l])
            xc = xc + mod_c[5][:, None] * rmsnorm(fc, ffn_post_g[l])
    return x
```

```python
import functools
import math

import jax
import jax.numpy as jnp
from jax import lax
from jax.experimental import pallas as pl
from jax.experimental.pallas import tpu as pltpu

F32 = jnp.float32
BF16 = jnp.bfloat16
HIGHEST = lax.Precision.HIGHEST

D_MODEL = 1024
N_MOD = 6
NORM_EPS = 1e-6
GRID_W = 64
ROPE_BASE = 10000.0

HEADS = 8
MLA_NOPE = 64
MLA_ROPE = 32
MLA_V = 64
Q_RANK = 256
KV_RANK = 128
MLA_COLS = Q_RANK + KV_RANK + MLA_ROPE

RWKV_HEAD = 64
RWKV_WIDTH = HEADS * RWKV_HEAD
DECAY_LORA = 64
ICLR_LORA = 64
GATE_LORA = 160
RWKV_COLS = 3 * RWKV_WIDTH + 2 * DECAY_LORA + 2 * ICLR_LORA + GATE_LORA
RWKV_PAD = 2048
GATE_PAD = 256
LNX_EPS = 64e-5

N_EXPERTS = 16
N_GROUPS = 4
EXPERTS_PER_GROUP = 4
EXPERT_HIDDEN = 256

LANES = 128
HEAD_SLOT = 128
IN_PACKED = Q_RANK + KV_RANK + 2 * LANES + RWKV_PAD
TOKEN_TILE = 256
CHUNK = 64
VMEM_LIMIT = 48 * 1024 * 1024


def _dot(a, b):
    return jnp.dot(a, b, preferred_element_type=F32)


def _dot_nt(a, b):
    return lax.dot_general(a, b, (((1,), (1,)), ((), ())), preferred_element_type=F32)


def _dot_tn(a, b):
    return lax.dot_general(a, b, (((0,), (0,)), ((), ())), preferred_element_type=F32)


def _rms(x, g):
    return x * lax.rsqrt(jnp.mean(x * x, axis=-1, keepdims=True) + NORM_EPS) * g


def _seg_sum(x, ones_bd):
    hi = x.astype(BF16)
    lo = (x - hi.astype(F32)).astype(BF16)
    return _dot(hi, ones_bd) + _dot(lo, ones_bd)


def _params(sem):
    return pltpu.CompilerParams(dimension_semantics=sem, vmem_limit_bytes=VMEM_LIMIT)


def _ada_kernel(c_ref, w_ref, b_ref, o_ref):
    c = c_ref[...]
    cond = c * jax.nn.sigmoid(c)
    o_ref[0] = jnp.dot(cond, w_ref[0], precision=HIGHEST, preferred_element_type=F32) + b_ref[0]


def _ada_modulation(cond_rows, ada_w, ada_b):
    depth, d, n = ada_w.shape
    tn = 1536
    rows = cond_rows.shape[0]
    return pl.pallas_call(
        _ada_kernel,
        out_shape=jax.ShapeDtypeStruct((depth, rows, n), F32),
        grid=(depth, n // tn),
        in_specs=[
            pl.BlockSpec((rows, d), lambda l, j: (0, 0)),
            pl.BlockSpec((1, d, tn), lambda l, j: (l, 0, j)),
            pl.BlockSpec((1, 1, tn), lambda l, j: (l, 0, j)),
        ],
        out_specs=pl.BlockSpec((1, rows, tn), lambda l, j: (l, 0, j)),
        compiler_params=_params(("parallel", "parallel")),
        name="ada_modulation",
    )(cond_rows, ada_w, ada_b.reshape(depth, 1, n))


def _in_proj_kernel(x_ref, mod_ref, g_ref, win_ref, qng_ref, kvng_ref, wq1_ref, wq2_ref, wuk_ref, wuv_ref,
                    cq_ref, sq_ref, ck_ref, q_out, k_out, v_out, p_out, *, q_scale):
    x = x_ref[...]
    shift = mod_ref[0, 0:1, :]
    scale = mod_ref[0, 1:2, :]
    h = _rms(x, g_ref[...]) * (1.0 + scale) + shift
    p = _dot(h.astype(BF16), win_ref[...])
    c_q = p[:, 0:Q_RANK]
    c_kv = p[:, Q_RANK:Q_RANK + KV_RANK]
    kr_a = p[:, Q_RANK + KV_RANK:Q_RANK + KV_RANK + LANES]
    kr_b = p[:, Q_RANK + KV_RANK + LANES:Q_RANK + KV_RANK + 2 * LANES]
    p_out[...] = p[:, Q_RANK + KV_RANK + 2 * LANES:]

    tile8 = lambda t: jnp.concatenate([t] * HEADS, axis=1)
    cqn = _rms(c_q, qng_ref[...]).astype(BF16)
    q = _dot(cqn, wq1_ref[...]) * tile8(cq_ref[...]) + _dot(cqn, wq2_ref[...]) * tile8(sq_ref[...])
    q_out[...] = (q * q_scale).astype(BF16)

    ckvn = _rms(c_kv, kvng_ref[...]).astype(BF16)
    k_rot = kr_a * ck_ref[...] + kr_b * sq_ref[...]
    k_out[...] = (_dot(ckvn, wuk_ref[...]) + tile8(k_rot)).astype(BF16)
    v_out[...] = _dot(ckvn, wuv_ref[...]).astype(BF16)


def _in_proj(x_all, mods, g, w_in_p, qng, kvng, wq1, wq2, wuk, wuv, cq_t, sq_t, ck_t, *, n_lat, seq, ctx_len):
    nt = x_all.shape[0]
    tm = TOKEN_TILE
    lat_tiles = n_lat // tm
    n_batch = n_lat // seq

    def mod_idx(i):
        return (jnp.where(i < lat_tiles, (i * tm) // seq, n_batch), 0, 0)

    def tab_idx(i):
        return (jnp.where(i < lat_tiles, i % (seq // tm), seq // tm + (i - lat_tiles) % (ctx_len // tm)), 0)

    const = lambda i: (0, 0)
    row = lambda i: (i, 0)
    qw = HEADS * HEAD_SLOT
    return pl.pallas_call(
        functools.partial(_in_proj_kernel, q_scale=float((MLA_NOPE + MLA_ROPE) ** -0.5)),
        out_shape=(
            jax.ShapeDtypeStruct((nt, qw), BF16),
            jax.ShapeDtypeStruct((nt, qw), BF16),
            jax.ShapeDtypeStruct((nt, HEADS * MLA_V), BF16),
            jax.ShapeDtypeStruct((nt, RWKV_PAD), F32),
        ),
        grid=(nt // tm,),
        in_specs=[
            pl.BlockSpec((tm, D_MODEL), row),
            pl.BlockSpec((1, N_MOD, D_MODEL), mod_idx),
            pl.BlockSpec((1, D_MODEL), const),
            pl.BlockSpec((D_MODEL, IN_PACKED), const),
            pl.BlockSpec((1, Q_RANK), const),
            pl.BlockSpec((1, KV_RANK), const),
            pl.BlockSpec((Q_RANK, qw), const),
            pl.BlockSpec((Q_RANK, qw), const),
            pl.BlockSpec((KV_RANK, qw), const),
            pl.BlockSpec((KV_RANK, HEADS * MLA_V), const),
            pl.BlockSpec((tm, HEAD_SLOT), tab_idx),
            pl.BlockSpec((tm, HEAD_SLOT), tab_idx),
            pl.BlockSpec((tm, HEAD_SLOT), tab_idx),
        ],
        out_specs=(
            pl.BlockSpec((tm, qw), row),
            pl.BlockSpec((tm, qw), row),
            pl.BlockSpec((tm, HEADS * MLA_V), row),
            pl.BlockSpec((tm, RWKV_PAD), row),
        ),
        compiler_params=_params(("parallel",)),
        name="in_proj",
    )(x_all, mods, g, w_in_p, qng, kvng, wq1, wq2, wuk, wuv, cq_t, sq_t, ck_t)


def _attn_kernel(q_ref, kl_ref, kc_ref, vl_ref, vc_ref, o_ref, *, lat_q_tiles):
    j = pl.program_id(2)

    def compute(use_lat):
        outs = []
        for h in range(2):
            hs = slice(h * HEAD_SLOT, (h + 1) * HEAD_SLOT)
            q = q_ref[:, hs]
            s_c = _dot_nt(q, kc_ref[:, hs])
            m = jnp.max(s_c, axis=1, keepdims=True)
            if use_lat:
                s_l = _dot_nt(q, kl_ref[:, hs])
                m = jnp.maximum(m, jnp.max(s_l, axis=1, keepdims=True))
                p_l = jnp.exp(s_l - m)
                den = jnp.sum(p_l, axis=1, keepdims=True)
                acc = _dot(p_l.astype(BF16), vl_ref[...])
            p_c = jnp.exp(s_c - m)
            if use_lat:
                den = den + jnp.sum(p_c, axis=1, keepdims=True)
                acc = acc + _dot(p_c.astype(BF16), vc_ref[...])
            else:
                den = jnp.sum(p_c, axis=1, keepdims=True)
                acc = _dot(p_c.astype(BF16), vc_ref[...])
            outs.append(acc / den)
        lane = lax.broadcasted_iota(jnp.int32, outs[0].shape, 1)
        o_ref[...] = jnp.where(lane < MLA_V, outs[0], outs[1]).astype(o_ref.dtype)

    @pl.when(j < lat_q_tiles)
    def _():
        compute(True)

    @pl.when(j >= lat_q_tiles)
    def _():
        compute(False)


def _attention(q, k, v, *, n_batch, seq, ctx_len, ctx_out):
    nt = q.shape[0]
    tq = TOKEN_TILE
    lat_q = seq // tq
    ctx_q = ctx_len // tq if ctx_out else 0
    lat_tiles_all = n_batch * lat_q
    ctx_blk0 = (n_batch * seq) // ctx_len

    def q_idx(b, hp, j):
        return (jnp.where(j < lat_q, b * lat_q + j, lat_tiles_all + b * (ctx_len // tq) + (j - lat_q)), hp)

    return pl.pallas_call(
        functools.partial(_attn_kernel, lat_q_tiles=lat_q),
        out_shape=jax.ShapeDtypeStruct((nt, HEADS * MLA_V), BF16),
        grid=(n_batch, HEADS // 2, lat_q + ctx_q),
        in_specs=[
            pl.BlockSpec((tq, 2 * HEAD_SLOT), q_idx),
            pl.BlockSpec((seq, 2 * HEAD_SLOT), lambda b, hp, j: (b, hp)),
            pl.BlockSpec((ctx_len, 2 * HEAD_SLOT), lambda b, hp, j: (ctx_blk0 + b, hp)),
            pl.BlockSpec((seq, 2 * MLA_V), lambda b, hp, j: (b, hp)),
            pl.BlockSpec((ctx_len, 2 * MLA_V), lambda b, hp, j: (ctx_blk0 + b, hp)),
        ],
        out_specs=pl.BlockSpec((tq, 2 * MLA_V), q_idx),
        compiler_params=_params(("parallel", "parallel", "arbitrary")),
        name="attention",
    )(q, k, k, v, v)


def _rwkv_prep_kernel(p_ref, hp_ref, hn_ref, mu_ref, w2_ref, w0_ref, a2_ref, a0_ref, g2_ref, kk_ref, ka_ref, rk_ref,
                      ones_ref, r_out, v_out, kkn_out, ke0_out, ke1_out, lw0_out, lw1_out, b0_out, b1_out, g_out,
                      bon_out, *, lat_tiles, seq_tiles, ctx_tiles):
    i = pl.program_id(0)
    tm = p_ref.shape[0]
    is_lat = i < lat_tiles
    local = jnp.where(is_lat, i % seq_tiles, (i - lat_tiles) % ctx_tiles)
    seg = jnp.where(is_lat, seq_tiles, ctx_tiles)
    keep_prev = jnp.where(local == 0, 0.0, 1.0)
    keep_next = jnp.where(local == seg - 1, 0.0, 1.0)

    p = p_ref[...]
    row = lax.broadcasted_iota(jnp.int32, (tm, 1), 0)
    prev = jnp.where(row == 0, hp_ref[7:8, :] * keep_prev, pltpu.roll(p, 1, 0))
    nxt = jnp.where(row == tm - 1, hn_ref[0:1, :] * keep_next, pltpu.roll(p, tm - 1, 0))
    ps = p + mu_ref[0:1, :] * (prev - p) + mu_ref[1:2, :] * (nxt - p)

    w = RWKV_WIDTH
    r = ps[:, 0:w]
    k = ps[:, w:2 * w]
    v = ps[:, 2 * w:3 * w]
    wl = ps[:, 3 * w:3 * w + 2 * DECAY_LORA]
    al = ps[:, 3 * w + 2 * DECAY_LORA:3 * w + 2 * DECAY_LORA + 2 * ICLR_LORA]
    gl = ps[:, 3 * w + 2 * DECAY_LORA + 2 * ICLR_LORA:3 * w + 2 * DECAY_LORA + 2 * ICLR_LORA + GATE_PAD]

    g_out[...] = _dot(jax.nn.sigmoid(gl).astype(BF16), g2_ref[...])
    z = w0_ref[...] + _dot(jnp.tanh(wl).astype(BF16), w2_ref[...])
    logw = (-math.exp(-0.5)) * jax.nn.sigmoid(z)
    a = jax.nn.sigmoid(a0_ref[...] + _dot(al.astype(BF16), a2_ref[...]))

    ones_bd = ones_ref[...]
    kk = k * kk_ref[...]
    kk = kk * lax.rsqrt(_seg_sum(kk * kk, ones_bd) + 1e-12)
    ka = ka_ref[...]
    ke0 = k * (1.0 + (a[:, 0:w] - 1.0) * ka)
    ke1 = k * (1.0 + (a[:, w:2 * w] - 1.0) * ka)

    r_out[...] = r
    v_out[...] = v
    kkn_out[...] = kk
    ke0_out[...] = ke0
    ke1_out[...] = ke1
    lw0_out[...] = logw[:, 0:w]
    lw1_out[...] = logw[:, w:2 * w]
    b0_out[...] = a[:, 0:w] * kk
    b1_out[...] = a[:, w:2 * w] * kk
    bon_out[...] = _seg_sum(r * (ke0 + ke1) * rk_ref[...], ones_bd) * v


def _rwkv_prep(p_rw, mu, w2bd, w0, a2bd, a0, g2p, k_k, k_a, r_k, ones_bd, *, n_lat, seq, ctx_len):
    nt = p_rw.shape[0]
    tm = TOKEN_TILE
    n_tiles = nt // tm
    halo = 8
    blocks8 = nt // halo
    const = lambda i: (0, 0)
    row = lambda i: (i, 0)
    w = RWKV_WIDTH
    out = jax.ShapeDtypeStruct((nt, w), F32)
    return pl.pallas_call(
        functools.partial(_rwkv_prep_kernel, lat_tiles=n_lat // tm, seq_tiles=seq // tm, ctx_tiles=ctx_len // tm),
        out_shape=(out,) * 11,
        grid=(n_tiles,),
        in_specs=[
            pl.BlockSpec((tm, RWKV_PAD), row),
            pl.BlockSpec((halo, RWKV_PAD), lambda i: (jnp.maximum(i * (tm // halo) - 1, 0), 0)),
            pl.BlockSpec((halo, RWKV_PAD), lambda i: (jnp.minimum((i + 1) * (tm // halo), blocks8 - 1), 0)),
            pl.BlockSpec((2, RWKV_PAD), const),
            pl.BlockSpec((2 * DECAY_LORA, 2 * w), const),
            pl.BlockSpec((1, 2 * w), const),
            pl.BlockSpec((2 * ICLR_LORA, 2 * w), const),
            pl.BlockSpec((1, 2 * w), const),
            pl.BlockSpec((GATE_PAD, w), const),
            pl.BlockSpec((1, w), const),
            pl.BlockSpec((1, w), const),
            pl.BlockSpec((1, w), const),
            pl.BlockSpec((w, w), const),
        ],
        out_specs=(pl.BlockSpec((tm, w), row),) * 11,
        compiler_params=_params(("parallel",)),
        name="rwkv_prep",
    )(p_rw, p_rw, p_rw, mu, w2bd, w0, a2bd, a0, g2p, k_k, k_a, r_k, ones_bd)


def _unit_tri_inverse(a, ti, si):
    n = a.shape[0]
    same = lambda b: (ti // b) == (si // b)
    t = jnp.where(ti == si, 1.0, 0.0) - jnp.where(same(2), a, 0.0)
    b = 4
    while b <= n:
        a_n = jnp.where(jnp.logical_and(same(b), jnp.logical_not(same(b // 2))), a, 0.0)
        ta = jnp.dot(t, a_n, precision=HIGHEST, preferred_element_type=F32)
        t = t - jnp.dot(ta, t, precision=HIGHEST, preferred_element_type=F32)
        b *= 2
    return t


def _scan_kernel(r_ref, v_ref, kk_ref, ke_ref, lw_ref, b_ref, y_ref, h_ref, *, rev):
    @pl.when(pl.program_id(1) == 0)
    def _():
        h_ref[...] = jnp.zeros_like(h_ref)

    n = CHUNK
    ti = lax.broadcasted_iota(jnp.int32, (n, n), 0)
    si = lax.broadcasted_iota(jnp.int32, (n, n), 1)
    if rev:
        incl, strict = si >= ti, si > ti
    else:
        incl, strict = si <= ti, si < ti
    eye = ti == si

    lw = lw_ref[...]
    cum = jnp.dot(jnp.where(incl, 1.0, 0.0), lw, precision=HIGHEST, preferred_element_type=F32)
    e_in = jnp.exp(cum)
    e_ex = jnp.exp(cum - lw)
    e_neg = jnp.exp(-cum)
    last = 0 if rev else n - 1
    p_all = e_in[last:last + 1, :]

    rt = r_ref[...] * e_in
    kt = kk_ref[...] * e_ex
    bt = b_ref[...] * e_neg
    kb = ke_ref[...] * e_neg
    bh = bt * p_all
    kh = kb * p_all
    v = v_ref[...]

    ys = []
    for h in range(HEADS):
        hs = slice(h * RWKV_HEAD, (h + 1) * RWKV_HEAD)
        rt_h, kt_h, v_h = rt[:, hs], kt[:, hs], v[:, hs]
        gram = _dot_nt(jnp.concatenate([kt_h, rt_h], axis=0).astype(BF16),
                       jnp.concatenate([bt[:, hs], kb[:, hs]], axis=0).astype(BF16))
        a_m = jnp.where(strict, gram[0:n, 0:n], 0.0)
        b_m = jnp.where(strict, gram[0:n, n:2 * n], 0.0)
        c_m = jnp.where(incl, gram[n:2 * n, 0:n], 0.0)
        e_m = jnp.where(incl, gram[n:2 * n, n:2 * n], 0.0)
        t_inv = _unit_tri_inverse(a_m, ti, si)

        v_b = v_h.astype(BF16)
        bev = _dot(jnp.concatenate([b_m, e_m], axis=0).astype(BF16), v_b)
        m12 = _dot(t_inv.astype(BF16), jnp.concatenate([kt_h, bev[0:n]], axis=1).astype(BF16))
        m12_b = m12.astype(BF16)
        qy = jnp.concatenate([rt_h, bev[n:2 * n]], axis=1) - _dot(c_m.astype(BF16), m12_b)
        gh = _dot_tn(bh[:, hs].astype(BF16), m12_b)
        g_m = jnp.where(eye, p_all[:, hs], 0.0) - gh[:, 0:n]
        h_add = _dot_tn(kh[:, hs].astype(BF16), v_b) - gh[:, n:2 * n]

        h0 = h_ref[h]
        ys.append(jnp.dot(qy[:, 0:n], h0, precision=HIGHEST, preferred_element_type=F32) + qy[:, n:2 * n])
        h_ref[h] = jnp.dot(g_m, h0, precision=HIGHEST, preferred_element_type=F32) + h_add
    y_ref[...] = jnp.concatenate(ys, axis=1)


def _rwkv_scan(r, v, kk, ke, lw, b, *, n_batch, seq, ctx_len, rev):
    nt = r.shape[0]
    n = CHUNK
    lat_c = seq // n
    ctx_c = ctx_len // n
    ctx0 = n_batch * lat_c

    def idx(bi, c):
        if rev:
            return (jnp.where(c < ctx_c, ctx0 + bi * ctx_c + (ctx_c - 1 - c), bi * lat_c + (lat_c - 1 - (c - ctx_c))), 0)
        return (jnp.where(c < ctx_c, ctx0 + bi * ctx_c + c, bi * lat_c + (c - ctx_c)), 0)

    spec = pl.BlockSpec((n, RWKV_WIDTH), idx)
    return pl.pallas_call(
        functools.partial(_scan_kernel, rev=rev),
        out_shape=jax.ShapeDtypeStruct((nt, RWKV_WIDTH), F32),
        grid=(n_batch, ctx_c + lat_c),
        in_specs=[spec] * 6,
        out_specs=spec,
        scratch_shapes=[pltpu.VMEM((HEADS, RWKV_HEAD, RWKV_HEAD), F32)],
        compiler_params=_params(("parallel", "arbitrary")),
        name="rwkv_scan_bwd" if rev else "rwkv_scan_fwd",
    )(r, v, kk, ke, lw, b)


def _mix_out_kernel(yf_ref, yb_ref, att_ref, g_ref, bon_ref, lng_ref, lnb_ref, ones_ref, woa_ref, wor_ref, x_ref,
                    mod_ref, pg_ref, o_ref):
    ones_bd = ones_ref[...]
    y = yf_ref[...] + yb_ref[...]
    inv_n = 1.0 / RWKV_HEAD
    mu = _seg_sum(y, ones_bd) * inv_n
    yc = y - mu
    var = _seg_sum(yc * yc, ones_bd) * inv_n
    yn = yc * lax.rsqrt(var + LNX_EPS) * lng_ref[...] + lnb_ref[...]
    rw = (yn + bon_ref[...]) * g_ref[...]
    mix = _dot(att_ref[...], woa_ref[...]) + _dot(rw.astype(BF16), wor_ref[...])
    o_ref[...] = x_ref[...] + mod_ref[0, 2:3, :] * _rms(mix, pg_ref[...])


def _mix_out(yf, yb, att, g, bon, lnx_g, lnx_b, ones_bd, wo_att, wo_rw, x_all, mods, post_g, *, rows, n_lat, seq):
    tm = TOKEN_TILE
    lat_tiles = n_lat // tm
    n_batch = n_lat // seq
    w = RWKV_WIDTH

    def mod_idx(i):
        return (jnp.where(i < lat_tiles, (i * tm) // seq, n_batch), 0, 0)

    const = lambda i: (0, 0)
    row = lambda i: (i, 0)
    return pl.pallas_call(
        _mix_out_kernel,
        out_shape=jax.ShapeDtypeStruct((rows, D_MODEL), F32),
        grid=(rows // tm,),
        in_specs=[
            pl.BlockSpec((tm, w), row),
            pl.BlockSpec((tm, w), row),
            pl.BlockSpec((tm, w), row),
            pl.BlockSpec((tm, w), row),
            pl.BlockSpec((tm, w), row),
            pl.BlockSpec((1, w), const),
            pl.BlockSpec((1, w), const),
            pl.BlockSpec((w, w), const),
            pl.BlockSpec((w, D_MODEL), const),
            pl.BlockSpec((w, D_MODEL), const),
            pl.BlockSpec((tm, D_MODEL), row),
            pl.BlockSpec((1, N_MOD, D_MODEL), mod_idx),
            pl.BlockSpec((1, D_MODEL), const),
        ],
        out_specs=pl.BlockSpec((tm, D_MODEL), row),
        compiler_params=_params(("parallel",)),
        name="mix_out",
    )(yf, yb, att, g, bon, lnx_g, lnx_b, ones_bd, wo_att, wo_rw, x_all, mods, post_g)


def _route(logits_t, bias_col):
    scores = jax.nn.sigmoid(logits_t[0:N_EXPERTS, :])
    biased = scores + bias_col
    s_rows = [scores[e:e + 1, :] for e in range(N_EXPERTS)]
    b_rows = [biased[e:e + 1, :] for e in range(N_EXPERTS)]
    npg = EXPERTS_PER_GROUP
    group_scores = []
    for gi in range(N_GROUPS):
        bg = b_rows[gi * npg:(gi + 1) * npg]
        best_pair = None
        for i in range(npg):
            for j in range(i + 1, npg):
                pair = bg[i] + bg[j]
                best_pair = pair if best_pair is None else jnp.maximum(best_pair, pair)
        group_scores.append(best_pair)
    best = group_scores[0]
    best_idx = jnp.zeros(best.shape, jnp.int32)
    for gi in range(1, N_GROUPS):
        upd = group_scores[gi] > best
        best = jnp.where(upd, group_scores[gi], best)
        best_idx = jnp.where(upd, gi, best_idx)
    pick = lambda rows, j: functools.reduce(
        lambda acc, gi: jnp.where(best_idx == gi, rows[gi * npg + j], acc), range(1, N_GROUPS), rows[j])
    bb = [pick(b_rows, j) for j in range(npg)]
    ss = [pick(s_rows, j) for j in range(npg)]
    weights = []
    for j in range(npg):
        rank = jnp.zeros(best.shape, jnp.int32)
        for i in range(npg):
            if i == j:
                continue
            beats = (bb[i] > bb[j]) | ((bb[i] == bb[j]) & (i < j)) if i < j else (bb[i] > bb[j])
            rank = rank + beats.astype(jnp.int32)
        weights.append(jnp.where(rank < 2, ss[j], 0.0))
    den = weights[0] + weights[1] + weights[2] + weights[3]
    gates = [wj / den for wj in weights]
    return [jnp.where(best_idx == e // npg, gates[e % npg], 0.0) for e in range(N_EXPERTS)]


def _moe_kernel(x_ref, mod_ref, pre_ref, post_ref, rw_ref, rb_ref, wgu_ref, wd_ref, o_ref, h_scr, gate_scr, acc_scr):
    e = pl.program_id(1)
    n_e = pl.num_programs(1)
    tm = x_ref.shape[0]

    @pl.when(e == 0)
    def _():
        h = _rms(x_ref[...], pre_ref[...]) * (1.0 + mod_ref[0, 4:5, :]) + mod_ref[0, 3:4, :]
        h_scr[...] = h.astype(BF16)
        logits_t = lax.dot_general(rw_ref[...], h, (((1,), (1,)), ((), ())), precision=HIGHEST,
                                   preferred_element_type=F32)
        rows = _route(logits_t, rb_ref[...])
        sub = lax.broadcasted_iota(jnp.int32, (LANES, tm), 0)
        gates_t = jnp.where(sub == N_EXPERTS, 1.0, 0.0)
        for k in range(N_EXPERTS):
            gates_t = jnp.where(sub == k, rows[k], gates_t)
        gate_scr[...] = gates_t.T
        acc_scr[...] = jnp.zeros_like(acc_scr)

    lane = lax.broadcasted_iota(jnp.int32, (tm, LANES), 1)
    gate = jnp.sum(jnp.where(lane == e, gate_scr[...], 0.0), axis=1, keepdims=True)
    gu = _dot(h_scr[...], wgu_ref[0])
    g_part = gu[:, 0:EXPERT_HIDDEN]
    act = g_part * jax.nn.sigmoid(g_part) * gu[:, EXPERT_HIDDEN:] * gate
    acc_scr[...] += _dot(act.astype(BF16), wd_ref[0])

    @pl.when(e == n_e - 1)
    def _():
        o_ref[...] = x_ref[...] + mod_ref[0, 5:6, :] * _rms(acc_scr[...], post_ref[...])


def _moe(x_all, mods, pre_g, post_g, router_wt, router_b, wgu, wd, *, rows, n_lat, seq, tm):
    lat_tiles = n_lat // tm
    n_batch = n_lat // seq
    n_e = wgu.shape[0]

    def mod_idx(i, e):
        return (jnp.where(i < lat_tiles, (i * tm) // seq, n_batch), 0, 0)

    const = lambda i, e: (0, 0)
    row = lambda i, e: (i, 0)
    return pl.pallas_call(
        _moe_kernel,
        out_shape=jax.ShapeDtypeStruct((rows, D_MODEL), F32),
        grid=(rows // tm, n_e),
        in_specs=[
            pl.BlockSpec((tm, D_MODEL), row),
            pl.BlockSpec((1, N_MOD, D_MODEL), mod_idx),
            pl.BlockSpec((1, D_MODEL), const),
            pl.BlockSpec((1, D_MODEL), const),
            pl.BlockSpec((LANES, D_MODEL), const),
            pl.BlockSpec((N_EXPERTS, 1), const),
            pl.BlockSpec((1, D_MODEL, 2 * EXPERT_HIDDEN), lambda i, e: (e, 0, 0)),
            pl.BlockSpec((1, EXPERT_HIDDEN, D_MODEL), lambda i, e: (e, 0, 0)),
        ],
        out_specs=pl.BlockSpec((tm, D_MODEL), row),
        scratch_shapes=[
            pltpu.VMEM((tm, D_MODEL), BF16),
            pltpu.VMEM((tm, LANES), F32),
            pltpu.VMEM((tm, D_MODEL), F32),
        ],
        compiler_params=_params(("parallel", "arbitrary")),
        name="moe",
    )(x_all, mods, pre_g, post_g, router_wt, router_b, wgu, wd)


def _pack_in_proj(w_in, shift_mu):
    d = w_in.shape[0]
    z = lambda n: jnp.zeros((d, n), w_in.dtype)
    kr = w_in[:, Q_RANK + KV_RANK:MLA_COLS]
    half = MLA_ROPE // 2
    kr_sw = jnp.concatenate([kr[:, half:], kr[:, :half]], axis=1)
    pad = LANES - MLA_NOPE - MLA_ROPE
    packed = jnp.concatenate([
        w_in[:, :Q_RANK + KV_RANK],
        z(MLA_NOPE), kr, z(pad),
        z(MLA_NOPE), kr_sw, z(pad),
        w_in[:, MLA_COLS:], z(RWKV_PAD - RWKV_COLS),
    ], axis=1)
    mu = jnp.pad(shift_mu, ((0, 0), (0, RWKV_PAD - RWKV_COLS)))
    return packed.astype(BF16), mu


def _pack_mla(w_uq, w_ukv):
    half = MLA_ROPE // 2
    dq = MLA_NOPE + MLA_ROPE
    q3 = w_uq.reshape(Q_RANK, HEADS, dq)
    zq = lambda n: jnp.zeros((Q_RANK, HEADS, n), w_uq.dtype)
    pad = HEAD_SLOT - dq
    wq1 = jnp.concatenate([q3, zq(pad)], axis=2).reshape(Q_RANK, HEADS * HEAD_SLOT)
    wq2 = jnp.concatenate([zq(MLA_NOPE), q3[:, :, MLA_NOPE + half:], q3[:, :, MLA_NOPE:MLA_NOPE + half], zq(pad)],
                          axis=2).reshape(Q_RANK, HEADS * HEAD_SLOT)
    kv3 = w_ukv.reshape(KV_RANK, HEADS, MLA_NOPE + MLA_V)
    wuk = jnp.concatenate([kv3[:, :, :MLA_NOPE], jnp.zeros((KV_RANK, HEADS, HEAD_SLOT - MLA_NOPE), w_ukv.dtype)],
                          axis=2).reshape(KV_RANK, HEADS * HEAD_SLOT)
    wuv = kv3[:, :, MLA_NOPE:].reshape(KV_RANK, HEADS * MLA_V)
    return wq1.astype(BF16), wq2.astype(BF16), wuk.astype(BF16), wuv.astype(BF16)


def _block_diag2(m):
    r, c = m.shape[1], m.shape[2]
    z = jnp.zeros((r, c), m.dtype)
    return jnp.concatenate([jnp.concatenate([m[0], z], axis=1), jnp.concatenate([z, m[1]], axis=1)], axis=0)


def _rope_tables(seq, ctx_len):
    axis_dim = MLA_ROPE // 2
    t = jnp.arange(seq, dtype=jnp.int32)
    row = (t // GRID_W).astype(F32)
    col = (t % GRID_W).astype(F32)
    inv_freq = ROPE_BASE ** (-jnp.arange(0, axis_dim, 2, dtype=F32) / axis_dim)
    ang = jnp.concatenate([row[:, None] * inv_freq, col[:, None] * inv_freq], axis=-1)
    cos = jnp.concatenate([jnp.cos(ang), jnp.ones((ctx_len, axis_dim), F32)], axis=0)
    sin = jnp.concatenate([jnp.sin(ang), jnp.zeros((ctx_len, axis_dim), F32)], axis=0)
    n = seq + ctx_len
    ones = jnp.ones((n, MLA_NOPE), F32)
    z_nope = jnp.zeros((n, MLA_NOPE), F32)
    z_pad = jnp.zeros((n, HEAD_SLOT - MLA_NOPE - MLA_ROPE), F32)
    cq = jnp.concatenate([ones, cos, cos, z_pad], axis=1)
    sq = jnp.concatenate([z_nope, -sin, sin, z_pad], axis=1)
    ck = jnp.concatenate([z_nope, cos, cos, z_pad], axis=1)
    return cq, sq, ck


def kernel(x, c, ctx, c_ctx, ada_w, ada_b, mix_pre_g, mix_post_g, ffn_pre_g, ffn_post_g, w_in, q_norm_g, kv_norm_g,
           w_uq, w_ukv, shift_mu, decay_w0, decay_w2, iclr_a0, iclr_a2, gate_g2, k_k, k_a, r_k, lnx_g, lnx_b, w_out,
           router_w, router_bias, exp_w_gate, exp_w_up, exp_w_down, sh_w_gate, sh_w_up, sh_w_down):
    n_batch, seq, d = x.shape
    ctx_len = ctx.shape[1]
    depth = ada_w.shape[0]
    assert d == D_MODEL and seq % TOKEN_TILE == 0 and ctx_len % TOKEN_TILE == 0 and seq % ctx_len == 0
    assert seq % GRID_W == 0 and n_batch + 1 <= 8
    n_lat = n_batch * seq
    n_ctx = n_batch * ctx_len
    moe_tile = 1024 if (n_lat % 1024 == 0 and n_ctx % 1024 == 0 and seq % 1024 == 0) else TOKEN_TILE

    x_all = jnp.concatenate([x.reshape(n_lat, d), ctx.reshape(n_ctx, d)], axis=0)
    cond_rows = jnp.concatenate([c, c_ctx[None], jnp.zeros((8 - n_batch - 1, d), F32)], axis=0)
    mods_all = _ada_modulation(cond_rows, ada_w, ada_b).reshape(depth, 8, N_MOD, d)

    cq_t, sq_t, ck_t = _rope_tables(seq, ctx_len)
    w = RWKV_WIDTH
    ones_bd = (jnp.arange(w)[:, None] // RWKV_HEAD == jnp.arange(w)[None, :] // RWKV_HEAD).astype(BF16)
    router_wt = jnp.pad(router_w.T, ((0, LANES - N_EXPERTS), (0, 0)))
    router_b = router_bias.reshape(N_EXPERTS, 1)
    row1 = lambda a: a.reshape(1, -1)

    for l in range(depth):
        ctx_out = l < depth - 1
        mods = mods_all[l]
        rows = n_lat + n_ctx if ctx_out else n_lat

        w_in_p, mu = _pack_in_proj(w_in[l], shift_mu[l])
        wq1, wq2, wuk, wuv = _pack_mla(w_uq[l], w_ukv[l])
        q, k, v, p_rw = _in_proj(x_all, mods, row1(mix_pre_g[l]), w_in_p, row1(q_norm_g[l]), row1(kv_norm_g[l]),
                                 wq1, wq2, wuk, wuv, cq_t, sq_t, ck_t, n_lat=n_lat, seq=seq, ctx_len=ctx_len)
        att = _attention(q, k, v, n_batch=n_batch, seq=seq, ctx_len=ctx_len, ctx_out=ctx_out)

        g2p = jnp.pad(gate_g2[l], ((0, GATE_PAD - GATE_LORA), (0, 0))).astype(BF16)
        r, vv, kk, ke0, ke1, lw0, lw1, b0, b1, g, bon = _rwkv_prep(
            p_rw, mu, _block_diag2(decay_w2[l]).astype(BF16), decay_w0[l].reshape(1, 2 * w),
            _block_diag2(iclr_a2[l]).astype(BF16), iclr_a0[l].reshape(1, 2 * w), g2p,
            row1(k_k[l]), row1(k_a[l]), row1(r_k[l]), ones_bd, n_lat=n_lat, seq=seq, ctx_len=ctx_len)
        yf = _rwkv_scan(r, vv, kk, ke0, lw0, b0, n_batch=n_batch, seq=seq, ctx_len=ctx_len, rev=False)
        yb = _rwkv_scan(r, vv, kk, ke1, lw1, b1, n_batch=n_batch, seq=seq, ctx_len=ctx_len, rev=True)

        wo = w_out[l].astype(BF16)
        x_all = _mix_out(yf, yb, att, g, bon, row1(lnx_g[l]), row1(lnx_b[l]), ones_bd, wo[:w], wo[w:], x_all, mods,
                         row1(mix_post_g[l]), rows=rows, n_lat=n_lat, seq=seq)

        wgu = jnp.concatenate([
            jnp.concatenate([exp_w_gate[l], exp_w_up[l]], axis=2),
            jnp.concatenate([sh_w_gate[l], sh_w_up[l]], axis=1)[None]], axis=0).astype(BF16)
        wd = jnp.concatenate([exp_w_down[l], sh_w_down[l][None]], axis=0).astype(BF16)
        x_all = _moe(x_all, mods, row1(ffn_pre_g[l]), row1(ffn_post_g[l]), router_wt, router_b, wgu, wd,
                     rows=rows, n_lat=n_lat, seq=seq, tm=moe_tile)

    return x_all[:n_lat].reshape(n_batch, seq, d)
```

```python
import functools
import math

import jax
import jax.numpy as jnp
from jax import lax
from jax.experimental import pallas as pl
from jax.experimental.pallas import tpu as pltpu

F32 = jnp.float32
BF16 = jnp.bfloat16
HIGHEST = lax.Precision.HIGHEST

D_MODEL = 1024
N_MOD = 6
NORM_EPS = 1e-6
GRID_W = 64
ROPE_BASE = 10000.0

HEADS = 8
MLA_NOPE = 64
MLA_ROPE = 32
MLA_V = 64
Q_RANK = 256
KV_RANK = 128
MLA_COLS = Q_RANK + KV_RANK + MLA_ROPE

RWKV_HEAD = 64
RWKV_WIDTH = HEADS * RWKV_HEAD
DECAY_LORA = 64
ICLR_LORA = 64
GATE_LORA = 160
RWKV_COLS = 3 * RWKV_WIDTH + 2 * DECAY_LORA + 2 * ICLR_LORA + GATE_LORA
RWKV_PAD = 2048
GATE_PAD = 256
LNX_EPS = 64e-5

N_EXPERTS = 16
N_GROUPS = 4
EXPERTS_PER_GROUP = 4
EXPERT_HIDDEN = 256

LANES = 128
HEAD_SLOT = 128
IN_PACKED = Q_RANK + KV_RANK + 2 * LANES + RWKV_PAD
TOKEN_TILE = 256
CHUNK = 64
VMEM_LIMIT = 48 * 1024 * 1024


def _dot(a, b):
    return jnp.dot(a, b, preferred_element_type=F32)


def _dot_nt(a, b):
    return lax.dot_general(a, b, (((1,), (1,)), ((), ())), preferred_element_type=F32)


def _dot_tn(a, b):
    return lax.dot_general(a, b, (((0,), (0,)), ((), ())), preferred_element_type=F32)


def _rms(x, g):
    return x * lax.rsqrt(jnp.mean(x * x, axis=-1, keepdims=True) + NORM_EPS) * g


def _seg_sum(x, ones_bd):
    hi = x.astype(BF16)
    lo = (x - hi.astype(F32)).astype(BF16)
    return _dot(hi, ones_bd) + _dot(lo, ones_bd)


def _params(sem):
    return pltpu.CompilerParams(dimension_semantics=sem, vmem_limit_bytes=VMEM_LIMIT)


def _ada_kernel(c_ref, w_ref, b_ref, o_ref):
    c = c_ref[...]
    cond = c * jax.nn.sigmoid(c)
    o_ref[0] = jnp.dot(cond, w_ref[0], precision=HIGHEST, preferred_element_type=F32) + b_ref[0]


def _ada_modulation(cond_rows, ada_w, ada_b):
    depth, d, n = ada_w.shape
    tn = 1536
    rows = cond_rows.shape[0]
    return pl.pallas_call(
        _ada_kernel,
        out_shape=jax.ShapeDtypeStruct((depth, rows, n), F32),
        grid=(depth, n // tn),
        in_specs=[
            pl.BlockSpec((rows, d), lambda l, j: (0, 0)),
            pl.BlockSpec((1, d, tn), lambda l, j: (l, 0, j)),
            pl.BlockSpec((1, 1, tn), lambda l, j: (l, 0, j)),
        ],
        out_specs=pl.BlockSpec((1, rows, tn), lambda l, j: (l, 0, j)),
        compiler_params=_params(("parallel", "parallel")),
        name="ada_modulation",
    )(cond_rows, ada_w, ada_b.reshape(depth, 1, n))


def _in_proj_kernel(x_ref, mod_ref, g_ref, win_ref, qng_ref, kvng_ref, wq1_ref, wq2_ref, wuk_ref, wuv_ref,
                    cq_ref, sq_ref, ck_ref, q_out, k_out, v_out, p_out, *, q_scale):
    x = x_ref[...]
    shift = mod_ref[0, 0:1, :]
    scale = mod_ref[0, 1:2, :]
    h = _rms(x, g_ref[...]) * (1.0 + scale) + shift
    p = _dot(h.astype(BF16), win_ref[...])
    c_q = p[:, 0:Q_RANK]
    c_kv = p[:, Q_RANK:Q_RANK + KV_RANK]
    kr_a = p[:, Q_RANK + KV_RANK:Q_RANK + KV_RANK + LANES]
    kr_b = p[:, Q_RANK + KV_RANK + LANES:Q_RANK + KV_RANK + 2 * LANES]
    p_out[...] = p[:, Q_RANK + KV_RANK + 2 * LANES:]

    tile8 = lambda t: jnp.concatenate([t] * HEADS, axis=1)
    cqn = _rms(c_q, qng_ref[...]).astype(BF16)
    q = _dot(cqn, wq1_ref[...]) * tile8(cq_ref[...]) + _dot(cqn, wq2_ref[...]) * tile8(sq_ref[...])
    q_out[...] = (q * q_scale).astype(BF16)

    ckvn = _rms(c_kv, kvng_ref[...]).astype(BF16)
    k_rot = kr_a * ck_ref[...] + kr_b * sq_ref[...]
    k_out[...] = (_dot(ckvn, wuk_ref[...]) + tile8(k_rot)).astype(BF16)
    v_out[...] = _dot(ckvn, wuv_ref[...]).astype(BF16)


def _in_proj(x_all, mods, g, w_in_p, qng, kvng, wq1, wq2, wuk, wuv, cq_t, sq_t, ck_t, *, n_lat, seq, ctx_len):
    nt = x_all.shape[0]
    tm = TOKEN_TILE
    lat_tiles = n_lat // tm
    n_batch = n_lat // seq

    def mod_idx(i):
        return (jnp.where(i < lat_tiles, (i * tm) // seq, n_batch), 0, 0)

    def tab_idx(i):
        return (jnp.where(i < lat_tiles, i % (seq // tm), seq // tm + (i - lat_tiles) % (ctx_len // tm)), 0)

    const = lambda i: (0, 0)
    row = lambda i: (i, 0)
    qw = HEADS * HEAD_SLOT
    return pl.pallas_call(
        functools.partial(_in_proj_kernel, q_scale=float((MLA_NOPE + MLA_ROPE) ** -0.5)),
        out_shape=(
            jax.ShapeDtypeStruct((nt, qw), BF16),
            jax.ShapeDtypeStruct((nt, qw), BF16),
            jax.ShapeDtypeStruct((nt, HEADS * MLA_V), BF16),
            jax.ShapeDtypeStruct((nt, RWKV_PAD), F32),
        ),
        grid=(nt // tm,),
        in_specs=[
            pl.BlockSpec((tm, D_MODEL), row),
            pl.BlockSpec((1, N_MOD, D_MODEL), mod_idx),
            pl.BlockSpec((1, D_MODEL), const),
            pl.BlockSpec((D_MODEL, IN_PACKED), const),
            pl.BlockSpec((1, Q_RANK), const),
            pl.BlockSpec((1, KV_RANK), const),
            pl.BlockSpec((Q_RANK, qw), const),
            pl.BlockSpec((Q_RANK, qw), const),
            pl.BlockSpec((KV_RANK, qw), const),
            pl.BlockSpec((KV_RANK, HEADS * MLA_V), const),
            pl.BlockSpec((tm, HEAD_SLOT), tab_idx),
            pl.BlockSpec((tm, HEAD_SLOT), tab_idx),
            pl.BlockSpec((tm, HEAD_SLOT), tab_idx),
        ],
        out_specs=(
            pl.BlockSpec((tm, qw), row),
            pl.BlockSpec((tm, qw), row),
            pl.BlockSpec((tm, HEADS * MLA_V), row),
            pl.BlockSpec((tm, RWKV_PAD), row),
        ),
        compiler_params=_params(("parallel",)),
        name="in_proj",
    )(x_all, mods, g, w_in_p, qng, kvng, wq1, wq2, wuk, wuv, cq_t, sq_t, ck_t)


def _attn_kernel(q_ref, kl_ref, kc_ref, vl_ref, vc_ref, o_ref, *, lat_q_tiles):
    j = pl.program_id(2)

    def compute(use_lat):
        outs = []
        for h in range(2):
            hs = slice(h * HEAD_SLOT, (h + 1) * HEAD_SLOT)
            q = q_ref[:, hs]
            s_c = _dot_nt(q, kc_ref[:, hs])
            m = jnp.max(s_c, axis=1, keepdims=True)
            if use_lat:
                s_l = _dot_nt(q, kl_ref[:, hs])
                m = jnp.maximum(m, jnp.max(s_l, axis=1, keepdims=True))
                p_l = jnp.exp(s_l - m)
                den = jnp.sum(p_l, axis=1, keepdims=True)
                acc = _dot(p_l.astype(BF16), vl_ref[...])
            p_c = jnp.exp(s_c - m)
            if use_lat:
                den = den + jnp.sum(p_c, axis=1, keepdims=True)
                acc = acc + _dot(p_c.astype(BF16), vc_ref[...])
            else:
                den = jnp.sum(p_c, axis=1, keepdims=True)
                acc = _dot(p_c.astype(BF16), vc_ref[...])
            outs.append(acc / den)
        lane = lax.broadcasted_iota(jnp.int32, outs[0].shape, 1)
        o_ref[...] = jnp.where(lane < MLA_V, outs[0], outs[1]).astype(o_ref.dtype)

    @pl.when(j < lat_q_tiles)
    def _():
        compute(True)

    @pl.when(j >= lat_q_tiles)
    def _():
        compute(False)


def _attention(q, k, v, *, n_batch, seq, ctx_len, ctx_out):
    nt = q.shape[0]
    tq = TOKEN_TILE
    lat_q = seq // tq
    ctx_q = ctx_len // tq if ctx_out else 0
    lat_tiles_all = n_batch * lat_q
    ctx_blk0 = (n_batch * seq) // ctx_len

    def q_idx(b, hp, j):
        return (jnp.where(j < lat_q, b * lat_q + j, lat_tiles_all + b * (ctx_len // tq) + (j - lat_q)), hp)

    return pl.pallas_call(
        functools.partial(_attn_kernel, lat_q_tiles=lat_q),
        out_shape=jax.ShapeDtypeStruct((nt, HEADS * MLA_V), BF16),
        grid=(n_batch, HEADS // 2, lat_q + ctx_q),
        in_specs=[
            pl.BlockSpec((tq, 2 * HEAD_SLOT), q_idx),
            pl.BlockSpec((seq, 2 * HEAD_SLOT), lambda b, hp, j: (b, hp)),
            pl.BlockSpec((ctx_len, 2 * HEAD_SLOT), lambda b, hp, j: (ctx_blk0 + b, hp)),
            pl.BlockSpec((seq, 2 * MLA_V), lambda b, hp, j: (b, hp)),
            pl.BlockSpec((ctx_len, 2 * MLA_V), lambda b, hp, j: (ctx_blk0 + b, hp)),
        ],
        out_specs=pl.BlockSpec((tq, 2 * MLA_V), q_idx),
        compiler_params=_params(("parallel", "parallel", "arbitrary")),
        name="attention",
    )(q, k, k, v, v)


def _split3(x):
    hi = x.astype(BF16)
    r1 = x - hi.astype(F32)
    mid = r1.astype(BF16)
    lo = (r1 - mid.astype(F32)).astype(BF16)
    return hi, mid, lo


def _chunk_transpose(x):
    n = CHUNK
    xt = x.T
    rows = []
    for c in range(x.shape[0] // n):
        rows.append(jnp.concatenate(
            [xt[h * n:(h + 1) * n, c * n:(c + 1) * n] for h in range(x.shape[1] // n)], axis=1))
    return jnp.concatenate(rows, axis=0)


def _rwkv_prep_kernel(p_ref, hp_ref, hn_ref, mu_ref, w2_ref, w0_ref, a2_ref, a0_ref, g2_ref, kk_ref, ka_ref, rk_ref,
                      ones_ref, trif_ref, trir_ref, onesc_ref, v_out, rt0_out, kt0_out, bt0_out, kb0_out, pc0_out,
                      rt1_out, kt1_out, bt1_out, kb1_out, pc1_out, g_out, bon_out, *, lat_tiles, seq_tiles,
                      ctx_tiles):
    i = pl.program_id(0)
    tm = p_ref.shape[0]
    is_lat = i < lat_tiles
    local = jnp.where(is_lat, i % seq_tiles, (i - lat_tiles) % ctx_tiles)
    seg = jnp.where(is_lat, seq_tiles, ctx_tiles)
    keep_prev = jnp.where(local == 0, 0.0, 1.0)
    keep_next = jnp.where(local == seg - 1, 0.0, 1.0)

    p = p_ref[...]
    row = lax.broadcasted_iota(jnp.int32, (tm, 1), 0)
    prev = jnp.where(row == 0, hp_ref[7:8, :] * keep_prev, pltpu.roll(p, 1, 0))
    nxt = jnp.where(row == tm - 1, hn_ref[0:1, :] * keep_next, pltpu.roll(p, tm - 1, 0))
    ps = p + mu_ref[0:1, :] * (prev - p) + mu_ref[1:2, :] * (nxt - p)

    w = RWKV_WIDTH
    r = ps[:, 0:w]
    k = ps[:, w:2 * w]
    v = ps[:, 2 * w:3 * w]
    wl = ps[:, 3 * w:3 * w + 2 * DECAY_LORA]
    al = ps[:, 3 * w + 2 * DECAY_LORA:3 * w + 2 * DECAY_LORA + 2 * ICLR_LORA]
    gl = ps[:, 3 * w + 2 * DECAY_LORA + 2 * ICLR_LORA:3 * w + 2 * DECAY_LORA + 2 * ICLR_LORA + GATE_PAD]

    g_out[...] = _dot(jax.nn.sigmoid(gl).astype(BF16), g2_ref[...])
    z = w0_ref[...] + _dot(jnp.tanh(wl).astype(BF16), w2_ref[...])
    logw = (-math.exp(-0.5)) * jax.nn.sigmoid(z)
    a = jax.nn.sigmoid(a0_ref[...] + _dot(al.astype(BF16), a2_ref[...]))

    ones_bd = ones_ref[...]
    kk = k * kk_ref[...]
    kk = kk * lax.rsqrt(_seg_sum(kk * kk, ones_bd) + 1e-12)
    ka = ka_ref[...]
    v_out[...] = v.astype(BF16)

    ke_sum = None
    for d, tri_ref, (rt_out, kt_out, bt_out, kb_out, pc_out) in (
            (0, trif_ref, (rt0_out, kt0_out, bt0_out, kb0_out, pc0_out)),
            (1, trir_ref, (rt1_out, kt1_out, bt1_out, kb1_out, pc1_out))):
        lw = logw[:, d * w:(d + 1) * w]
        a_d = a[:, d * w:(d + 1) * w]
        parts = _split3(lw)
        cum = _dot(tri_ref[...], parts[0]) + _dot(tri_ref[...], parts[1]) + _dot(tri_ref[...], parts[2])
        tot = _dot(onesc_ref[...], parts[0]) + _dot(onesc_ref[...], parts[1]) + _dot(onesc_ref[...], parts[2])
        e_neg = jnp.exp(-cum)
        ke = k * (1.0 + (a_d - 1.0) * ka)
        ke_sum = ke if ke_sum is None else ke_sum + ke
        rt_out[...] = (r * jnp.exp(cum)).astype(BF16)
        kt_out[...] = (kk * jnp.exp(cum - lw)).astype(BF16)
        bt_out[...] = _chunk_transpose(a_d * kk * e_neg).astype(BF16)
        kb_out[...] = _chunk_transpose(ke * e_neg).astype(BF16)
        pc_out[...] = _chunk_transpose(jnp.exp(tot))
    bon_out[...] = _seg_sum(r * ke_sum * rk_ref[...], ones_bd) * v


def _rwkv_prep(p_rw, mu, w2bd, w0, a2bd, a0, g2p, k_k, k_a, r_k, ones_bd, tri_f, tri_r, ones_c, *, n_lat, seq,
               ctx_len):
    nt = p_rw.shape[0]
    tm = TOKEN_TILE
    n_tiles = nt // tm
    halo = 8
    blocks8 = nt // halo
    const = lambda i: (0, 0)
    row = lambda i: (i, 0)
    w = RWKV_WIDTH
    o16 = jax.ShapeDtypeStruct((nt, w), BF16)
    o32 = jax.ShapeDtypeStruct((nt, w), F32)
    return pl.pallas_call(
        functools.partial(_rwkv_prep_kernel, lat_tiles=n_lat // tm, seq_tiles=seq // tm, ctx_tiles=ctx_len // tm),
        out_shape=(o16, o16, o16, o16, o16, o32, o16, o16, o16, o16, o32, o32, o32),
        grid=(n_tiles,),
        in_specs=[
            pl.BlockSpec((tm, RWKV_PAD), row),
            pl.BlockSpec((halo, RWKV_PAD), lambda i: (jnp.maximum(i * (tm // halo) - 1, 0), 0)),
            pl.BlockSpec((halo, RWKV_PAD), lambda i: (jnp.minimum((i + 1) * (tm // halo), blocks8 - 1), 0)),
            pl.BlockSpec((2, RWKV_PAD), const),
            pl.BlockSpec((2 * DECAY_LORA, 2 * w), const),
            pl.BlockSpec((1, 2 * w), const),
            pl.BlockSpec((2 * ICLR_LORA, 2 * w), const),
            pl.BlockSpec((1, 2 * w), const),
            pl.BlockSpec((GATE_PAD, w), const),
            pl.BlockSpec((1, w), const),
            pl.BlockSpec((1, w), const),
            pl.BlockSpec((1, w), const),
            pl.BlockSpec((w, w), const),
            pl.BlockSpec((tm, tm), const),
            pl.BlockSpec((tm, tm), const),
            pl.BlockSpec((tm, tm), const),
        ],
        out_specs=(pl.BlockSpec((tm, w), row),) * 13,
        compiler_params=_params(("parallel",)),
        name="rwkv_prep",
    )(p_rw, p_rw, p_rw, mu, w2bd, w0, a2bd, a0, g2p, k_k, k_a, r_k, ones_bd, tri_f, tri_r, ones_c)


QUAD = 4
QUAD_W = QUAD * RWKV_HEAD


def _scan_kernel(v0_ref, rt0_ref, kt0_ref, bt0_ref, kb0_ref, pc0_ref, v1_ref, rt1_ref, kt1_ref, bt1_ref, kb1_ref,
                 pc1_ref, y0_ref, y1_ref, h_ref):
    @pl.when(pl.program_id(1) == 0)
    def _():
        h_ref[...] = jnp.zeros_like(h_ref)

    n, qw = CHUNK, QUAD_W
    t_i = lax.broadcasted_iota(jnp.int32, (n, qw), 0)
    s_i = lax.broadcasted_iota(jnp.int32, (n, qw), 1) % n
    bd_mask = (lax.broadcasted_iota(jnp.int32, (qw, qw), 0) // n) == (lax.broadcasted_iota(jnp.int32, (qw, qw), 1) // n)
    eye = jnp.where(t_i == s_i, 1.0, 0.0)
    same16 = (t_i // 16) == (s_i // 16)
    same32 = (t_i // 32) == (s_i // 32)
    sib32 = jnp.logical_and(same32, jnp.logical_not(same16))
    sib64 = jnp.logical_not(same32)
    incl = (s_i <= t_i, s_i >= t_i)
    strict = (s_i < t_i, s_i > t_i)

    groups = [(d, q) for d in (0, 1) for q in range(HEADS // QUAD)]
    dirs = [d for d, _ in groups]
    ng = range(len(groups))
    refs = ((v0_ref, rt0_ref, kt0_ref, bt0_ref, kb0_ref, pc0_ref), (v1_ref, rt1_ref, kt1_ref, bt1_ref, kb1_ref, pc1_ref))
    ld = lambda k: [refs[d][k][:, q * qw:(q + 1) * qw] for d, q in groups]
    v, rt, kt, bt_t, kb_t, pc = (ld(k) for k in range(6))

    def bd(x):
        return jnp.where(bd_mask, jnp.concatenate([x.astype(BF16)] * QUAD, axis=0), jnp.zeros((), BF16))

    mm = lambda x, w: _dot(x.astype(BF16), w)
    rows2 = lambda x, y: jnp.concatenate([x.astype(BF16), y.astype(BF16)], axis=0)

    v_bd = [bd(v[g]) for g in ng]
    lhs = [rows2(kt[g], rt[g]) for g in ng]
    g1 = [_dot(lhs[g], bd(bt_t[g])) for g in ng]
    g2 = [_dot(lhs[g], bd(kb_t[g])) for g in ng]
    a_m = [jnp.where(strict[dirs[g]], g1[g][0:n], 0.0) for g in ng]
    c_m = [jnp.where(incl[dirs[g]], g1[g][n:2 * n], 0.0) for g in ng]
    b_m = [jnp.where(strict[dirs[g]], g2[g][0:n], 0.0) for g in ng]
    e_m = [jnp.where(incl[dirs[g]], g2[g][n:2 * n], 0.0) for g in ng]

    dd = [jnp.where(same16, a_m[g], 0.0) for g in ng]
    dd_bd = [bd(dd[g]) for g in ng]
    d2 = [mm(dd[g], dd_bd[g]) for g in ng]
    t = [eye - dd[g] for g in ng]
    d2_bd = [bd(d2[g]) for g in ng]
    t = [t[g] + mm(t[g], d2_bd[g]) for g in ng]
    d4 = [mm(d2[g], d2_bd[g]) for g in ng]
    d4_bd = [bd(d4[g]) for g in ng]
    t = [t[g] + mm(t[g], d4_bd[g]) for g in ng]
    d8 = [mm(d4[g], d4_bd[g]) for g in ng]
    t = [t[g] + mm(t[g], bd(d8[g])) for g in ng]
    for sib in (sib32, sib64):
        ta = [mm(t[g], bd(jnp.where(sib, a_m[g], 0.0))) for g in ng]
        t = [t[g] - mm(ta[g], bd(t[g])) for g in ng]

    bev = [_dot(rows2(b_m[g], e_m[g]), v_bd[g]) for g in ng]
    m1 = [mm(t[g], bd(kt[g])) for g in ng]
    w2 = [mm(t[g], bd(bev[g][0:n])) for g in ng]
    qh = [rt[g].astype(F32) - mm(c_m[g], bd(m1[g])) for g in ng]
    y0 = [bev[g][n:2 * n] - mm(c_m[g], bd(w2[g])) for g in ng]
    kv = [_dot(kb_t[g], v_bd[g]) for g in ng]

    h0 = [h_ref[g] for g in ng]
    uy = [_dot(rows2(m1[g], qh[g]), bd(h0[g])) for g in ng]
    u = [uy[g][0:n] + w2[g] for g in ng]
    for g, (d, q) in enumerate(groups):
        (y0_ref, y1_ref)[d][:, q * qw:(q + 1) * qw] = uy[g][n:2 * n] + y0[g]
        h_ref[g] = pc[g] * (h0[g] - _dot(bt_t[g], bd(u[g])) + kv[g])


def _rwkv_scan(v, ops0, ops1, *, n_batch, seq, ctx_len):
    nt = v.shape[0]
    n = CHUNK
    lat_c = seq // n
    ctx_c = ctx_len // n
    ctx0 = n_batch * lat_c

    def idx_f(bi, c):
        return (jnp.where(c < ctx_c, ctx0 + bi * ctx_c + c, bi * lat_c + (c - ctx_c)), 0)

    def idx_r(bi, c):
        return (jnp.where(c < ctx_c, ctx0 + bi * ctx_c + (ctx_c - 1 - c), bi * lat_c + (lat_c - 1 - (c - ctx_c))), 0)

    spec_f = pl.BlockSpec((n, RWKV_WIDTH), idx_f)
    spec_r = pl.BlockSpec((n, RWKV_WIDTH), idx_r)
    out = jax.ShapeDtypeStruct((nt, RWKV_WIDTH), F32)
    return pl.pallas_call(
        _scan_kernel,
        out_shape=(out, out),
        grid=(n_batch, ctx_c + lat_c),
        in_specs=[spec_f] * 6 + [spec_r] * 6,
        out_specs=(spec_f, spec_r),
        scratch_shapes=[pltpu.VMEM((2 * HEADS // QUAD, n, QUAD_W), F32)],
        compiler_params=_params(("parallel", "arbitrary")),
        name="rwkv_scan",
    )(v, *ops0, v, *ops1)


def _mix_out_kernel(yf_ref, yb_ref, att_ref, g_ref, bon_ref, lng_ref, lnb_ref, ones_ref, woa_ref, wor_ref, x_ref,
                    mod_ref, pg_ref, o_ref):
    ones_bd = ones_ref[...]
    y = yf_ref[...] + yb_ref[...]
    inv_n = 1.0 / RWKV_HEAD
    mu = _seg_sum(y, ones_bd) * inv_n
    yc = y - mu
    var = _seg_sum(yc * yc, ones_bd) * inv_n
    yn = yc * lax.rsqrt(var + LNX_EPS) * lng_ref[...] + lnb_ref[...]
    rw = (yn + bon_ref[...]) * g_ref[...]
    mix = _dot(att_ref[...], woa_ref[...]) + _dot(rw.astype(BF16), wor_ref[...])
    o_ref[...] = x_ref[...] + mod_ref[0, 2:3, :] * _rms(mix, pg_ref[...])


def _mix_out(yf, yb, att, g, bon, lnx_g, lnx_b, ones_bd, wo_att, wo_rw, x_all, mods, post_g, *, rows, n_lat, seq):
    tm = TOKEN_TILE
    lat_tiles = n_lat // tm
    n_batch = n_lat // seq
    w = RWKV_WIDTH

    def mod_idx(i):
        return (jnp.where(i < lat_tiles, (i * tm) // seq, n_batch), 0, 0)

    const = lambda i: (0, 0)
    row = lambda i: (i, 0)
    return pl.pallas_call(
        _mix_out_kernel,
        out_shape=jax.ShapeDtypeStruct((rows, D_MODEL), F32),
        grid=(rows // tm,),
        in_specs=[
            pl.BlockSpec((tm, w), row),
            pl.BlockSpec((tm, w), row),
            pl.BlockSpec((tm, w), row),
            pl.BlockSpec((tm, w), row),
            pl.BlockSpec((tm, w), row),
            pl.BlockSpec((1, w), const),
            pl.BlockSpec((1, w), const),
            pl.BlockSpec((w, w), const),
            pl.BlockSpec((w, D_MODEL), const),
            pl.BlockSpec((w, D_MODEL), const),
            pl.BlockSpec((tm, D_MODEL), row),
            pl.BlockSpec((1, N_MOD, D_MODEL), mod_idx),
            pl.BlockSpec((1, D_MODEL), const),
        ],
        out_specs=pl.BlockSpec((tm, D_MODEL), row),
        compiler_params=_params(("parallel",)),
        name="mix_out",
    )(yf, yb, att, g, bon, lnx_g, lnx_b, ones_bd, wo_att, wo_rw, x_all, mods, post_g)


def _route(logits_t, bias_col):
    scores = jax.nn.sigmoid(logits_t[0:N_EXPERTS, :])
    biased = scores + bias_col
    s_rows = [scores[e:e + 1, :] for e in range(N_EXPERTS)]
    b_rows = [biased[e:e + 1, :] for e in range(N_EXPERTS)]
    npg = EXPERTS_PER_GROUP
    group_scores = []
    for gi in range(N_GROUPS):
        bg = b_rows[gi * npg:(gi + 1) * npg]
        best_pair = None
        for i in range(npg):
            for j in range(i + 1, npg):
                pair = bg[i] + bg[j]
                best_pair = pair if best_pair is None else jnp.maximum(best_pair, pair)
        group_scores.append(best_pair)
    best = group_scores[0]
    best_idx = jnp.zeros(best.shape, jnp.int32)
    for gi in range(1, N_GROUPS):
        upd = group_scores[gi] > best
        best = jnp.where(upd, group_scores[gi], best)
        best_idx = jnp.where(upd, gi, best_idx)
    pick = lambda rows, j: functools.reduce(
        lambda acc, gi: jnp.where(best_idx == gi, rows[gi * npg + j], acc), range(1, N_GROUPS), rows[j])
    bb = [pick(b_rows, j) for j in range(npg)]
    ss = [pick(s_rows, j) for j in range(npg)]
    weights = []
    for j in range(npg):
        rank = jnp.zeros(best.shape, jnp.int32)
        for i in range(npg):
            if i == j:
                continue
            beats = (bb[i] > bb[j]) | ((bb[i] == bb[j]) & (i < j)) if i < j else (bb[i] > bb[j])
            rank = rank + beats.astype(jnp.int32)
        weights.append(jnp.where(rank < 2, ss[j], 0.0))
    den = weights[0] + weights[1] + weights[2] + weights[3]
    gates = [wj / den for wj in weights]
    return [jnp.where(best_idx == e // npg, gates[e % npg], 0.0) for e in range(N_EXPERTS)]


def _moe_kernel(x_ref, mod_ref, pre_ref, post_ref, rw_ref, rb_ref, wgu_ref, wd_ref, o_ref, h_scr, gate_scr, acc_scr):
    e = pl.program_id(1)
    n_e = pl.num_programs(1)
    tm = x_ref.shape[0]

    @pl.when(e == 0)
    def _():
        h = _rms(x_ref[...], pre_ref[...]) * (1.0 + mod_ref[0, 4:5, :]) + mod_ref[0, 3:4, :]
        h_scr[...] = h.astype(BF16)
        logits_t = lax.dot_general(rw_ref[...], h, (((1,), (1,)), ((), ())), precision=HIGHEST,
                                   preferred_element_type=F32)
        rows = _route(logits_t, rb_ref[...])
        sub = lax.broadcasted_iota(jnp.int32, (LANES, tm), 0)
        gates_t = jnp.where(sub == N_EXPERTS, 1.0, 0.0)
        for k in range(N_EXPERTS):
            gates_t = jnp.where(sub == k, rows[k], gates_t)
        gate_scr[...] = gates_t.T
        acc_scr[...] = jnp.zeros_like(acc_scr)

    lane = lax.broadcasted_iota(jnp.int32, (tm, LANES), 1)
    gate = jnp.sum(jnp.where(lane == e, gate_scr[...], 0.0), axis=1, keepdims=True)
    gu = _dot(h_scr[...], wgu_ref[0])
    g_part = gu[:, 0:EXPERT_HIDDEN]
    act = g_part * jax.nn.sigmoid(g_part) * gu[:, EXPERT_HIDDEN:] * gate
    acc_scr[...] += _dot(act.astype(BF16), wd_ref[0])

    @pl.when(e == n_e - 1)
    def _():
        o_ref[...] = x_ref[...] + mod_ref[0, 5:6, :] * _rms(acc_scr[...], post_ref[...])


def _moe(x_all, mods, pre_g, post_g, router_wt, router_b, wgu, wd, *, rows, n_lat, seq, tm):
    lat_tiles = n_lat // tm
    n_batch = n_lat // seq
    n_e = wgu.shape[0]

    def mod_idx(i, e):
        return (jnp.where(i < lat_tiles, (i * tm) // seq, n_batch), 0, 0)

    const = lambda i, e: (0, 0)
    row = lambda i, e: (i, 0)
    return pl.pallas_call(
        _moe_kernel,
        out_shape=jax.ShapeDtypeStruct((rows, D_MODEL), F32),
        grid=(rows // tm, n_e),
        in_specs=[
            pl.BlockSpec((tm, D_MODEL), row),
            pl.BlockSpec((1, N_MOD, D_MODEL), mod_idx),
            pl.BlockSpec((1, D_MODEL), const),
            pl.BlockSpec((1, D_MODEL), const),
            pl.BlockSpec((LANES, D_MODEL), const),
            pl.BlockSpec((N_EXPERTS, 1), const),
            pl.BlockSpec((1, D_MODEL, 2 * EXPERT_HIDDEN), lambda i, e: (e, 0, 0)),
            pl.BlockSpec((1, EXPERT_HIDDEN, D_MODEL), lambda i, e: (e, 0, 0)),
        ],
        out_specs=pl.BlockSpec((tm, D_MODEL), row),
        scratch_shapes=[
            pltpu.VMEM((tm, D_MODEL), BF16),
            pltpu.VMEM((tm, LANES), F32),
            pltpu.VMEM((tm, D_MODEL), F32),
        ],
        compiler_params=_params(("parallel", "arbitrary")),
        name="moe",
    )(x_all, mods, pre_g, post_g, router_wt, router_b, wgu, wd)


def _pack_in_proj(w_in, shift_mu):
    d = w_in.shape[0]
    z = lambda n: jnp.zeros((d, n), w_in.dtype)
    kr = w_in[:, Q_RANK + KV_RANK:MLA_COLS]
    half = MLA_ROPE // 2
    kr_sw = jnp.concatenate([kr[:, half:], kr[:, :half]], axis=1)
    pad = LANES - MLA_NOPE - MLA_ROPE
    packed = jnp.concatenate([
        w_in[:, :Q_RANK + KV_RANK],
        z(MLA_NOPE), kr, z(pad),
        z(MLA_NOPE), kr_sw, z(pad),
        w_in[:, MLA_COLS:], z(RWKV_PAD - RWKV_COLS),
    ], axis=1)
    mu = jnp.pad(shift_mu, ((0, 0), (0, RWKV_PAD - RWKV_COLS)))
    return packed.astype(BF16), mu


def _pack_mla(w_uq, w_ukv):
    half = MLA_ROPE // 2
    dq = MLA_NOPE + MLA_ROPE
    q3 = w_uq.reshape(Q_RANK, HEADS, dq)
    zq = lambda n: jnp.zeros((Q_RANK, HEADS, n), w_uq.dtype)
    pad = HEAD_SLOT - dq
    wq1 = jnp.concatenate([q3, zq(pad)], axis=2).reshape(Q_RANK, HEADS * HEAD_SLOT)
    wq2 = jnp.concatenate([zq(MLA_NOPE), q3[:, :, MLA_NOPE + half:], q3[:, :, MLA_NOPE:MLA_NOPE + half], zq(pad)],
                          axis=2).reshape(Q_RANK, HEADS * HEAD_SLOT)
    kv3 = w_ukv.reshape(KV_RANK, HEADS, MLA_NOPE + MLA_V)
    wuk = jnp.concatenate([kv3[:, :, :MLA_NOPE], jnp.zeros((KV_RANK, HEADS, HEAD_SLOT - MLA_NOPE), w_ukv.dtype)],
                          axis=2).reshape(KV_RANK, HEADS * HEAD_SLOT)
    wuv = kv3[:, :, MLA_NOPE:].reshape(KV_RANK, HEADS * MLA_V)
    return wq1.astype(BF16), wq2.astype(BF16), wuk.astype(BF16), wuv.astype(BF16)


def _block_diag2(m):
    r, c = m.shape[1], m.shape[2]
    z = jnp.zeros((r, c), m.dtype)
    return jnp.concatenate([jnp.concatenate([m[0], z], axis=1), jnp.concatenate([z, m[1]], axis=1)], axis=0)


def _rope_tables(seq, ctx_len):
    axis_dim = MLA_ROPE // 2
    t = jnp.arange(seq, dtype=jnp.int32)
    row = (t // GRID_W).astype(F32)
    col = (t % GRID_W).astype(F32)
    inv_freq = ROPE_BASE ** (-jnp.arange(0, axis_dim, 2, dtype=F32) / axis_dim)
    ang = jnp.concatenate([row[:, None] * inv_freq, col[:, None] * inv_freq], axis=-1)
    cos = jnp.concatenate([jnp.cos(ang), jnp.ones((ctx_len, axis_dim), F32)], axis=0)
    sin = jnp.concatenate([jnp.sin(ang), jnp.zeros((ctx_len, axis_dim), F32)], axis=0)
    n = seq + ctx_len
    ones = jnp.ones((n, MLA_NOPE), F32)
    z_nope = jnp.zeros((n, MLA_NOPE), F32)
    z_pad = jnp.zeros((n, HEAD_SLOT - MLA_NOPE - MLA_ROPE), F32)
    cq = jnp.concatenate([ones, cos, cos, z_pad], axis=1)
    sq = jnp.concatenate([z_nope, -sin, sin, z_pad], axis=1)
    ck = jnp.concatenate([z_nope, cos, cos, z_pad], axis=1)
    return cq, sq, ck


def kernel(x, c, ctx, c_ctx, ada_w, ada_b, mix_pre_g, mix_post_g, ffn_pre_g, ffn_post_g, w_in, q_norm_g, kv_norm_g,
           w_uq, w_ukv, shift_mu, decay_w0, decay_w2, iclr_a0, iclr_a2, gate_g2, k_k, k_a, r_k, lnx_g, lnx_b, w_out,
           router_w, router_bias, exp_w_gate, exp_w_up, exp_w_down, sh_w_gate, sh_w_up, sh_w_down):
    n_batch, seq, d = x.shape
    ctx_len = ctx.shape[1]
    depth = ada_w.shape[0]
    assert d == D_MODEL and seq % TOKEN_TILE == 0 and ctx_len % TOKEN_TILE == 0 and seq % ctx_len == 0
    assert seq % GRID_W == 0 and n_batch + 1 <= 8
    n_lat = n_batch * seq
    n_ctx = n_batch * ctx_len
    moe_tile = 1024 if (n_lat % 1024 == 0 and n_ctx % 1024 == 0 and seq % 1024 == 0) else TOKEN_TILE

    x_all = jnp.concatenate([x.reshape(n_lat, d), ctx.reshape(n_ctx, d)], axis=0)
    cond_rows = jnp.concatenate([c, c_ctx[None], jnp.zeros((8 - n_batch - 1, d), F32)], axis=0)
    mods_all = _ada_modulation(cond_rows, ada_w, ada_b).reshape(depth, 8, N_MOD, d)

    cq_t, sq_t, ck_t = _rope_tables(seq, ctx_len)
    w = RWKV_WIDTH
    ones_bd = (jnp.arange(w)[:, None] // RWKV_HEAD == jnp.arange(w)[None, :] // RWKV_HEAD).astype(BF16)
    ti = jnp.arange(TOKEN_TILE)
    same_chunk = (ti[:, None] // CHUNK) == (ti[None, :] // CHUNK)
    ones_c = same_chunk.astype(BF16)
    tri_f = jnp.logical_and(same_chunk, ti[None, :] <= ti[:, None]).astype(BF16)
    tri_r = jnp.logical_and(same_chunk, ti[None, :] >= ti[:, None]).astype(BF16)
    router_wt = jnp.pad(router_w.T, ((0, LANES - N_EXPERTS), (0, 0)))
    router_b = router_bias.reshape(N_EXPERTS, 1)
    row1 = lambda a: a.reshape(1, -1)

    for l in range(depth):
        ctx_out = l < depth - 1
        mods = mods_all[l]
        rows = n_lat + n_ctx if ctx_out else n_lat

        w_in_p, mu = _pack_in_proj(w_in[l], shift_mu[l])
        wq1, wq2, wuk, wuv = _pack_mla(w_uq[l], w_ukv[l])
        q, k, v, p_rw = _in_proj(x_all, mods, row1(mix_pre_g[l]), w_in_p, row1(q_norm_g[l]), row1(kv_norm_g[l]),
                                 wq1, wq2, wuk, wuv, cq_t, sq_t, ck_t, n_lat=n_lat, seq=seq, ctx_len=ctx_len)
        att = _attention(q, k, v, n_batch=n_batch, seq=seq, ctx_len=ctx_len, ctx_out=ctx_out)

        g2p = jnp.pad(gate_g2[l], ((0, GATE_PAD - GATE_LORA), (0, 0))).astype(BF16)
        prep = _rwkv_prep(
            p_rw, mu, _block_diag2(decay_w2[l]).astype(BF16), decay_w0[l].reshape(1, 2 * w),
            _block_diag2(iclr_a2[l]).astype(BF16), iclr_a0[l].reshape(1, 2 * w), g2p,
            row1(k_k[l]), row1(k_a[l]), row1(r_k[l]), ones_bd, tri_f, tri_r, ones_c,
            n_lat=n_lat, seq=seq, ctx_len=ctx_len)
        vv, g, bon = prep[0], prep[11], prep[12]
        yf, yb = _rwkv_scan(vv, prep[1:6], prep[6:11], n_batch=n_batch, seq=seq, ctx_len=ctx_len)

        wo = w_out[l].astype(BF16)
        x_all = _mix_out(yf, yb, att, g, bon, row1(lnx_g[l]), row1(lnx_b[l]), ones_bd, wo[:w], wo[w:], x_all, mods,
                         row1(mix_post_g[l]), rows=rows, n_lat=n_lat, seq=seq)

        wgu = jnp.concatenate([
            jnp.concatenate([exp_w_gate[l], exp_w_up[l]], axis=2),
            jnp.concatenate([sh_w_gate[l], sh_w_up[l]], axis=1)[None]], axis=0).astype(BF16)
        wd = jnp.concatenate([exp_w_down[l], sh_w_down[l][None]], axis=0).astype(BF16)
        x_all = _moe(x_all, mods, row1(ffn_pre_g[l]), row1(ffn_post_g[l]), router_wt, router_b, wgu, wd,
                     rows=rows, n_lat=n_lat, seq=seq, tm=moe_tile)

    return x_all[:n_lat].reshape(n_batch, seq, d)
```

```python
import functools
import math

import jax
import jax.numpy as jnp
from jax import lax
from jax.experimental import pallas as pl
from jax.experimental.pallas import tpu as pltpu

F32 = jnp.float32
BF16 = jnp.bfloat16
HIGHEST = lax.Precision.HIGHEST

D_MODEL = 1024
N_MOD = 6
NORM_EPS = 1e-6
GRID_W = 64
ROPE_BASE = 10000.0

HEADS = 8
MLA_NOPE = 64
MLA_ROPE = 32
MLA_V = 64
Q_RANK = 256
KV_RANK = 128
MLA_COLS = Q_RANK + KV_RANK + MLA_ROPE

RWKV_HEAD = 64
RWKV_WIDTH = HEADS * RWKV_HEAD
DECAY_LORA = 64
ICLR_LORA = 64
GATE_LORA = 160
RWKV_COLS = 3 * RWKV_WIDTH + 2 * DECAY_LORA + 2 * ICLR_LORA + GATE_LORA
RWKV_PAD = 2048
GATE_PAD = 256
LNX_EPS = 64e-5

N_EXPERTS = 16
N_GROUPS = 4
EXPERTS_PER_GROUP = 4
EXPERT_HIDDEN = 256

LANES = 128
HEAD_SLOT = 128
IN_PACKED = Q_RANK + KV_RANK + 2 * LANES + RWKV_PAD
TOKEN_TILE = 256
ATTN_KV_BLOCK = 1024
CHUNK = 64
VMEM_LIMIT = 48 * 1024 * 1024


def _dot(a, b):
    return jnp.dot(a, b, preferred_element_type=F32)


def _dot_nt(a, b):
    return lax.dot_general(a, b, (((1,), (1,)), ((), ())), preferred_element_type=F32)


def _dot_tn(a, b):
    return lax.dot_general(a, b, (((0,), (0,)), ((), ())), preferred_element_type=F32)


def _rms(x, g):
    return x * lax.rsqrt(jnp.mean(x * x, axis=-1, keepdims=True) + NORM_EPS) * g


def _seg_sum(x, ones_bd):
    hi = x.astype(BF16)
    lo = (x - hi.astype(F32)).astype(BF16)
    return _dot(hi, ones_bd) + _dot(lo, ones_bd)


def _params(sem):
    return pltpu.CompilerParams(dimension_semantics=sem, vmem_limit_bytes=VMEM_LIMIT)


def _ada_kernel(c_ref, w_ref, b_ref, o_ref):
    c = c_ref[...]
    cond = c * jax.nn.sigmoid(c)
    o_ref[0] = jnp.dot(cond, w_ref[0], precision=HIGHEST, preferred_element_type=F32) + b_ref[0]


def _ada_modulation(cond_rows, ada_w, ada_b):
    depth, d, n = ada_w.shape
    tn = 1536
    rows = cond_rows.shape[0]
    return pl.pallas_call(
        _ada_kernel,
        out_shape=jax.ShapeDtypeStruct((depth, rows, n), F32),
        grid=(depth, n // tn),
        in_specs=[
            pl.BlockSpec((rows, d), lambda l, j: (0, 0)),
            pl.BlockSpec((1, d, tn), lambda l, j: (l, 0, j)),
            pl.BlockSpec((1, 1, tn), lambda l, j: (l, 0, j)),
        ],
        out_specs=pl.BlockSpec((1, rows, tn), lambda l, j: (l, 0, j)),
        compiler_params=_params(("parallel", "parallel")),
        name="ada_modulation",
    )(cond_rows, ada_w, ada_b.reshape(depth, 1, n))


def _in_proj_kernel(x_ref, mod_ref, g_ref, win_ref, qng_ref, kvng_ref, wq1_ref, wq2_ref, wuk_ref, wuv_ref,
                    cq_ref, sq_ref, ck_ref, q_out, k_out, v_out, p_out, *, q_scale):
    x = x_ref[...]
    shift = mod_ref[0, 0:1, :]
    scale = mod_ref[0, 1:2, :]
    h = _rms(x, g_ref[...]) * (1.0 + scale) + shift
    p = _dot(h.astype(BF16), win_ref[...])
    c_q = p[:, 0:Q_RANK]
    c_kv = p[:, Q_RANK:Q_RANK + KV_RANK]
    kr_a = p[:, Q_RANK + KV_RANK:Q_RANK + KV_RANK + LANES]
    kr_b = p[:, Q_RANK + KV_RANK + LANES:Q_RANK + KV_RANK + 2 * LANES]
    p_out[...] = p[:, Q_RANK + KV_RANK + 2 * LANES:]

    tile8 = lambda t: jnp.concatenate([t] * HEADS, axis=1)
    cqn = _rms(c_q, qng_ref[...]).astype(BF16)
    q = _dot(cqn, wq1_ref[...]) * tile8(cq_ref[...]) + _dot(cqn, wq2_ref[...]) * tile8(sq_ref[...])
    q_out[...] = (q * q_scale).astype(BF16)

    ckvn = _rms(c_kv, kvng_ref[...]).astype(BF16)
    k_rot = kr_a * ck_ref[...] + kr_b * sq_ref[...]
    k_out[...] = (_dot(ckvn, wuk_ref[...]) + tile8(k_rot)).astype(BF16)
    v_out[...] = _dot(ckvn, wuv_ref[...]).astype(BF16)


def _in_proj(x_all, mods, g, w_in_p, qng, kvng, wq1, wq2, wuk, wuv, cq_t, sq_t, ck_t, *, n_lat, seq, ctx_len):
    nt = x_all.shape[0]
    tm = TOKEN_TILE
    lat_tiles = n_lat // tm
    n_batch = n_lat // seq

    def mod_idx(i):
        return (jnp.where(i < lat_tiles, (i * tm) // seq, n_batch), 0, 0)

    def tab_idx(i):
        return (jnp.where(i < lat_tiles, i % (seq // tm), seq // tm + (i - lat_tiles) % (ctx_len // tm)), 0)

    const = lambda i: (0, 0)
    row = lambda i: (i, 0)
    qw = HEADS * HEAD_SLOT
    return pl.pallas_call(
        functools.partial(_in_proj_kernel, q_scale=float((MLA_NOPE + MLA_ROPE) ** -0.5 * math.log2(math.e))),
        out_shape=(
            jax.ShapeDtypeStruct((nt, qw), BF16),
            jax.ShapeDtypeStruct((nt, qw), BF16),
            jax.ShapeDtypeStruct((nt, HEADS * MLA_V), BF16),
            jax.ShapeDtypeStruct((nt, RWKV_PAD), F32),
        ),
        grid=(nt // tm,),
        in_specs=[
            pl.BlockSpec((tm, D_MODEL), row),
            pl.BlockSpec((1, N_MOD, D_MODEL), mod_idx),
            pl.BlockSpec((1, D_MODEL), const),
            pl.BlockSpec((D_MODEL, IN_PACKED), const),
            pl.BlockSpec((1, Q_RANK), const),
            pl.BlockSpec((1, KV_RANK), const),
            pl.BlockSpec((Q_RANK, qw), const),
            pl.BlockSpec((Q_RANK, qw), const),
            pl.BlockSpec((KV_RANK, qw), const),
            pl.BlockSpec((KV_RANK, HEADS * MLA_V), const),
            pl.BlockSpec((tm, HEAD_SLOT), tab_idx),
            pl.BlockSpec((tm, HEAD_SLOT), tab_idx),
            pl.BlockSpec((tm, HEAD_SLOT), tab_idx),
        ],
        out_specs=(
            pl.BlockSpec((tm, qw), row),
            pl.BlockSpec((tm, qw), row),
            pl.BlockSpec((tm, HEADS * MLA_V), row),
            pl.BlockSpec((tm, RWKV_PAD), row),
        ),
        compiler_params=_params(("parallel",)),
        name="in_proj",
    )(x_all, mods, g, w_in_p, qng, kvng, wq1, wq2, wuk, wuv, cq_t, sq_t, ck_t)


def _attn_kernel(q_ref, kl_ref, kc_ref, vl_ref, vc_ref, o_ref, *, lat_q_tiles):
    j = pl.program_id(2)

    def compute(use_lat):
        blocks = [(kc_ref, vc_ref, 0, kc_ref.shape[0])]
        if use_lat:
            kb = min(ATTN_KV_BLOCK, kl_ref.shape[0])
            blocks += [(kl_ref, vl_ref, s0, kb) for s0 in range(0, kl_ref.shape[0], kb)]
        heads = range(2)
        hs = [slice(h * HEAD_SLOT, (h + 1) * HEAD_SLOT) for h in heads]
        q = [q_ref[:, hs[h]] for h in heads]

        def scores(j):
            k_ref, _, s0, size = blocks[j]
            return [_dot_nt(q[h], k_ref[s0:s0 + size, hs[h]]) for h in heads]

        m, den, acc = [None] * 2, [None] * 2, [None] * 2
        s_cur = scores(0)
        for j, (_, v_ref, s0, size) in enumerate(blocks):
            s_next = scores(j + 1) if j + 1 < len(blocks) else None
            for h in heads:
                s = s_cur[h]
                m_blk = jnp.max(s, axis=1, keepdims=True)
                if j == 0:
                    m[h] = m_blk
                    p = jnp.exp2(s - m_blk)
                    den[h] = jnp.sum(p, axis=1, keepdims=True)
                    acc[h] = _dot(p.astype(BF16), v_ref[s0:s0 + size, :])
                else:
                    m_new = jnp.maximum(m[h], m_blk)
                    alpha = jnp.exp2(m[h] - m_new)
                    p = jnp.exp2(s - m_new)
                    den[h] = alpha * den[h] + jnp.sum(p, axis=1, keepdims=True)
                    acc[h] = alpha * acc[h] + _dot(p.astype(BF16), v_ref[s0:s0 + size, :])
                    m[h] = m_new
            s_cur = s_next
        outs = [acc[h] / den[h] for h in heads]
        lane = lax.broadcasted_iota(jnp.int32, outs[0].shape, 1)
        o_ref[...] = jnp.where(lane < MLA_V, outs[0], outs[1]).astype(o_ref.dtype)

    @pl.when(j < lat_q_tiles)
    def _():
        compute(True)

    @pl.when(j >= lat_q_tiles)
    def _():
        compute(False)


def _attention(q, k, v, *, n_batch, seq, ctx_len, ctx_out):
    nt = q.shape[0]
    tq = TOKEN_TILE
    lat_q = seq // tq
    ctx_q = ctx_len // tq if ctx_out else 0
    lat_tiles_all = n_batch * lat_q
    ctx_blk0 = (n_batch * seq) // ctx_len

    def q_idx(b, hp, j):
        return (jnp.where(j < lat_q, b * lat_q + j, lat_tiles_all + b * (ctx_len // tq) + (j - lat_q)), hp)

    return pl.pallas_call(
        functools.partial(_attn_kernel, lat_q_tiles=lat_q),
        out_shape=jax.ShapeDtypeStruct((nt, HEADS * MLA_V), BF16),
        grid=(n_batch, HEADS // 2, lat_q + ctx_q),
        in_specs=[
            pl.BlockSpec((tq, 2 * HEAD_SLOT), q_idx),
            pl.BlockSpec((seq, 2 * HEAD_SLOT), lambda b, hp, j: (b, hp)),
            pl.BlockSpec((ctx_len, 2 * HEAD_SLOT), lambda b, hp, j: (ctx_blk0 + b, hp)),
            pl.BlockSpec((seq, 2 * MLA_V), lambda b, hp, j: (b, hp)),
            pl.BlockSpec((ctx_len, 2 * MLA_V), lambda b, hp, j: (ctx_blk0 + b, hp)),
        ],
        out_specs=pl.BlockSpec((tq, 2 * MLA_V), q_idx),
        compiler_params=_params(("parallel", "parallel", "arbitrary")),
        name="attention",
    )(q, k, k, v, v)


def _split3(x):
    hi = x.astype(BF16)
    r1 = x - hi.astype(F32)
    mid = r1.astype(BF16)
    lo = (r1 - mid.astype(F32)).astype(BF16)
    return hi, mid, lo


def _chunk_transpose(x):
    n = CHUNK
    xt = x.T
    rows = []
    for c in range(x.shape[0] // n):
        rows.append(jnp.concatenate(
            [xt[h * n:(h + 1) * n, c * n:(c + 1) * n] for h in range(x.shape[1] // n)], axis=1))
    return jnp.concatenate(rows, axis=0)


def _rwkv_prep_kernel(p_ref, hp_ref, hn_ref, mu_ref, w2_ref, w0_ref, a2_ref, a0_ref, g2_ref, kk_ref, ka_ref, rk_ref,
                      ones_ref, trif_ref, trir_ref, onesc_ref, v_out, rt0_out, kt0_out, bt0_out, kb0_out, pc0_out,
                      rt1_out, kt1_out, bt1_out, kb1_out, pc1_out, g_out, bon_out, *, lat_tiles, seq_tiles,
                      ctx_tiles):
    i = pl.program_id(0)
    tm = p_ref.shape[0]
    is_lat = i < lat_tiles
    local = jnp.where(is_lat, i % seq_tiles, (i - lat_tiles) % ctx_tiles)
    seg = jnp.where(is_lat, seq_tiles, ctx_tiles)
    keep_prev = jnp.where(local == 0, 0.0, 1.0)
    keep_next = jnp.where(local == seg - 1, 0.0, 1.0)

    p = p_ref[...]
    row = lax.broadcasted_iota(jnp.int32, (tm, 1), 0)
    prev = jnp.where(row == 0, hp_ref[7:8, :] * keep_prev, pltpu.roll(p, 1, 0))
    nxt = jnp.where(row == tm - 1, hn_ref[0:1, :] * keep_next, pltpu.roll(p, tm - 1, 0))
    ps = p + mu_ref[0:1, :] * (prev - p) + mu_ref[1:2, :] * (nxt - p)

    w = RWKV_WIDTH
    r = ps[:, 0:w]
    k = ps[:, w:2 * w]
    v = ps[:, 2 * w:3 * w]
    wl = ps[:, 3 * w:3 * w + 2 * DECAY_LORA]
    al = ps[:, 3 * w + 2 * DECAY_LORA:3 * w + 2 * DECAY_LORA + 2 * ICLR_LORA]
    gl = ps[:, 3 * w + 2 * DECAY_LORA + 2 * ICLR_LORA:3 * w + 2 * DECAY_LORA + 2 * ICLR_LORA + GATE_PAD]

    g_out[...] = _dot(jax.nn.sigmoid(gl).astype(BF16), g2_ref[...])
    z = w0_ref[...] + _dot(jnp.tanh(wl).astype(BF16), w2_ref[...])
    logw = (-math.exp(-0.5)) * jax.nn.sigmoid(z)
    a = jax.nn.sigmoid(a0_ref[...] + _dot(al.astype(BF16), a2_ref[...]))

    ones_bd = ones_ref[...]
    kk = k * kk_ref[...]
    kk = kk * lax.rsqrt(_seg_sum(kk * kk, ones_bd) + 1e-12)
    ka = ka_ref[...]
    v_out[...] = v.astype(BF16)

    ke_sum = None
    for d, tri_ref, (rt_out, kt_out, bt_out, kb_out, pc_out) in (
            (0, trif_ref, (rt0_out, kt0_out, bt0_out, kb0_out, pc0_out)),
            (1, trir_ref, (rt1_out, kt1_out, bt1_out, kb1_out, pc1_out))):
        lw = logw[:, d * w:(d + 1) * w]
        a_d = a[:, d * w:(d + 1) * w]
        parts = _split3(lw)
        cum = _dot(tri_ref[...], parts[0]) + _dot(tri_ref[...], parts[1]) + _dot(tri_ref[...], parts[2])
        tot = _dot(onesc_ref[...], parts[0]) + _dot(onesc_ref[...], parts[1]) + _dot(onesc_ref[...], parts[2])
        e_neg = jnp.exp(-cum)
        ke = k * (1.0 + (a_d - 1.0) * ka)
        ke_sum = ke if ke_sum is None else ke_sum + ke
        rt_out[...] = (r * jnp.exp(cum)).astype(BF16)
        kt_out[...] = (kk * jnp.exp(cum - lw)).astype(BF16)
        bt_out[...] = _chunk_transpose(a_d * kk * e_neg).astype(BF16)
        kb_out[...] = _chunk_transpose(ke * e_neg).astype(BF16)
        pc_out[...] = _chunk_transpose(jnp.exp(tot))
    bon_out[...] = _seg_sum(r * ke_sum * rk_ref[...], ones_bd) * v


def _rwkv_prep(p_rw, mu, w2bd, w0, a2bd, a0, g2p, k_k, k_a, r_k, ones_bd, tri_f, tri_r, ones_c, *, n_lat, seq,
               ctx_len):
    nt = p_rw.shape[0]
    tm = TOKEN_TILE
    n_tiles = nt // tm
    halo = 8
    blocks8 = nt // halo
    const = lambda i: (0, 0)
    row = lambda i: (i, 0)
    w = RWKV_WIDTH
    o16 = jax.ShapeDtypeStruct((nt, w), BF16)
    o32 = jax.ShapeDtypeStruct((nt, w), F32)
    return pl.pallas_call(
        functools.partial(_rwkv_prep_kernel, lat_tiles=n_lat // tm, seq_tiles=seq // tm, ctx_tiles=ctx_len // tm),
        out_shape=(o16, o16, o16, o16, o16, o32, o16, o16, o16, o16, o32, o32, o32),
        grid=(n_tiles,),
        in_specs=[
            pl.BlockSpec((tm, RWKV_PAD), row),
            pl.BlockSpec((halo, RWKV_PAD), lambda i: (jnp.maximum(i * (tm // halo) - 1, 0), 0)),
            pl.BlockSpec((halo, RWKV_PAD), lambda i: (jnp.minimum((i + 1) * (tm // halo), blocks8 - 1), 0)),
            pl.BlockSpec((2, RWKV_PAD), const),
            pl.BlockSpec((2 * DECAY_LORA, 2 * w), const),
            pl.BlockSpec((1, 2 * w), const),
            pl.BlockSpec((2 * ICLR_LORA, 2 * w), const),
            pl.BlockSpec((1, 2 * w), const),
            pl.BlockSpec((GATE_PAD, w), const),
            pl.BlockSpec((1, w), const),
            pl.BlockSpec((1, w), const),
            pl.BlockSpec((1, w), const),
            pl.BlockSpec((w, w), const),
            pl.BlockSpec((tm, tm), const),
            pl.BlockSpec((tm, tm), const),
            pl.BlockSpec((tm, tm), const),
        ],
        out_specs=(pl.BlockSpec((tm, w), row),) * 13,
        compiler_params=_params(("parallel",)),
        name="rwkv_prep",
    )(p_rw, p_rw, p_rw, mu, w2bd, w0, a2bd, a0, g2p, k_k, k_a, r_k, ones_bd, tri_f, tri_r, ones_c)


QUAD = 2
QUAD_W = QUAD * RWKV_HEAD
SCAN_CHUNKS = 4


def _scan_kernel(v0_ref, rt0_ref, kt0_ref, bt0_ref, kb0_ref, pc0_ref, v1_ref, rt1_ref, kt1_ref, bt1_ref, kb1_ref,
                 pc1_ref, y0_ref, y1_ref, h_ref):
    @pl.when(pl.program_id(1) == 0)
    def _():
        h_ref[...] = jnp.zeros_like(h_ref)

    n, qw = CHUNK, QUAD_W
    t_i = lax.broadcasted_iota(jnp.int32, (n, qw), 0)
    s_i = lax.broadcasted_iota(jnp.int32, (n, qw), 1) % n
    bd_mask = (lax.broadcasted_iota(jnp.int32, (qw, qw), 0) // n) == (lax.broadcasted_iota(jnp.int32, (qw, qw), 1) // n)
    eye = jnp.where(t_i == s_i, 1.0, 0.0)
    same16 = (t_i // 16) == (s_i // 16)
    same32 = (t_i // 32) == (s_i // 32)
    sib32 = jnp.logical_and(same32, jnp.logical_not(same16))
    sib64 = jnp.logical_not(same32)
    incl = (s_i <= t_i, s_i >= t_i)
    strict = (s_i < t_i, s_i > t_i)

    groups = [(d, q, j) for j in range(SCAN_CHUNKS) for d in (0, 1) for q in range(HEADS // QUAD)]
    dirs = [d for d, _, _ in groups]
    ng = range(len(groups))
    row0 = lambda d, j: (SCAN_CHUNKS - 1 - j) * n if d else j * n
    refs = ((v0_ref, rt0_ref, kt0_ref, bt0_ref, kb0_ref, pc0_ref), (v1_ref, rt1_ref, kt1_ref, bt1_ref, kb1_ref, pc1_ref))
    ld = lambda k: [refs[d][k][row0(d, j):row0(d, j) + n, q * qw:(q + 1) * qw] for d, q, j in groups]
    v, rt, kt, bt_t, kb_t, pc = (ld(k) for k in range(6))

    def bd(x):
        return jnp.where(bd_mask, jnp.concatenate([x.astype(BF16)] * QUAD, axis=0), jnp.zeros((), BF16))

    mm = lambda x, w: _dot(x.astype(BF16), w)
    rows2 = lambda x, y: jnp.concatenate([x.astype(BF16), y.astype(BF16)], axis=0)

    lhs =[rows2(kt[g], rt[g]) for g in ng]
    g1 = [_dot(lhs[g], bd(bt_t[g])) for g in ng]
    g2 = [_dot(lhs[g], bd(kb_t[g])) for g in ng]
    a_m = [jnp.where(strict[dirs[g]], g1[g][0:n], 0.0) for g in ng]
    c_m = [jnp.where(incl[dirs[g]], g1[g][n:2 * n], 0.0) for g in ng]
    b_m = [jnp.where(strict[dirs[g]], g2[g][0:n], 0.0) for g in ng]
    e_m = [jnp.where(incl[dirs[g]], g2[g][n:2 * n], 0.0) for g in ng]

    dd = [jnp.where(same16, a_m[g], 0.0) for g in ng]
    pw = [mm(dd[g], bd(dd[g])) for g in ng]
    t = [eye - dd[g] for g in ng]
    for _ in range(2):
        both = [_dot(rows2(t[g], pw[g]), bd(pw[g])) for g in ng]
        t = [t[g] + both[g][0:n] for g in ng]
        pw = [both[g][n:2 * n] for g in ng]
    t = [t[g] + mm(t[g], bd(pw[g])) for g in ng]
    for sib in (sib32, sib64):
        ta = [mm(t[g], bd(jnp.where(sib, a_m[g], 0.0))) for g in ng]
        t = [t[g] - mm(ta[g], bd(t[g])) for g in ng]

    bev = [_dot(jnp.concatenate([rows2(b_m[g], e_m[g]), kb_t[g]], axis=0), bd(v[g])) for g in ng]
    kv = [bev[g][2 * n:3 * n] for g in ng]
    m1 = [mm(t[g], bd(kt[g])) for g in ng]
    w2 = [mm(t[g], bd(bev[g][0:n])) for g in ng]
    qh = [rt[g].astype(F32) - mm(c_m[g], bd(m1[g])) for g in ng]
    y0 = [bev[g][n:2 * n] - mm(c_m[g], bd(w2[g])) for g in ng]

    chains = len(groups) // SCAN_CHUNKS
    h = [h_ref[s] for s in range(chains)]
    for g, (d, q, j) in enumerate(groups):
        s = g % chains
        uy = _dot(rows2(m1[g], qh[g]), bd(h[s]))
        (y0_ref, y1_ref)[d][row0(d, j):row0(d, j) + n, q * qw:(q + 1) * qw] = uy[n:2 * n] + y0[g]
        h[s] = pc[g] * (h[s] - _dot(bt_t[g], bd(uy[0:n] + w2[g])) + kv[g])
    for s in range(chains):
        h_ref[s] = h[s]


def _rwkv_scan(v, ops0, ops1, *, n_batch, seq, ctx_len):
    nt = v.shape[0]
    n = CHUNK * SCAN_CHUNKS
    assert seq % n == 0 and ctx_len % n == 0
    lat_c = seq // n
    ctx_c = ctx_len // n
    ctx0 = n_batch * lat_c

    def idx_f(bi, c):
        return (jnp.where(c < ctx_c, ctx0 + bi * ctx_c + c, bi * lat_c + (c - ctx_c)), 0)

    def idx_r(bi, c):
        return (jnp.where(c < ctx_c, ctx0 + bi * ctx_c + (ctx_c - 1 - c), bi * lat_c + (lat_c - 1 - (c - ctx_c))), 0)

    spec_f = pl.BlockSpec((n, RWKV_WIDTH), idx_f)
    spec_r = pl.BlockSpec((n, RWKV_WIDTH), idx_r)
    out = jax.ShapeDtypeStruct((nt, RWKV_WIDTH), F32)
    return pl.pallas_call(
        _scan_kernel,
        out_shape=(out, out),
        grid=(n_batch, ctx_c + lat_c),
        in_specs=[spec_f] * 6 + [spec_r] * 6,
        out_specs=(spec_f, spec_r),
        scratch_shapes=[pltpu.VMEM((2 * HEADS // QUAD, CHUNK, QUAD_W), F32)],
        compiler_params=_params(("parallel", "arbitrary")),
        name="rwkv_scan",
    )(v, *ops0, v, *ops1)


def _mix_out_kernel(yf_ref, yb_ref, att_ref, g_ref, bon_ref, lng_ref, lnb_ref, ones_ref, woa_ref, wor_ref, x_ref,
                    mod_ref, pg_ref, o_ref):
    ones_bd = ones_ref[...]
    y = yf_ref[...] + yb_ref[...]
    inv_n = 1.0 / RWKV_HEAD
    mu = _seg_sum(y, ones_bd) * inv_n
    yc = y - mu
    var = _seg_sum(yc * yc, ones_bd) * inv_n
    yn = yc * lax.rsqrt(var + LNX_EPS) * lng_ref[...] + lnb_ref[...]
    rw = (yn + bon_ref[...]) * g_ref[...]
    mix = _dot(att_ref[...], woa_ref[...]) + _dot(rw.astype(BF16), wor_ref[...])
    o_ref[...] = x_ref[...] + mod_ref[0, 2:3, :] * _rms(mix, pg_ref[...])


def _mix_out(yf, yb, att, g, bon, lnx_g, lnx_b, ones_bd, wo_att, wo_rw, x_all, mods, post_g, *, rows, n_lat, seq):
    tm = TOKEN_TILE
    lat_tiles = n_lat // tm
    n_batch = n_lat // seq
    w = RWKV_WIDTH

    def mod_idx(i):
        return (jnp.where(i < lat_tiles, (i * tm) // seq, n_batch), 0, 0)

    const = lambda i: (0, 0)
    row = lambda i: (i, 0)
    return pl.pallas_call(
        _mix_out_kernel,
        out_shape=jax.ShapeDtypeStruct((rows, D_MODEL), F32),
        grid=(rows // tm,),
        in_specs=[
            pl.BlockSpec((tm, w), row),
            pl.BlockSpec((tm, w), row),
            pl.BlockSpec((tm, w), row),
            pl.BlockSpec((tm, w), row),
            pl.BlockSpec((tm, w), row),
            pl.BlockSpec((1, w), const),
            pl.BlockSpec((1, w), const),
            pl.BlockSpec((w, w), const),
            pl.BlockSpec((w, D_MODEL), const),
            pl.BlockSpec((w, D_MODEL), const),
            pl.BlockSpec((tm, D_MODEL), row),
            pl.BlockSpec((1, N_MOD, D_MODEL), mod_idx),
            pl.BlockSpec((1, D_MODEL), const),
        ],
        out_specs=pl.BlockSpec((tm, D_MODEL), row),
        compiler_params=_params(("parallel",)),
        name="mix_out",
    )(yf, yb, att, g, bon, lnx_g, lnx_b, ones_bd, wo_att, wo_rw, x_all, mods, post_g)


def _route(logits_t, bias_col):
    scores = jax.nn.sigmoid(logits_t[0:N_EXPERTS, :])
    biased = scores + bias_col
    s_rows = [scores[e:e + 1, :] for e in range(N_EXPERTS)]
    b_rows = [biased[e:e + 1, :] for e in range(N_EXPERTS)]
    npg = EXPERTS_PER_GROUP
    group_scores = []
    for gi in range(N_GROUPS):
        bg = b_rows[gi * npg:(gi + 1) * npg]
        best_pair = None
        for i in range(npg):
            for j in range(i + 1, npg):
                pair = bg[i] + bg[j]
                best_pair = pair if best_pair is None else jnp.maximum(best_pair, pair)
        group_scores.append(best_pair)
    best = group_scores[0]
    best_idx = jnp.zeros(best.shape, jnp.int32)
    for gi in range(1, N_GROUPS):
        upd = group_scores[gi] > best
        best = jnp.where(upd, group_scores[gi], best)
        best_idx = jnp.where(upd, gi, best_idx)
    pick = lambda rows, j: functools.reduce(
        lambda acc, gi: jnp.where(best_idx == gi, rows[gi * npg + j], acc), range(1, N_GROUPS), rows[j])
    bb = [pick(b_rows, j) for j in range(npg)]
    ss = [pick(s_rows, j) for j in range(npg)]
    weights = []
    for j in range(npg):
        rank = jnp.zeros(best.shape, jnp.int32)
        for i in range(npg):
            if i == j:
                continue
            beats = (bb[i] > bb[j]) | ((bb[i] == bb[j]) & (i < j)) if i < j else (bb[i] > bb[j])
            rank = rank + beats.astype(jnp.int32)
        weights.append(jnp.where(rank < 2, ss[j], 0.0))
    den = weights[0] + weights[1] + weights[2] + weights[3]
    gates = [wj / den for wj in weights]
    return [jnp.where(best_idx == e // npg, gates[e % npg], 0.0) for e in range(N_EXPERTS)]


def _moe_kernel(x_ref, mod_ref, pre_ref, post_ref, rw_ref, rb_ref, wgu_ref, wd_ref, o_ref, h_scr, gate_scr, acc_scr):
    e = pl.program_id(1)
    n_e = pl.num_programs(1)
    tm = x_ref.shape[0]

    @pl.when(e == 0)
    def _():
        h = _rms(x_ref[...], pre_ref[...]) * (1.0 + mod_ref[0, 4:5, :]) + mod_ref[0, 3:4, :]
        h_scr[...] = h.astype(BF16)
        logits_t = lax.dot_general(rw_ref[...], h, (((1,), (1,)), ((), ())), precision=HIGHEST,
                                   preferred_element_type=F32)
        rows = _route(logits_t, rb_ref[...])
        sub = lax.broadcasted_iota(jnp.int32, (LANES, tm), 0)
        gates_t = jnp.where(sub == N_EXPERTS, 1.0, 0.0)
        for k in range(N_EXPERTS):
            gates_t = jnp.where(sub == k, rows[k], gates_t)
        gate_scr[...] = gates_t.T
        acc_scr[...] = jnp.zeros_like(acc_scr)

    lane = lax.broadcasted_iota(jnp.int32, (tm, LANES), 1)
    gate = jnp.sum(jnp.where(lane == e, gate_scr[...], 0.0), axis=1, keepdims=True)
    gu = _dot(h_scr[...], wgu_ref[0])
    g_part = gu[:, 0:EXPERT_HIDDEN]
    act = g_part * jax.nn.sigmoid(g_part) * gu[:, EXPERT_HIDDEN:] * gate
    acc_scr[...] += _dot(act.astype(BF16), wd_ref[0])

    @pl.when(e == n_e - 1)
    def _():
        o_ref[...] = x_ref[...] + mod_ref[0, 5:6, :] * _rms(acc_scr[...], post_ref[...])


def _moe(x_all, mods, pre_g, post_g, router_wt, router_b, wgu, wd, *, rows, n_lat, seq, tm):
    lat_tiles = n_lat // tm
    n_batch = n_lat // seq
    n_e = wgu.shape[0]

    def mod_idx(i, e):
        return (jnp.where(i < lat_tiles, (i * tm) // seq, n_batch), 0, 0)

    const = lambda i, e: (0, 0)
    row = lambda i, e: (i, 0)
    return pl.pallas_call(
        _moe_kernel,
        out_shape=jax.ShapeDtypeStruct((rows, D_MODEL), F32),
        grid=(rows // tm, n_e),
        in_specs=[
            pl.BlockSpec((tm, D_MODEL), row),
            pl.BlockSpec((1, N_MOD, D_MODEL), mod_idx),
            pl.BlockSpec((1, D_MODEL), const),
            pl.BlockSpec((1, D_MODEL), const),
            pl.BlockSpec((LANES, D_MODEL), const),
            pl.BlockSpec((N_EXPERTS, 1), const),
            pl.BlockSpec((1, D_MODEL, 2 * EXPERT_HIDDEN), lambda i, e: (e, 0, 0)),
            pl.BlockSpec((1, EXPERT_HIDDEN, D_MODEL), lambda i, e: (e, 0, 0)),
        ],
        out_specs=pl.BlockSpec((tm, D_MODEL), row),
        scratch_shapes=[
            pltpu.VMEM((tm, D_MODEL), BF16),
            pltpu.VMEM((tm, LANES), F32),
            pltpu.VMEM((tm, D_MODEL), F32),
        ],
        compiler_params=_params(("parallel", "arbitrary")),
        name="moe",
    )(x_all, mods, pre_g, post_g, router_wt, router_b, wgu, wd)


def _pack_in_proj(w_in, shift_mu):
    d = w_in.shape[0]
    z = lambda n: jnp.zeros((d, n), w_in.dtype)
    kr = w_in[:, Q_RANK + KV_RANK:MLA_COLS]
    half = MLA_ROPE // 2
    kr_sw = jnp.concatenate([kr[:, half:], kr[:, :half]], axis=1)
    pad = LANES - MLA_NOPE - MLA_ROPE
    packed = jnp.concatenate([
        w_in[:, :Q_RANK + KV_RANK],
        z(MLA_NOPE), kr, z(pad),
        z(MLA_NOPE), kr_sw, z(pad),
        w_in[:, MLA_COLS:], z(RWKV_PAD - RWKV_COLS),
    ], axis=1)
    mu = jnp.pad(shift_mu, ((0, 0), (0, RWKV_PAD - RWKV_COLS)))
    return packed.astype(BF16), mu


def _pack_mla(w_uq, w_ukv):
    half = MLA_ROPE // 2
    dq = MLA_NOPE + MLA_ROPE
    q3 = w_uq.reshape(Q_RANK, HEADS, dq)
    zq = lambda n: jnp.zeros((Q_RANK, HEADS, n), w_uq.dtype)
    pad = HEAD_SLOT - dq
    wq1 = jnp.concatenate([q3, zq(pad)], axis=2).reshape(Q_RANK, HEADS * HEAD_SLOT)
    wq2 = jnp.concatenate([zq(MLA_NOPE), q3[:, :, MLA_NOPE + half:], q3[:, :, MLA_NOPE:MLA_NOPE + half], zq(pad)],
                          axis=2).reshape(Q_RANK, HEADS * HEAD_SLOT)
    kv3 = w_ukv.reshape(KV_RANK, HEADS, MLA_NOPE + MLA_V)
    wuk = jnp.concatenate([kv3[:, :, :MLA_NOPE], jnp.zeros((KV_RANK, HEADS, HEAD_SLOT - MLA_NOPE), w_ukv.dtype)],
                          axis=2).reshape(KV_RANK, HEADS * HEAD_SLOT)
    wuv = kv3[:, :, MLA_NOPE:].reshape(KV_RANK, HEADS * MLA_V)
    return wq1.astype(BF16), wq2.astype(BF16), wuk.astype(BF16), wuv.astype(BF16)


def _block_diag2(m):
    r, c = m.shape[1], m.shape[2]
    z = jnp.zeros((r, c), m.dtype)
    return jnp.concatenate([jnp.concatenate([m[0], z], axis=1), jnp.concatenate([z, m[1]], axis=1)], axis=0)


def _rope_tables(seq, ctx_len):
    axis_dim = MLA_ROPE // 2
    t = jnp.arange(seq, dtype=jnp.int32)
    row = (t // GRID_W).astype(F32)
    col = (t % GRID_W).astype(F32)
    inv_freq = ROPE_BASE ** (-jnp.arange(0, axis_dim, 2, dtype=F32) / axis_dim)
    ang = jnp.concatenate([row[:, None] * inv_freq, col[:, None] * inv_freq], axis=-1)
    cos = jnp.concatenate([jnp.cos(ang), jnp.ones((ctx_len, axis_dim), F32)], axis=0)
    sin = jnp.concatenate([jnp.sin(ang), jnp.zeros((ctx_len, axis_dim), F32)], axis=0)
    n = seq + ctx_len
    ones = jnp.ones((n, MLA_NOPE), F32)
    z_nope = jnp.zeros((n, MLA_NOPE), F32)
    z_pad = jnp.zeros((n, HEAD_SLOT - MLA_NOPE - MLA_ROPE), F32)
    cq = jnp.concatenate([ones, cos, cos, z_pad], axis=1)
    sq = jnp.concatenate([z_nope, -sin, sin, z_pad], axis=1)
    ck = jnp.concatenate([z_nope, cos, cos, z_pad], axis=1)
    return cq, sq, ck


def kernel(x, c, ctx, c_ctx, ada_w, ada_b, mix_pre_g, mix_post_g, ffn_pre_g, ffn_post_g, w_in, q_norm_g, kv_norm_g,
           w_uq, w_ukv, shift_mu, decay_w0, decay_w2, iclr_a0, iclr_a2, gate_g2, k_k, k_a, r_k, lnx_g, lnx_b, w_out,
           router_w, router_bias, exp_w_gate, exp_w_up, exp_w_down, sh_w_gate, sh_w_up, sh_w_down):
    n_batch, seq, d = x.shape
    ctx_len = ctx.shape[1]
    depth = ada_w.shape[0]
    assert d == D_MODEL and seq % TOKEN_TILE == 0 and ctx_len % TOKEN_TILE == 0 and seq % ctx_len == 0
    assert seq % GRID_W == 0 and n_batch + 1 <= 8
    n_lat = n_batch * seq
    n_ctx = n_batch * ctx_len
    moe_tile = 1024 if (n_lat % 1024 == 0 and n_ctx % 1024 == 0 and seq % 1024 == 0) else TOKEN_TILE

    x_all = jnp.concatenate([x.reshape(n_lat, d), ctx.reshape(n_ctx, d)], axis=0)
    cond_rows = jnp.concatenate([c, c_ctx[None], jnp.zeros((8 - n_batch - 1, d), F32)], axis=0)
    mods_all = _ada_modulation(cond_rows, ada_w, ada_b).reshape(depth, 8, N_MOD, d)

    cq_t, sq_t, ck_t = _rope_tables(seq, ctx_len)
    w = RWKV_WIDTH
    ones_bd = (jnp.arange(w)[:, None] // RWKV_HEAD == jnp.arange(w)[None, :] // RWKV_HEAD).astype(BF16)
    ti = jnp.arange(TOKEN_TILE)
    same_chunk = (ti[:, None] // CHUNK) == (ti[None, :] // CHUNK)
    ones_c = same_chunk.astype(BF16)
    tri_f = jnp.logical_and(same_chunk, ti[None, :] <= ti[:, None]).astype(BF16)
    tri_r = jnp.logical_and(same_chunk, ti[None, :] >= ti[:, None]).astype(BF16)
    router_wt = jnp.pad(router_w.T, ((0, LANES - N_EXPERTS), (0, 0)))
    router_b = router_bias.reshape(N_EXPERTS, 1)
    row1 = lambda a: a.reshape(1, -1)

    for l in range(depth):
        ctx_out = l < depth - 1
        mods = mods_all[l]
        rows = n_lat + n_ctx if ctx_out else n_lat

        w_in_p, mu = _pack_in_proj(w_in[l], shift_mu[l])
        wq1, wq2, wuk, wuv = _pack_mla(w_uq[l], w_ukv[l])
        q, k, v, p_rw = _in_proj(x_all, mods, row1(mix_pre_g[l]), w_in_p, row1(q_norm_g[l]), row1(kv_norm_g[l]),
                                 wq1, wq2, wuk, wuv, cq_t, sq_t, ck_t, n_lat=n_lat, seq=seq, ctx_len=ctx_len)
        att = _attention(q, k, v, n_batch=n_batch, seq=seq, ctx_len=ctx_len, ctx_out=ctx_out)

        g2p = jnp.pad(gate_g2[l], ((0, GATE_PAD - GATE_LORA), (0, 0))).astype(BF16)
        prep = _rwkv_prep(
            p_rw, mu, _block_diag2(decay_w2[l]).astype(BF16), decay_w0[l].reshape(1, 2 * w),
            _block_diag2(iclr_a2[l]).astype(BF16), iclr_a0[l].reshape(1, 2 * w), g2p,
            row1(k_k[l]), row1(k_a[l]), row1(r_k[l]), ones_bd, tri_f, tri_r, ones_c,
            n_lat=n_lat, seq=seq, ctx_len=ctx_len)
        vv, g, bon = prep[0], prep[11], prep[12]
        yf, yb = _rwkv_scan(vv, prep[1:6], prep[6:11], n_batch=n_batch, seq=seq, ctx_len=ctx_len)

        wo = w_out[l].astype(BF16)
        x_all = _mix_out(yf, yb, att, g, bon, row1(lnx_g[l]), row1(lnx_b[l]), ones_bd, wo[:w], wo[w:], x_all, mods,
                         row1(mix_post_g[l]), rows=rows, n_lat=n_lat, seq=seq)

        wgu = jnp.concatenate([
            jnp.concatenate([exp_w_gate[l], exp_w_up[l]], axis=2),
            jnp.concatenate([sh_w_gate[l], sh_w_up[l]], axis=1)[None]], axis=0).astype(BF16)
        wd = jnp.concatenate([exp_w_down[l], sh_w_down[l][None]], axis=0).astype(BF16)
        x_all = _moe(x_all, mods, row1(ffn_pre_g[l]), row1(ffn_post_g[l]), router_wt, router_b, wgu, wd,
                     rows=rows, n_lat=n_lat, seq=seq, tm=moe_tile)

    return x_all[:n_lat].reshape(n_batch, seq, d)
```

```python
import functools
import math

import jax
import jax.numpy as jnp
from jax import lax
from jax.experimental import pallas as pl
from jax.experimental.pallas import tpu as pltpu

F32 = jnp.float32
BF16 = jnp.bfloat16
HIGHEST = lax.Precision.HIGHEST

D_MODEL = 1024
N_MOD = 6
NORM_EPS = 1e-6
GRID_W = 64
ROPE_BASE = 10000.0

HEADS = 8
MLA_NOPE = 64
MLA_ROPE = 32
MLA_V = 64
Q_RANK = 256
KV_RANK = 128
MLA_COLS = Q_RANK + KV_RANK + MLA_ROPE

RWKV_HEAD = 64
RWKV_WIDTH = HEADS * RWKV_HEAD
DECAY_LORA = 64
ICLR_LORA = 64
GATE_LORA = 160
RWKV_COLS = 3 * RWKV_WIDTH + 2 * DECAY_LORA + 2 * ICLR_LORA + GATE_LORA
RWKV_PAD = 2048
GATE_PAD = 256
LNX_EPS = 64e-5

N_EXPERTS = 16
N_GROUPS = 4
EXPERTS_PER_GROUP = 4
EXPERT_HIDDEN = 256

LANES = 128
HEAD_SLOT = 128
IN_PACKED = Q_RANK + KV_RANK + 2 * LANES + RWKV_PAD
TOKEN_TILE = 256
ATTN_KV_BLOCK = 1024
ATTN_Q_TILE = 512
CHUNK = 64
VMEM_LIMIT = 48 * 1024 * 1024


def _dot(a, b):
    return jnp.dot(a, b, preferred_element_type=F32)


def _dot_nt(a, b):
    return lax.dot_general(a, b, (((1,), (1,)), ((), ())), preferred_element_type=F32)


def _dot_tn(a, b):
    return lax.dot_general(a, b, (((0,), (0,)), ((), ())), preferred_element_type=F32)


def _rms(x, g):
    return x * lax.rsqrt(jnp.mean(x * x, axis=-1, keepdims=True) + NORM_EPS) * g


def _seg_sum(x, ones_bd):
    hi = x.astype(BF16)
    lo = (x - hi.astype(F32)).astype(BF16)
    return _dot(hi, ones_bd) + _dot(lo, ones_bd)


def _params(sem):
    return pltpu.CompilerParams(dimension_semantics=sem, vmem_limit_bytes=VMEM_LIMIT)


def _ada_kernel(c_ref, w_ref, b_ref, o_ref):
    c = c_ref[...]
    cond = c * jax.nn.sigmoid(c)
    o_ref[0] = jnp.dot(cond, w_ref[0], precision=HIGHEST, preferred_element_type=F32) + b_ref[0]


def _ada_modulation(cond_rows, ada_w, ada_b):
    depth, d, n = ada_w.shape
    tn = 1536
    rows = cond_rows.shape[0]
    return pl.pallas_call(
        _ada_kernel,
        out_shape=jax.ShapeDtypeStruct((depth, rows, n), F32),
        grid=(depth, n // tn),
        in_specs=[
            pl.BlockSpec((rows, d), lambda l, j: (0, 0)),
            pl.BlockSpec((1, d, tn), lambda l, j: (l, 0, j)),
            pl.BlockSpec((1, 1, tn), lambda l, j: (l, 0, j)),
        ],
        out_specs=pl.BlockSpec((1, rows, tn), lambda l, j: (l, 0, j)),
        compiler_params=_params(("parallel", "parallel")),
        name="ada_modulation",
    )(cond_rows, ada_w, ada_b.reshape(depth, 1, n))


def _stream_specs(tm, lat_tiles, merged):
    off = lat_tiles if merged else 0
    return (pl.BlockSpec((tm, D_MODEL), lambda i: (jnp.minimum(i, lat_tiles - 1), 0)),
            pl.BlockSpec((tm, D_MODEL), lambda i: (jnp.maximum(i - lat_tiles, 0) + off, 0)))


def _in_proj_kernel(xl_ref, xc_ref, mod_ref, g_ref, win_ref, qng_ref, kvng_ref, wq1_ref, wq2_ref, wuk_ref, wuv_ref,
                    cq_ref, sq_ref, ck_ref, q_out, k_out, v_out, p_out, *, q_scale, lat_tiles):
    x = jnp.where(pl.program_id(0) < lat_tiles, xl_ref[...], xc_ref[...])
    shift = mod_ref[0, 0:1, :]
    scale = mod_ref[0, 1:2, :]
    h = _rms(x, g_ref[...]) * (1.0 + scale) + shift
    p = _dot(h.astype(BF16), win_ref[...])
    c_q = p[:, 0:Q_RANK]
    c_kv = p[:, Q_RANK:Q_RANK + KV_RANK]
    kr_a = p[:, Q_RANK + KV_RANK:Q_RANK + KV_RANK + LANES]
    kr_b = p[:, Q_RANK + KV_RANK + LANES:Q_RANK + KV_RANK + 2 * LANES]
    p_out[...] = p[:, Q_RANK + KV_RANK + 2 * LANES:]

    tile8 = lambda t: jnp.concatenate([t] * HEADS, axis=1)
    cqn = _rms(c_q, qng_ref[...]).astype(BF16)
    q = _dot(cqn, wq1_ref[...]) * tile8(cq_ref[...]) + _dot(cqn, wq2_ref[...]) * tile8(sq_ref[...])
    q_out[...] = (q * q_scale).astype(BF16)

    ckvn = _rms(c_kv, kvng_ref[...]).astype(BF16)
    k_rot = kr_a * ck_ref[...] + kr_b * sq_ref[...]
    k_out[...] = (_dot(ckvn, wuk_ref[...]) + tile8(k_rot)).astype(BF16)
    v_out[...] = _dot(ckvn, wuv_ref[...]).astype(BF16)


def _in_proj(x_lat, x_ctx, mods, g, w_in_p, qng, kvng, wq1, wq2, wuk, wuv, cq_t, sq_t, ck_t, *, n_lat, n_ctx, seq,
             ctx_len):
    nt = n_lat + n_ctx
    tm = TOKEN_TILE
    lat_tiles = n_lat // tm
    n_batch = n_lat // seq

    def mod_idx(i):
        return (jnp.where(i < lat_tiles, (i * tm) // seq, n_batch), 0, 0)

    def tab_idx(i):
        return (jnp.where(i < lat_tiles, i % (seq // tm), seq // tm + (i - lat_tiles) % (ctx_len // tm)), 0)

    const = lambda i: (0, 0)
    row = lambda i: (i, 0)
    qw = HEADS * HEAD_SLOT
    return pl.pallas_call(
        functools.partial(_in_proj_kernel, q_scale=float((MLA_NOPE + MLA_ROPE) ** -0.5 * math.log2(math.e)),
                          lat_tiles=lat_tiles),
        out_shape=(
            jax.ShapeDtypeStruct((nt, qw), BF16),
            jax.ShapeDtypeStruct((nt, qw), BF16),
            jax.ShapeDtypeStruct((nt, HEADS * MLA_V), BF16),
            jax.ShapeDtypeStruct((nt, RWKV_PAD), F32),
        ),
        grid=(nt // tm,),
        in_specs=[
            *_stream_specs(tm, lat_tiles, x_ctx is x_lat),
            pl.BlockSpec((1, N_MOD, D_MODEL), mod_idx),
            pl.BlockSpec((1, D_MODEL), const),
            pl.BlockSpec((D_MODEL, IN_PACKED), const),
            pl.BlockSpec((1, Q_RANK), const),
            pl.BlockSpec((1, KV_RANK), const),
            pl.BlockSpec((Q_RANK, qw), const),
            pl.BlockSpec((Q_RANK, qw), const),
            pl.BlockSpec((KV_RANK, qw), const),
            pl.BlockSpec((KV_RANK, HEADS * MLA_V), const),
            pl.BlockSpec((tm, HEAD_SLOT), tab_idx),
            pl.BlockSpec((tm, HEAD_SLOT), tab_idx),
            pl.BlockSpec((tm, HEAD_SLOT), tab_idx),
        ],
        out_specs=(
            pl.BlockSpec((tm, qw), row),
            pl.BlockSpec((tm, qw), row),
            pl.BlockSpec((tm, HEADS * MLA_V), row),
            pl.BlockSpec((tm, RWKV_PAD), row),
        ),
        compiler_params=_params(("parallel",)),
        name="in_proj",
    )(x_lat, x_ctx, mods, g, w_in_p, qng, kvng, wq1, wq2, wuk, wuv, cq_t, sq_t, ck_t)


def _attn_kernel(q_ref, kc_ref, vc_ref, *rest):
    o_ref = rest[-1]
    blocks = [(kc_ref, vc_ref, 0, kc_ref.shape[0])]
    if len(rest) == 3:
        kl_ref, vl_ref = rest[0], rest[1]
        kb = min(ATTN_KV_BLOCK, kl_ref.shape[0])
        blocks += [(kl_ref, vl_ref, s0, kb) for s0 in range(0, kl_ref.shape[0], kb)]
    heads = range(2)
    hs = [slice(h * HEAD_SLOT, (h + 1) * HEAD_SLOT) for h in heads]
    q = [q_ref[:, hs[h]] for h in heads]

    def scores(j):
        k_ref, _, s0, size = blocks[j]
        return [_dot_nt(q[h], k_ref[s0:s0 + size, hs[h]]) for h in heads]

    m, den, acc = [None] * 2, [None] * 2, [None] * 2
    s_cur = scores(0)
    for j, (_, v_ref, s0, size) in enumerate(blocks):
        s_next = scores(j + 1) if j + 1 < len(blocks) else None
        for h in heads:
            s = s_cur[h]
            m_blk = jnp.max(s, axis=1, keepdims=True)
            if j == 0:
                m[h] = m_blk
                p = jnp.exp2(s - m_blk)
                den[h] = jnp.sum(p, axis=1, keepdims=True)
                acc[h] = _dot(p.astype(BF16), v_ref[s0:s0 + size, :])
            else:
                m_new = jnp.maximum(m[h], m_blk)
                alpha = jnp.exp2(m[h] - m_new)
                p = jnp.exp2(s - m_new)
                den[h] = alpha * den[h] + jnp.sum(p, axis=1, keepdims=True)
                acc[h] = alpha * acc[h] + _dot(p.astype(BF16), v_ref[s0:s0 + size, :])
                m[h] = m_new
        s_cur = s_next
    outs = [acc[h] / den[h] for h in heads]
    lane = lax.broadcasted_iota(jnp.int32, outs[0].shape, 1)
    o_ref[...] = jnp.where(lane < MLA_V, outs[0], outs[1]).astype(o_ref.dtype)


def _attention(q, k, v, *, n_batch, seq, ctx_len, latent):
    tq = ATTN_Q_TILE if latent else TOKEN_TILE
    seg = seq if latent else ctx_len
    assert seg % tq == 0
    q_tiles = seg // tq
    q_blk0 = 0 if latent else (n_batch * seq) // tq
    ctx_blk0 = (n_batch * seq) // ctx_len
    kv_ctx = lambda b, hp, j: (ctx_blk0 + b, hp)
    kv_lat = lambda b, hp, j: (b, hp)
    in_specs = [
        pl.BlockSpec((tq, 2 * HEAD_SLOT), lambda b, hp, j: (q_blk0 + b * q_tiles + j, hp)),
        pl.BlockSpec((ctx_len, 2 * HEAD_SLOT), kv_ctx),
        pl.BlockSpec((ctx_len, 2 * MLA_V), kv_ctx),
    ]
    args = [q, k, v]
    if latent:
        in_specs += [pl.BlockSpec((seq, 2 * HEAD_SLOT), kv_lat), pl.BlockSpec((seq, 2 * MLA_V), kv_lat)]
        args += [k, v]
    return pl.pallas_call(
        _attn_kernel,
        out_shape=jax.ShapeDtypeStruct((n_batch * seg, HEADS * MLA_V), BF16),
        grid=(n_batch, HEADS // 2, q_tiles),
        in_specs=in_specs,
        out_specs=pl.BlockSpec((tq, 2 * MLA_V), lambda b, hp, j: (b * q_tiles + j, hp)),
        compiler_params=_params(("parallel", "parallel", "arbitrary")),
        name="attention_lat" if latent else "attention_ctx",
    )(*args)


def _split2(x):
    hi = x.astype(BF16)
    return hi, (x - hi.astype(F32)).astype(BF16)


def _chunk_transpose(x):
    n = CHUNK
    xt = x.T
    rows = []
    for c in range(x.shape[0] // n):
        rows.append(jnp.concatenate(
            [xt[h * n:(h + 1) * n, c * n:(c + 1) * n] for h in range(x.shape[1] // n)], axis=1))
    return jnp.concatenate(rows, axis=0)


def _rwkv_prep_kernel(p_ref, hp_ref, hn_ref, mu_ref, w2_ref, w0_ref, a2_ref, a0_ref, g2_ref, kk_ref, ka_ref, rk_ref,
                      ones_ref, trif_ref, trir_ref, onesc_ref, v_out, rt0_out, kt0_out, bt0_out, kb0_out, pc0_out,
                      rt1_out, kt1_out, bt1_out, kb1_out, pc1_out, g_out, bon_out, *, lat_tiles, seq_tiles,
                      ctx_tiles):
    i = pl.program_id(0)
    tm = p_ref.shape[0]
    is_lat = i < lat_tiles
    local = jnp.where(is_lat, i % seq_tiles, (i - lat_tiles) % ctx_tiles)
    seg = jnp.where(is_lat, seq_tiles, ctx_tiles)
    keep_prev = jnp.where(local == 0, 0.0, 1.0)
    keep_next = jnp.where(local == seg - 1, 0.0, 1.0)

    p = p_ref[...]
    row = lax.broadcasted_iota(jnp.int32, (tm, 1), 0)
    prev = jnp.where(row == 0, hp_ref[7:8, :] * keep_prev, pltpu.roll(p, 1, 0))
    nxt = jnp.where(row == tm - 1, hn_ref[0:1, :] * keep_next, pltpu.roll(p, tm - 1, 0))
    ps = p + mu_ref[0:1, :] * (prev - p) + mu_ref[1:2, :] * (nxt - p)

    w = RWKV_WIDTH
    r = ps[:, 0:w]
    k = ps[:, w:2 * w]
    v = ps[:, 2 * w:3 * w]
    wl = ps[:, 3 * w:3 * w + 2 * DECAY_LORA]
    al = ps[:, 3 * w + 2 * DECAY_LORA:3 * w + 2 * DECAY_LORA + 2 * ICLR_LORA]
    gl = ps[:, 3 * w + 2 * DECAY_LORA + 2 * ICLR_LORA:3 * w + 2 * DECAY_LORA + 2 * ICLR_LORA + GATE_PAD]

    g_out[...] = _dot(jax.nn.sigmoid(gl).astype(BF16), g2_ref[...])
    z = w0_ref[...] + _dot(jnp.tanh(wl).astype(BF16), w2_ref[...])
    logw = (-math.exp(-0.5)) * jax.nn.sigmoid(z)
    a = jax.nn.sigmoid(a0_ref[...] + _dot(al.astype(BF16), a2_ref[...]))

    ones_bd = ones_ref[...]
    kk = k * kk_ref[...]
    kk = kk * lax.rsqrt(_seg_sum(kk * kk, ones_bd) + 1e-12)
    ka = ka_ref[...]
    v_out[...] = v.astype(BF16)

    ke_sum = None
    for d, tri_ref, (rt_out, kt_out, bt_out, kb_out, pc_out) in (
            (0, trif_ref, (rt0_out, kt0_out, bt0_out, kb0_out, pc0_out)),
            (1, trir_ref, (rt1_out, kt1_out, bt1_out, kb1_out, pc1_out))):
        lw = logw[:, d * w:(d + 1) * w]
        a_d = a[:, d * w:(d + 1) * w]
        lw_hi, lw_lo = _split2(lw)
        cum = _dot(tri_ref[...], lw_hi) + _dot(tri_ref[...], lw_lo)
        tot = _dot(onesc_ref[...], lw_hi) + _dot(onesc_ref[...], lw_lo)
        e_neg = jnp.exp(-cum)
        ke = k * (1.0 + (a_d - 1.0) * ka)
        ke_sum = ke if ke_sum is None else ke_sum + ke
        rt_out[...] = (r * jnp.exp(cum)).astype(BF16)
        kt_out[...] = (kk * jnp.exp(cum - lw)).astype(BF16)
        bt_out[...] = _chunk_transpose(a_d * kk * e_neg).astype(BF16)
        kb_out[...] = _chunk_transpose(ke * e_neg).astype(BF16)
        pc_out[...] = _chunk_transpose(jnp.exp(tot))
    bon_out[...] = _seg_sum(r * ke_sum * rk_ref[...], ones_bd) * v


def _rwkv_prep(p_rw, mu, w2bd, w0, a2bd, a0, g2p, k_k, k_a, r_k, ones_bd, tri_f, tri_r, ones_c, *, n_lat, seq,
               ctx_len):
    nt = p_rw.shape[0]
    tm = TOKEN_TILE
    n_tiles = nt // tm
    halo = 8
    blocks8 = nt // halo
    const = lambda i: (0, 0)
    row = lambda i: (i, 0)
    w = RWKV_WIDTH
    o16 = jax.ShapeDtypeStruct((nt, w), BF16)
    o32 = jax.ShapeDtypeStruct((nt, w), F32)
    return pl.pallas_call(
        functools.partial(_rwkv_prep_kernel, lat_tiles=n_lat // tm, seq_tiles=seq // tm, ctx_tiles=ctx_len // tm),
        out_shape=(o16, o16, o16, o16, o16, o32, o16, o16, o16, o16, o32, o32, o32),
        grid=(n_tiles,),
        in_specs=[
            pl.BlockSpec((tm, RWKV_PAD), row),
            pl.BlockSpec((halo, RWKV_PAD), lambda i: (jnp.maximum(i * (tm // halo) - 1, 0), 0)),
            pl.BlockSpec((halo, RWKV_PAD), lambda i: (jnp.minimum((i + 1) * (tm // halo), blocks8 - 1), 0)),
            pl.BlockSpec((2, RWKV_PAD), const),
            pl.BlockSpec((2 * DECAY_LORA, 2 * w), const),
            pl.BlockSpec((1, 2 * w), const),
            pl.BlockSpec((2 * ICLR_LORA, 2 * w), const),
            pl.BlockSpec((1, 2 * w), const),
            pl.BlockSpec((GATE_PAD, w), const),
            pl.BlockSpec((1, w), const),
            pl.BlockSpec((1, w), const),
            pl.BlockSpec((1, w), const),
            pl.BlockSpec((w, w), const),
            pl.BlockSpec((tm, tm), const),
            pl.BlockSpec((tm, tm), const),
            pl.BlockSpec((tm, tm), const),
        ],
        out_specs=(pl.BlockSpec((tm, w), row),) * 13,
        compiler_params=_params(("parallel",)),
        name="rwkv_prep",
    )(p_rw, p_rw, p_rw, mu, w2bd, w0, a2bd, a0, g2p, k_k, k_a, r_k, ones_bd, tri_f, tri_r, ones_c)


QUAD = 2
QUAD_W = QUAD * RWKV_HEAD
SCAN_CHUNKS = 4


def _scan_kernel(v0_ref, rt0_ref, kt0_ref, bt0_ref, kb0_ref, pc0_ref, v1_ref, rt1_ref, kt1_ref, bt1_ref, kb1_ref,
                 pc1_ref, y0_ref, y1_ref, h_ref):
    @pl.when(pl.program_id(1) == 0)
    def _():
        h_ref[...] = jnp.zeros_like(h_ref)

    n, qw = CHUNK, QUAD_W
    t_i = lax.broadcasted_iota(jnp.int32, (n, qw), 0)
    s_i = lax.broadcasted_iota(jnp.int32, (n, qw), 1) % n
    bd_mask = (lax.broadcasted_iota(jnp.int32, (qw, qw), 0) // n) == (lax.broadcasted_iota(jnp.int32, (qw, qw), 1) // n)
    eye = jnp.where(t_i == s_i, 1.0, 0.0)
    same16 = (t_i // 16) == (s_i // 16)
    same32 = (t_i // 32) == (s_i // 32)
    sib32 = jnp.logical_and(same32, jnp.logical_not(same16))
    sib64 = jnp.logical_not(same32)
    incl = (s_i <= t_i, s_i >= t_i)
    strict = (s_i < t_i, s_i > t_i)

    groups = [(d, q, j) for j in range(SCAN_CHUNKS) for d in (0, 1) for q in range(HEADS // QUAD)]
    dirs = [d for d, _, _ in groups]
    ng = range(len(groups))
    row0 = lambda d, j: (SCAN_CHUNKS - 1 - j) * n if d else j * n
    refs = ((v0_ref, rt0_ref, kt0_ref, bt0_ref, kb0_ref, pc0_ref), (v1_ref, rt1_ref, kt1_ref, bt1_ref, kb1_ref, pc1_ref))
    ld = lambda k: [refs[d][k][row0(d, j):row0(d, j) + n, q * qw:(q + 1) * qw] for d, q, j in groups]
    v, rt, kt, bt_t, kb_t, pc = (ld(k) for k in range(6))

    def bd(x):
        return jnp.where(bd_mask, jnp.concatenate([x.astype(BF16)] * QUAD, axis=0), jnp.zeros((), BF16))

    mm = lambda x, w: _dot(x.astype(BF16), w)
    rows2 = lambda x, y: jnp.concatenate([x.astype(BF16), y.astype(BF16)], axis=0)

    bd2 = lambda x, y: jnp.concatenate([bd(x), bd(y)], axis=1)
    gram = [_dot(rows2(kt[g], rt[g]), bd2(bt_t[g], kb_t[g])) for g in ng]
    a_m = [jnp.where(strict[dirs[g]], gram[g][0:n, 0:qw], 0.0) for g in ng]
    c_m = [jnp.where(incl[dirs[g]], gram[g][n:2 * n, 0:qw], 0.0) for g in ng]
    b_m = [jnp.where(strict[dirs[g]], gram[g][0:n, qw:2 * qw], 0.0) for g in ng]
    e_m = [jnp.where(incl[dirs[g]], gram[g][n:2 * n, qw:2 * qw], 0.0) for g in ng]

    dd = [jnp.where(same16, a_m[g], 0.0) for g in ng]
    pw = [mm(dd[g], bd(dd[g])) for g in ng]
    t = [eye - dd[g] for g in ng]
    for _ in range(2):
        both = [_dot(rows2(t[g], pw[g]), bd(pw[g])) for g in ng]
        t = [t[g] + both[g][0:n] for g in ng]
        pw = [both[g][n:2 * n] for g in ng]
    t = [t[g] + mm(t[g], bd(pw[g])) for g in ng]
    for sib in (sib32, sib64):
        ta = [mm(t[g], bd(jnp.where(sib, a_m[g], 0.0))) for g in ng]
        t = [t[g] - mm(ta[g], bd(t[g])) for g in ng]

    bev = [_dot(jnp.concatenate([rows2(b_m[g], e_m[g]), kb_t[g]], axis=0), bd(v[g])) for g in ng]
    kv = [bev[g][2 * n:3 * n] for g in ng]
    mw = [mm(t[g], bd2(kt[g], bev[g][0:n])) for g in ng]
    m1 = [mw[g][:, 0:qw] for g in ng]
    w2 = [mw[g][:, qw:2 * qw] for g in ng]
    cq = [mm(c_m[g], bd2(m1[g], w2[g])) for g in ng]
    qh = [rt[g].astype(F32) - cq[g][:, 0:qw] for g in ng]
    y0 = [bev[g][n:2 * n] - cq[g][:, qw:2 * qw] for g in ng]

    chains = len(groups) // SCAN_CHUNKS
    h = [h_ref[s] for s in range(chains)]
    for g, (d, q, j) in enumerate(groups):
        s = g % chains
        uy = _dot(rows2(m1[g], qh[g]), bd(h[s]))
        (y0_ref, y1_ref)[d][row0(d, j):row0(d, j) + n, q * qw:(q + 1) * qw] = uy[n:2 * n] + y0[g]
        h[s] = pc[g] * (h[s] - _dot(bt_t[g], bd(uy[0:n] + w2[g])) + kv[g])
    for s in range(chains):
        h_ref[s] = h[s]


def _rwkv_scan(v, ops0, ops1, *, n_batch, seq, ctx_len):
    nt = v.shape[0]
    n = CHUNK * SCAN_CHUNKS
    assert seq % n == 0 and ctx_len % n == 0
    lat_c = seq // n
    ctx_c = ctx_len // n
    ctx0 = n_batch * lat_c

    def idx_f(bi, c):
        return (jnp.where(c < ctx_c, ctx0 + bi * ctx_c + c, bi * lat_c + (c - ctx_c)), 0)

    def idx_r(bi, c):
        return (jnp.where(c < ctx_c, ctx0 + bi * ctx_c + (ctx_c - 1 - c), bi * lat_c + (lat_c - 1 - (c - ctx_c))), 0)

    spec_f = pl.BlockSpec((n, RWKV_WIDTH), idx_f)
    spec_r = pl.BlockSpec((n, RWKV_WIDTH), idx_r)
    out = jax.ShapeDtypeStruct((nt, RWKV_WIDTH), F32)
    return pl.pallas_call(
        _scan_kernel,
        out_shape=(out, out),
        grid=(n_batch, ctx_c + lat_c),
        in_specs=[spec_f] * 6 + [spec_r] * 6,
        out_specs=(spec_f, spec_r),
        scratch_shapes=[pltpu.VMEM((2 * HEADS // QUAD, CHUNK, QUAD_W), F32)],
        compiler_params=_params(("parallel", "arbitrary")),
        name="rwkv_scan",
    )(v, *ops0, v, *ops1)


def _mix_out_kernel(*refs, lat_tiles):
    (yf_ref, yb_ref, g_ref, bon_ref, lng_ref, lnb_ref, ones_ref, woa_ref, wor_ref, xl_ref, xc_ref, mod_ref, pg_ref,
     att_ref) = refs[:14]
    o_ref = refs[-1]
    is_lat = pl.program_id(0) < lat_tiles
    ones_bd = ones_ref[...]
    y = yf_ref[...] + yb_ref[...]
    inv_n = 1.0 / RWKV_HEAD
    mu = _seg_sum(y, ones_bd) * inv_n
    yc = y - mu
    var = _seg_sum(yc * yc, ones_bd) * inv_n
    yn = yc * lax.rsqrt(var + LNX_EPS) * lng_ref[...] + lnb_ref[...]
    rw = (yn + bon_ref[...]) * g_ref[...]
    att = att_ref[...]
    if len(refs) == 16:
        att = jnp.where(is_lat, att, refs[14][...])
    mix = _dot(att, woa_ref[...]) + _dot(rw.astype(BF16), wor_ref[...])
    x = jnp.where(is_lat, xl_ref[...], xc_ref[...])
    o_ref[...] = x + mod_ref[0, 2:3, :] * _rms(mix, pg_ref[...])


def _mix_out(yf, yb, att_lat, att_ctx, g, bon, lnx_g, lnx_b, ones_bd, wo_att, wo_rw, x_lat, x_ctx, mods, post_g, *,
             rows, n_lat, seq):
    tm = TOKEN_TILE
    lat_tiles = n_lat // tm
    n_batch = n_lat // seq
    w = RWKV_WIDTH

    def mod_idx(i):
        return (jnp.where(i < lat_tiles, (i * tm) // seq, n_batch), 0, 0)

    const = lambda i: (0, 0)
    row = lambda i: (i, 0)
    in_specs = [
        pl.BlockSpec((tm, w), row),
        pl.BlockSpec((tm, w), row),
        pl.BlockSpec((tm, w), row),
        pl.BlockSpec((tm, w), row),
        pl.BlockSpec((1, w), const),
        pl.BlockSpec((1, w), const),
        pl.BlockSpec((w, w), const),
        pl.BlockSpec((w, D_MODEL), const),
        pl.BlockSpec((w, D_MODEL), const),
        *_stream_specs(tm, lat_tiles, x_ctx is x_lat),
        pl.BlockSpec((1, N_MOD, D_MODEL), mod_idx),
        pl.BlockSpec((1, D_MODEL), const),
        pl.BlockSpec((tm, w), lambda i: (jnp.minimum(i, lat_tiles - 1), 0)),
    ]
    args = [yf, yb, g, bon, lnx_g, lnx_b, ones_bd, wo_att, wo_rw, x_lat, x_ctx, mods, post_g, att_lat]
    if att_ctx is not None:
        in_specs.append(pl.BlockSpec((tm, w), lambda i: (jnp.maximum(i - lat_tiles, 0), 0)))
        args.append(att_ctx)
    return pl.pallas_call(
        functools.partial(_mix_out_kernel, lat_tiles=lat_tiles),
        out_shape=jax.ShapeDtypeStruct((rows, D_MODEL), F32),
        grid=(rows // tm,),
        in_specs=in_specs,
        out_specs=pl.BlockSpec((tm, D_MODEL), row),
        compiler_params=_params(("parallel",)),
        name="mix_out",
    )(*args)


def _route(logits_t, bias_col):
    scores = jax.nn.sigmoid(logits_t[0:N_EXPERTS, :])
    biased = scores + bias_col
    s_rows = [scores[e:e + 1, :] for e in range(N_EXPERTS)]
    b_rows = [biased[e:e + 1, :] for e in range(N_EXPERTS)]
    npg = EXPERTS_PER_GROUP
    group_scores = []
    for gi in range(N_GROUPS):
        bg = b_rows[gi * npg:(gi + 1) * npg]
        best_pair = None
        for i in range(npg):
            for j in range(i + 1, npg):
                pair = bg[i] + bg[j]
                best_pair = pair if best_pair is None else jnp.maximum(best_pair, pair)
        group_scores.append(best_pair)
    best = group_scores[0]
    best_idx = jnp.zeros(best.shape, jnp.int32)
    for gi in range(1, N_GROUPS):
        upd = group_scores[gi] > best
        best = jnp.where(upd, group_scores[gi], best)
        best_idx = jnp.where(upd, gi, best_idx)
    pick = lambda rows, j: functools.reduce(
        lambda acc, gi: jnp.where(best_idx == gi, rows[gi * npg + j], acc), range(1, N_GROUPS), rows[j])
    bb = [pick(b_rows, j) for j in range(npg)]
    ss = [pick(s_rows, j) for j in range(npg)]
    weights = []
    for j in range(npg):
        rank = jnp.zeros(best.shape, jnp.int32)
        for i in range(npg):
            if i == j:
                continue
            beats = (bb[i] > bb[j]) | ((bb[i] == bb[j]) & (i < j)) if i < j else (bb[i] > bb[j])
            rank = rank + beats.astype(jnp.int32)
        weights.append(jnp.where(rank < 2, ss[j], 0.0))
    den = weights[0] + weights[1] + weights[2] + weights[3]
    gates = [wj / den for wj in weights]
    return [jnp.where(best_idx == e // npg, gates[e % npg], 0.0) for e in range(N_EXPERTS)]


def _moe_kernel(x_ref, mod_ref, pre_ref, post_ref, rw_ref, rb_ref, wgu_ref, wd_ref, o_ref, h_scr, gate_scr, acc_scr):
    e = pl.program_id(1)
    n_e = pl.num_programs(1)
    tm = x_ref.shape[0]

    @pl.when(e == 0)
    def _():
        h = _rms(x_ref[...], pre_ref[...]) * (1.0 + mod_ref[0, 4:5, :]) + mod_ref[0, 3:4, :]
        h_hi = h.astype(BF16)
        h_scr[...] = h_hi
        h_lo = (h - h_hi.astype(F32)).astype(BF16)
        rw = rw_ref[...]
        rw_hi = rw.astype(BF16)
        rw_lo = (rw - rw_hi.astype(F32)).astype(BF16)
        logits_t = _dot_nt(rw_hi, h_hi) + _dot_nt(rw_lo, h_hi) + _dot_nt(rw_hi, h_lo)
        rows = _route(logits_t, rb_ref[...])
        sub = lax.broadcasted_iota(jnp.int32, (LANES, tm), 0)
        gates_t = jnp.where(sub == N_EXPERTS, 1.0, 0.0)
        for k in range(N_EXPERTS):
            gates_t = jnp.where(sub == k, rows[k], gates_t)
        gate_scr[...] = gates_t.T
        acc_scr[...] = jnp.zeros_like(acc_scr)

    lane = lax.broadcasted_iota(jnp.int32, (tm, LANES), 1)
    gate = jnp.sum(jnp.where(lane == e, gate_scr[...], 0.0), axis=1, keepdims=True)
    gu = _dot(h_scr[...], wgu_ref[0])
    g_part = gu[:, 0:EXPERT_HIDDEN]
    act = g_part * jax.nn.sigmoid(g_part) * gu[:, EXPERT_HIDDEN:] * gate
    acc_scr[...] += _dot(act.astype(BF16), wd_ref[0])

    @pl.when(e == n_e - 1)
    def _():
        o_ref[...] = x_ref[...] + mod_ref[0, 5:6, :] * _rms(acc_scr[...], post_ref[...])


def _moe(x_all, mods, pre_g, post_g, router_wt, router_b, wgu, wd, *, rows, n_lat, seq, tm):
    lat_tiles = n_lat // tm
    n_batch = n_lat // seq
    n_e = wgu.shape[0]

    def mod_idx(i, e):
        return (jnp.where(i < lat_tiles, (i * tm) // seq, n_batch), 0, 0)

    const = lambda i, e: (0, 0)
    row = lambda i, e: (i, 0)
    return pl.pallas_call(
        _moe_kernel,
        out_shape=jax.ShapeDtypeStruct((rows, D_MODEL), F32),
        grid=(rows // tm, n_e),
        in_specs=[
            pl.BlockSpec((tm, D_MODEL), row),
            pl.BlockSpec((1, N_MOD, D_MODEL), mod_idx),
            pl.BlockSpec((1, D_MODEL), const),
            pl.BlockSpec((1, D_MODEL), const),
            pl.BlockSpec((LANES, D_MODEL), const),
            pl.BlockSpec((N_EXPERTS, 1), const),
            pl.BlockSpec((1, D_MODEL, 2 * EXPERT_HIDDEN), lambda i, e: (e, 0, 0)),
            pl.BlockSpec((1, EXPERT_HIDDEN, D_MODEL), lambda i, e: (e, 0, 0)),
        ],
        out_specs=pl.BlockSpec((tm, D_MODEL), row),
        scratch_shapes=[
            pltpu.VMEM((tm, D_MODEL), BF16),
            pltpu.VMEM((tm, LANES), F32),
            pltpu.VMEM((tm, D_MODEL), F32),
        ],
        compiler_params=_params(("parallel", "arbitrary")),
        name="moe",
    )(x_all, mods, pre_g, post_g, router_wt, router_b, wgu, wd)


def _pack_in_proj(w_in, shift_mu):
    d = w_in.shape[0]
    z = lambda n: jnp.zeros((d, n), w_in.dtype)
    kr = w_in[:, Q_RANK + KV_RANK:MLA_COLS]
    half = MLA_ROPE // 2
    kr_sw = jnp.concatenate([kr[:, half:], kr[:, :half]], axis=1)
    pad = LANES - MLA_NOPE - MLA_ROPE
    packed = jnp.concatenate([
        w_in[:, :Q_RANK + KV_RANK],
        z(MLA_NOPE), kr, z(pad),
        z(MLA_NOPE), kr_sw, z(pad),
        w_in[:, MLA_COLS:], z(RWKV_PAD - RWKV_COLS),
    ], axis=1)
    mu = jnp.pad(shift_mu, ((0, 0), (0, RWKV_PAD - RWKV_COLS)))
    return packed.astype(BF16), mu


def _pack_mla(w_uq, w_ukv):
    half = MLA_ROPE // 2
    dq = MLA_NOPE + MLA_ROPE
    q3 = w_uq.reshape(Q_RANK, HEADS, dq)
    zq = lambda n: jnp.zeros((Q_RANK, HEADS, n), w_uq.dtype)
    pad = HEAD_SLOT - dq
    wq1 = jnp.concatenate([q3, zq(pad)], axis=2).reshape(Q_RANK, HEADS * HEAD_SLOT)
    wq2 = jnp.concatenate([zq(MLA_NOPE), q3[:, :, MLA_NOPE + half:], q3[:, :, MLA_NOPE:MLA_NOPE + half], zq(pad)],
                          axis=2).reshape(Q_RANK, HEADS * HEAD_SLOT)
    kv3 = w_ukv.reshape(KV_RANK, HEADS, MLA_NOPE + MLA_V)
    wuk = jnp.concatenate([kv3[:, :, :MLA_NOPE], jnp.zeros((KV_RANK, HEADS, HEAD_SLOT - MLA_NOPE), w_ukv.dtype)],
                          axis=2).reshape(KV_RANK, HEADS * HEAD_SLOT)
    wuv = kv3[:, :, MLA_NOPE:].reshape(KV_RANK, HEADS * MLA_V)
    return wq1.astype(BF16), wq2.astype(BF16), wuk.astype(BF16), wuv.astype(BF16)


def _block_diag2(m):
    r, c = m.shape[1], m.shape[2]
    z = jnp.zeros((r, c), m.dtype)
    return jnp.concatenate([jnp.concatenate([m[0], z], axis=1), jnp.concatenate([z, m[1]], axis=1)], axis=0)


def _rope_tables(seq, ctx_len):
    axis_dim = MLA_ROPE // 2
    t = jnp.arange(seq, dtype=jnp.int32)
    row = (t // GRID_W).astype(F32)
    col = (t % GRID_W).astype(F32)
    inv_freq = ROPE_BASE ** (-jnp.arange(0, axis_dim, 2, dtype=F32) / axis_dim)
    ang = jnp.concatenate([row[:, None] * inv_freq, col[:, None] * inv_freq], axis=-1)
    cos = jnp.concatenate([jnp.cos(ang), jnp.ones((ctx_len, axis_dim), F32)], axis=0)
    sin = jnp.concatenate([jnp.sin(ang), jnp.zeros((ctx_len, axis_dim), F32)], axis=0)
    n = seq + ctx_len
    ones = jnp.ones((n, MLA_NOPE), F32)
    z_nope = jnp.zeros((n, MLA_NOPE), F32)
    z_pad = jnp.zeros((n, HEAD_SLOT - MLA_NOPE - MLA_ROPE), F32)
    cq = jnp.concatenate([ones, cos, cos, z_pad], axis=1)
    sq = jnp.concatenate([z_nope, -sin, sin, z_pad], axis=1)
    ck = jnp.concatenate([z_nope, cos, cos, z_pad], axis=1)
    return cq, sq, ck


def kernel(x, c, ctx, c_ctx, ada_w, ada_b, mix_pre_g, mix_post_g, ffn_pre_g, ffn_post_g, w_in, q_norm_g, kv_norm_g,
           w_uq, w_ukv, shift_mu, decay_w0, decay_w2, iclr_a0, iclr_a2, gate_g2, k_k, k_a, r_k, lnx_g, lnx_b, w_out,
           router_w, router_bias, exp_w_gate, exp_w_up, exp_w_down, sh_w_gate, sh_w_up, sh_w_down):
    n_batch, seq, d = x.shape
    ctx_len = ctx.shape[1]
    depth = ada_w.shape[0]
    assert d == D_MODEL and seq % TOKEN_TILE == 0 and ctx_len % TOKEN_TILE == 0 and seq % ctx_len == 0
    assert seq % GRID_W == 0 and n_batch + 1 <= 8
    n_lat = n_batch * seq
    n_ctx = n_batch * ctx_len
    moe_tile = 1024 if (n_lat % 1024 == 0 and n_ctx % 1024 == 0 and seq % 1024 == 0) else TOKEN_TILE

    x_lat, x_ctx = x.reshape(n_lat, d), ctx.reshape(n_ctx, d)
    cond_rows = jnp.concatenate([c, c_ctx[None], jnp.zeros((8 - n_batch - 1, d), F32)], axis=0)
    mods_all = _ada_modulation(cond_rows, ada_w, ada_b).reshape(depth, 8, N_MOD, d)

    cq_t, sq_t, ck_t = _rope_tables(seq, ctx_len)
    w = RWKV_WIDTH
    ones_bd = (jnp.arange(w)[:, None] // RWKV_HEAD == jnp.arange(w)[None, :] // RWKV_HEAD).astype(BF16)
    ti = jnp.arange(TOKEN_TILE)
    same_chunk = (ti[:, None] // CHUNK) == (ti[None, :] // CHUNK)
    ones_c = same_chunk.astype(BF16)
    tri_f = jnp.logical_and(same_chunk, ti[None, :] <= ti[:, None]).astype(BF16)
    tri_r = jnp.logical_and(same_chunk, ti[None, :] >= ti[:, None]).astype(BF16)
    router_wt = jnp.pad(router_w.T, ((0, LANES - N_EXPERTS), (0, 0)))
    router_b = router_bias.reshape(N_EXPERTS, 1)
    row1 = lambda a: a.reshape(1, -1)

    for l in range(depth):
        ctx_out = l < depth - 1
        mods = mods_all[l]
        rows = n_lat + n_ctx if ctx_out else n_lat

        w_in_p, mu = _pack_in_proj(w_in[l], shift_mu[l])
        wq1, wq2, wuk, wuv = _pack_mla(w_uq[l], w_ukv[l])
        q, k, v, p_rw = _in_proj(x_lat, x_ctx, mods, row1(mix_pre_g[l]), w_in_p, row1(q_norm_g[l]),
                                 row1(kv_norm_g[l]), wq1, wq2, wuk, wuv, cq_t, sq_t, ck_t, n_lat=n_lat, n_ctx=n_ctx,
                                 seq=seq, ctx_len=ctx_len)
        att = _attention(q, k, v, n_batch=n_batch, seq=seq, ctx_len=ctx_len, latent=True)
        att_c = _attention(q, k, v, n_batch=n_batch, seq=seq, ctx_len=ctx_len, latent=False) if ctx_out else None

        g2p = jnp.pad(gate_g2[l], ((0, GATE_PAD - GATE_LORA), (0, 0))).astype(BF16)
        prep = _rwkv_prep(
            p_rw, mu, _block_diag2(decay_w2[l]).astype(BF16), decay_w0[l].reshape(1, 2 * w),
            _block_diag2(iclr_a2[l]).astype(BF16), iclr_a0[l].reshape(1, 2 * w), g2p,
            row1(k_k[l]), row1(k_a[l]), row1(r_k[l]), ones_bd, tri_f, tri_r, ones_c,
            n_lat=n_lat, seq=seq, ctx_len=ctx_len)
        vv, g, bon = prep[0], prep[11], prep[12]
        yf, yb = _rwkv_scan(vv, prep[1:6], prep[6:11], n_batch=n_batch, seq=seq, ctx_len=ctx_len)

        wo = w_out[l].astype(BF16)
        x_all = _mix_out(yf, yb, att, att_c, g, bon, row1(lnx_g[l]), row1(lnx_b[l]), ones_bd, wo[:w], wo[w:], x_lat,
                         x_ctx, mods, row1(mix_post_g[l]), rows=rows, n_lat=n_lat, seq=seq)

        wgu = jnp.concatenate([
            jnp.concatenate([exp_w_gate[l], exp_w_up[l]], axis=2),
            jnp.concatenate([sh_w_gate[l], sh_w_up[l]], axis=1)[None]], axis=0).astype(BF16)
        wd = jnp.concatenate([exp_w_down[l], sh_w_down[l][None]], axis=0).astype(BF16)
        x_all = _moe(x_all, mods, row1(ffn_pre_g[l]), row1(ffn_post_g[l]), router_wt, router_b, wgu, wd,
                     rows=rows, n_lat=n_lat, seq=seq, tm=moe_tile)
        x_lat = x_ctx = x_all

    return x_all.reshape(n_batch, seq, d)
```

```python
import functools
import math

import jax
import jax.numpy as jnp
from jax import lax
from jax.experimental import pallas as pl
from jax.experimental.pallas import tpu as pltpu

F32 = jnp.float32
BF16 = jnp.bfloat16
HIGHEST = lax.Precision.HIGHEST

D_MODEL = 1024
N_MOD = 6
NORM_EPS = 1e-6
GRID_W = 64
ROPE_BASE = 10000.0

HEADS = 8
MLA_NOPE = 64
MLA_ROPE = 32
MLA_V = 64
Q_RANK = 256
KV_RANK = 128
MLA_COLS = Q_RANK + KV_RANK + MLA_ROPE

RWKV_HEAD = 64
RWKV_WIDTH = HEADS * RWKV_HEAD
DECAY_LORA = 64
ICLR_LORA = 64
GATE_LORA = 160
RWKV_COLS = 3 * RWKV_WIDTH + 2 * DECAY_LORA + 2 * ICLR_LORA + GATE_LORA
RWKV_PAD = 2048
GATE_PAD = 256
LNX_EPS = 64e-5

N_EXPERTS = 16
N_GROUPS = 4
EXPERTS_PER_GROUP = 4
EXPERT_HIDDEN = 256

LANES = 128
HEAD_SLOT = 128
IN_PACKED = Q_RANK + KV_RANK + 2 * LANES + RWKV_PAD
TOKEN_TILE = 256
ATTN_KV_BLOCK = 1024
ATTN_Q_TILE = 512
CHUNK = 64
VMEM_LIMIT = 48 * 1024 * 1024


def _dot(a, b):
    return jnp.dot(a, b, preferred_element_type=F32)


def _dot_nt(a, b):
    return lax.dot_general(a, b, (((1,), (1,)), ((), ())), preferred_element_type=F32)


def _dot_tn(a, b):
    return lax.dot_general(a, b, (((0,), (0,)), ((), ())), preferred_element_type=F32)


def _rms(x, g):
    return x * lax.rsqrt(jnp.mean(x * x, axis=-1, keepdims=True) + NORM_EPS) * g


def _seg_sum(x, ones_bd):
    hi = x.astype(BF16)
    lo = (x - hi.astype(F32)).astype(BF16)
    return _dot(hi, ones_bd) + _dot(lo, ones_bd)


def _params(sem):
    return pltpu.CompilerParams(dimension_semantics=sem, vmem_limit_bytes=VMEM_LIMIT)


def _ada_kernel(c_ref, w_ref, b_ref, o_ref):
    c = c_ref[...]
    cond = c * jax.nn.sigmoid(c)
    o_ref[0] = jnp.dot(cond, w_ref[0], precision=HIGHEST, preferred_element_type=F32) + b_ref[0]


def _ada_modulation(cond_rows, ada_w, ada_b):
    depth, d, n = ada_w.shape
    tn = 1536
    rows = cond_rows.shape[0]
    return pl.pallas_call(
        _ada_kernel,
        out_shape=jax.ShapeDtypeStruct((depth, rows, n), F32),
        grid=(depth, n // tn),
        in_specs=[
            pl.BlockSpec((rows, d), lambda l, j: (0, 0)),
            pl.BlockSpec((1, d, tn), lambda l, j: (l, 0, j)),
            pl.BlockSpec((1, 1, tn), lambda l, j: (l, 0, j)),
        ],
        out_specs=pl.BlockSpec((1, rows, tn), lambda l, j: (l, 0, j)),
        compiler_params=_params(("parallel", "parallel")),
        name="ada_modulation",
    )(cond_rows, ada_w, ada_b.reshape(depth, 1, n))


def _stream_specs(tm, lat_tiles, merged):
    off = lat_tiles if merged else 0
    return (pl.BlockSpec((tm, D_MODEL), lambda i: (jnp.minimum(i, lat_tiles - 1), 0)),
            pl.BlockSpec((tm, D_MODEL), lambda i: (jnp.maximum(i - lat_tiles, 0) + off, 0)))


def _in_proj_kernel(xl_ref, xc_ref, mod_ref, g_ref, win_ref, qng_ref, kvng_ref, wq1_ref, wq2_ref, wuk_ref, wuv_ref,
                    cq_ref, sq_ref, ck_ref, q_out, k_out, v_out, p_out, *, q_scale, lat_tiles):
    x = jnp.where(pl.program_id(0) < lat_tiles, xl_ref[...], xc_ref[...])
    shift = mod_ref[0, 0:1, :]
    scale = mod_ref[0, 1:2, :]
    h = _rms(x, g_ref[...]) * (1.0 + scale) + shift
    p = _dot(h.astype(BF16), win_ref[...])
    c_q = p[:, 0:Q_RANK]
    c_kv = p[:, Q_RANK:Q_RANK + KV_RANK]
    kr_a = p[:, Q_RANK + KV_RANK:Q_RANK + KV_RANK + LANES]
    kr_b = p[:, Q_RANK + KV_RANK + LANES:Q_RANK + KV_RANK + 2 * LANES]
    p_out[...] = p[:, Q_RANK + KV_RANK + 2 * LANES:]

    tile8 = lambda t: jnp.concatenate([t] * HEADS, axis=1)
    cqn = _rms(c_q, qng_ref[...]).astype(BF16)
    q = _dot(cqn, wq1_ref[...]) * tile8(cq_ref[...]) + _dot(cqn, wq2_ref[...]) * tile8(sq_ref[...])
    q_out[...] = (q * q_scale).astype(BF16)

    ckvn = _rms(c_kv, kvng_ref[...]).astype(BF16)
    k_rot = kr_a * ck_ref[...] + kr_b * sq_ref[...]
    k_out[...] = (_dot(ckvn, wuk_ref[...]) + tile8(k_rot)).astype(BF16)
    v_out[...] = _dot_nt(wuv_ref[...], ckvn).astype(BF16)


def _in_proj(x_lat, x_ctx, mods, g, w_in_p, qng, kvng, wq1, wq2, wuk, wuv, cq_t, sq_t, ck_t, *, n_lat, n_ctx, seq,
             ctx_len):
    nt = n_lat + n_ctx
    tm = TOKEN_TILE
    lat_tiles = n_lat // tm
    n_batch = n_lat // seq

    def mod_idx(i):
        return (jnp.where(i < lat_tiles, (i * tm) // seq, n_batch), 0, 0)

    def tab_idx(i):
        return (jnp.where(i < lat_tiles, i % (seq // tm), seq // tm + (i - lat_tiles) % (ctx_len // tm)), 0)

    const = lambda i: (0, 0)
    row = lambda i: (i, 0)
    qw = HEADS * HEAD_SLOT
    return pl.pallas_call(
        functools.partial(_in_proj_kernel, q_scale=float((MLA_NOPE + MLA_ROPE) ** -0.5 * math.log2(math.e)),
                          lat_tiles=lat_tiles),
        out_shape=(
            jax.ShapeDtypeStruct((nt, qw), BF16),
            jax.ShapeDtypeStruct((nt, qw), BF16),
            jax.ShapeDtypeStruct((HEADS * MLA_V, nt), BF16),
            jax.ShapeDtypeStruct((nt, RWKV_PAD), F32),
        ),
        grid=(nt // tm,),
        in_specs=[
            *_stream_specs(tm, lat_tiles, x_ctx is x_lat),
            pl.BlockSpec((1, N_MOD, D_MODEL), mod_idx),
            pl.BlockSpec((1, D_MODEL), const),
            pl.BlockSpec((D_MODEL, IN_PACKED), const),
            pl.BlockSpec((1, Q_RANK), const),
            pl.BlockSpec((1, KV_RANK), const),
            pl.BlockSpec((Q_RANK, qw), const),
            pl.BlockSpec((Q_RANK, qw), const),
            pl.BlockSpec((KV_RANK, qw), const),
            pl.BlockSpec((HEADS * MLA_V, KV_RANK), const),
            pl.BlockSpec((tm, HEAD_SLOT), tab_idx),
            pl.BlockSpec((tm, HEAD_SLOT), tab_idx),
            pl.BlockSpec((tm, HEAD_SLOT), tab_idx),
        ],
        out_specs=(
            pl.BlockSpec((tm, qw), row),
            pl.BlockSpec((tm, qw), row),
            pl.BlockSpec((HEADS * MLA_V, tm), lambda i: (0, i)),
            pl.BlockSpec((tm, RWKV_PAD), row),
        ),
        compiler_params=_params(("parallel",)),
        name="in_proj",
    )(x_lat, x_ctx, mods, g, w_in_p, qng, kvng, wq1, wq2, wuk, wuv, cq_t, sq_t, ck_t)


def _attn_kernel(q_ref, kc_ref, vc_ref, *rest):
    o_ref = rest[-1]
    blocks = [(kc_ref, vc_ref, 0, kc_ref.shape[0])]
    if len(rest) == 3:
        kl_ref, vl_ref = rest[0], rest[1]
        kb = min(ATTN_KV_BLOCK, kl_ref.shape[0])
        blocks += [(kl_ref, vl_ref, s0, kb) for s0 in range(0, kl_ref.shape[0], kb)]
    heads = range(2)
    hs = [slice(h * HEAD_SLOT, (h + 1) * HEAD_SLOT) for h in heads]
    vs = [slice(h * MLA_V, (h + 1) * MLA_V) for h in heads]
    q = [q_ref[:, hs[h]] for h in heads]

    def scores(j):
        k_ref, _, s0, size = blocks[j]
        return [_dot_nt(k_ref[s0:s0 + size, hs[h]], q[h]) for h in heads]

    m, den, acc = [None] * 2, [None] * 2, [None] * 2
    s_cur = scores(0)
    for j, (_, vt_ref, s0, size) in enumerate(blocks):
        s_next = scores(j + 1) if j + 1 < len(blocks) else None
        for h in heads:
            s = s_cur[h]
            m_blk = jnp.max(s, axis=0, keepdims=True)
            if j == 0:
                m[h] = m_blk
                p = jnp.exp2(s - m_blk)
                den[h] = jnp.sum(p, axis=0, keepdims=True)
                acc[h] = _dot(vt_ref[vs[h], s0:s0 + size], p.astype(BF16))
            else:
                m_new = jnp.maximum(m[h], m_blk)
                alpha = jnp.exp2(m[h] - m_new)
                p = jnp.exp2(s - m_new)
                den[h] = alpha * den[h] + jnp.sum(p, axis=0, keepdims=True)
                acc[h] = alpha * acc[h] + _dot(vt_ref[vs[h], s0:s0 + size], p.astype(BF16))
                m[h] = m_new
        s_cur = s_next
    out_t = jnp.concatenate([acc[h] / den[h] for h in heads], axis=0)
    o_ref[...] = out_t.T.astype(o_ref.dtype)


def _attention(q, k, v_t, *, n_batch, seq, ctx_len, latent):
    tq = ATTN_Q_TILE if latent else TOKEN_TILE
    seg = seq if latent else ctx_len
    assert seg % tq == 0
    q_tiles = seg // tq
    q_blk0 = 0 if latent else (n_batch * seq) // tq
    ctx_blk0 = (n_batch * seq) // ctx_len
    kv_ctx = lambda b, hp, j: (ctx_blk0 + b, hp)
    kv_lat = lambda b, hp, j: (b, hp)
    in_specs = [
        pl.BlockSpec((tq, 2 * HEAD_SLOT), lambda b, hp, j: (q_blk0 + b * q_tiles + j, hp)),
        pl.BlockSpec((ctx_len, 2 * HEAD_SLOT), kv_ctx),
        pl.BlockSpec((2 * MLA_V, ctx_len), lambda b, hp, j: (hp, ctx_blk0 + b)),
    ]
    args = [q, k, v_t]
    if latent:
        in_specs += [pl.BlockSpec((seq, 2 * HEAD_SLOT), kv_lat),
                     pl.BlockSpec((2 * MLA_V, seq), lambda b, hp, j: (hp, b))]
        args += [k, v_t]
    return pl.pallas_call(
        _attn_kernel,
        out_shape=jax.ShapeDtypeStruct((n_batch * seg, HEADS * MLA_V), BF16),
        grid=(n_batch, HEADS // 2, q_tiles),
        in_specs=in_specs,
        out_specs=pl.BlockSpec((tq, 2 * MLA_V), lambda b, hp, j: (b * q_tiles + j, hp)),
        compiler_params=_params(("parallel", "parallel", "arbitrary")),
        name="attention_lat" if latent else "attention_ctx",
    )(*args)


def _split2(x):
    hi = x.astype(BF16)
    return hi, (x - hi.astype(F32)).astype(BF16)


def _chunk_transpose(x):
    n = CHUNK
    xt = x.T
    rows = []
    for c in range(x.shape[0] // n):
        rows.append(jnp.concatenate(
            [xt[h * n:(h + 1) * n, c * n:(c + 1) * n] for h in range(x.shape[1] // n)], axis=1))
    return jnp.concatenate(rows, axis=0)


def _rwkv_prep_kernel(p_ref, hp_ref, hn_ref, mu_ref, w2_ref, w0_ref, a2_ref, a0_ref, g2_ref, kk_ref, ka_ref, rk_ref,
                      ones_ref, trif_ref, trir_ref, onesc_ref, v_out, rt0_out, kt0_out, bt0_out, kb0_out, pc0_out,
                      rt1_out, kt1_out, bt1_out, kb1_out, pc1_out, g_out, bon_out, *, lat_tiles, seq_tiles,
                      ctx_tiles):
    i = pl.program_id(0)
    tm = p_ref.shape[0]
    is_lat = i < lat_tiles
    local = jnp.where(is_lat, i % seq_tiles, (i - lat_tiles) % ctx_tiles)
    seg = jnp.where(is_lat, seq_tiles, ctx_tiles)
    keep_prev = jnp.where(local == 0, 0.0, 1.0)
    keep_next = jnp.where(local == seg - 1, 0.0, 1.0)

    p = p_ref[...]
    row = lax.broadcasted_iota(jnp.int32, (tm, 1), 0)
    prev = jnp.where(row == 0, hp_ref[7:8, :] * keep_prev, pltpu.roll(p, 1, 0))
    nxt = jnp.where(row == tm - 1, hn_ref[0:1, :] * keep_next, pltpu.roll(p, tm - 1, 0))
    ps = p + mu_ref[0:1, :] * (prev - p) + mu_ref[1:2, :] * (nxt - p)

    w = RWKV_WIDTH
    r = ps[:, 0:w]
    k = ps[:, w:2 * w]
    v = ps[:, 2 * w:3 * w]
    wl = ps[:, 3 * w:3 * w + 2 * DECAY_LORA]
    al = ps[:, 3 * w + 2 * DECAY_LORA:3 * w + 2 * DECAY_LORA + 2 * ICLR_LORA]
    gl = ps[:, 3 * w + 2 * DECAY_LORA + 2 * ICLR_LORA:3 * w + 2 * DECAY_LORA + 2 * ICLR_LORA + GATE_PAD]

    g_out[...] = _dot(jax.nn.sigmoid(gl).astype(BF16), g2_ref[...])
    z = w0_ref[...] + _dot(jnp.tanh(wl).astype(BF16), w2_ref[...])
    logw = (-math.exp(-0.5)) * jax.nn.sigmoid(z)
    a = jax.nn.sigmoid(a0_ref[...] + _dot(al.astype(BF16), a2_ref[...]))

    ones_bd = ones_ref[...]
    kk = k * kk_ref[...]
    kk = kk * lax.rsqrt(_seg_sum(kk * kk, ones_bd) + 1e-12)
    ka = ka_ref[...]
    v_out[...] = v.astype(BF16)

    ke_sum = None
    for d, tri_ref, (rt_out, kt_out, bt_out, kb_out, pc_out) in (
            (0, trif_ref, (rt0_out, kt0_out, bt0_out, kb0_out, pc0_out)),
            (1, trir_ref, (rt1_out, kt1_out, bt1_out, kb1_out, pc1_out))):
        lw = logw[:, d * w:(d + 1) * w]
        a_d = a[:, d * w:(d + 1) * w]
        lw_hi, lw_lo = _split2(lw)
        cum = _dot(tri_ref[...], lw_hi) + _dot(tri_ref[...], lw_lo)
        tot = _dot(onesc_ref[...], lw_hi) + _dot(onesc_ref[...], lw_lo)
        e_neg = jnp.exp(-cum)
        ke = k * (1.0 + (a_d - 1.0) * ka)
        ke_sum = ke if ke_sum is None else ke_sum + ke
        rt_out[...] = (r * jnp.exp(cum)).astype(BF16)
        kt_out[...] = (kk * jnp.exp(cum - lw)).astype(BF16)
        bt_out[...] = _chunk_transpose(a_d * kk * e_neg).astype(BF16)
        kb_out[...] = _chunk_transpose(ke * e_neg).astype(BF16)
        pc_out[...] = _chunk_transpose(jnp.exp(tot))
    bon_out[...] = _seg_sum(r * ke_sum * rk_ref[...], ones_bd) * v


def _rwkv_prep(p_rw, mu, w2bd, w0, a2bd, a0, g2p, k_k, k_a, r_k, ones_bd, tri_f, tri_r, ones_c, *, n_lat, seq,
               ctx_len):
    nt = p_rw.shape[0]
    tm = TOKEN_TILE
    n_tiles = nt // tm
    halo = 8
    blocks8 = nt // halo
    const = lambda i: (0, 0)
    row = lambda i: (i, 0)
    w = RWKV_WIDTH
    o16 = jax.ShapeDtypeStruct((nt, w), BF16)
    o32 = jax.ShapeDtypeStruct((nt, w), F32)
    return pl.pallas_call(
        functools.partial(_rwkv_prep_kernel, lat_tiles=n_lat // tm, seq_tiles=seq // tm, ctx_tiles=ctx_len // tm),
        out_shape=(o16, o16, o16, o16, o16, o32, o16, o16, o16, o16, o32, o32, o32),
        grid=(n_tiles,),
        in_specs=[
            pl.BlockSpec((tm, RWKV_PAD), row),
            pl.BlockSpec((halo, RWKV_PAD), lambda i: (jnp.maximum(i * (tm // halo) - 1, 0), 0)),
            pl.BlockSpec((halo, RWKV_PAD), lambda i: (jnp.minimum((i + 1) * (tm // halo), blocks8 - 1), 0)),
            pl.BlockSpec((2, RWKV_PAD), const),
            pl.BlockSpec((2 * DECAY_LORA, 2 * w), const),
            pl.BlockSpec((1, 2 * w), const),
            pl.BlockSpec((2 * ICLR_LORA, 2 * w), const),
            pl.BlockSpec((1, 2 * w), const),
            pl.BlockSpec((GATE_PAD, w), const),
            pl.BlockSpec((1, w), const),
            pl.BlockSpec((1, w), const),
            pl.BlockSpec((1, w), const),
            pl.BlockSpec((w, w), const),
            pl.BlockSpec((tm, tm), const),
            pl.BlockSpec((tm, tm), const),
            pl.BlockSpec((tm, tm), const),
        ],
        out_specs=(pl.BlockSpec((tm, w), row),) * 13,
        compiler_params=_params(("parallel",)),
        name="rwkv_prep",
    )(p_rw, p_rw, p_rw, mu, w2bd, w0, a2bd, a0, g2p, k_k, k_a, r_k, ones_bd, tri_f, tri_r, ones_c)


QUAD = 2
QUAD_W = QUAD * RWKV_HEAD
SCAN_CHUNKS = 4


def _scan_kernel(v0_ref, rt0_ref, kt0_ref, bt0_ref, kb0_ref, pc0_ref, v1_ref, rt1_ref, kt1_ref, bt1_ref, kb1_ref,
                 pc1_ref, y0_ref, y1_ref, h_ref):
    @pl.when(pl.program_id(1) == 0)
    def _():
        h_ref[...] = jnp.zeros_like(h_ref)

    n, qw = CHUNK, QUAD_W
    t_i = lax.broadcasted_iota(jnp.int32, (n, qw), 0)
    s_i = lax.broadcasted_iota(jnp.int32, (n, qw), 1) % n
    bd_mask = (lax.broadcasted_iota(jnp.int32, (qw, qw), 0) // n) == (lax.broadcasted_iota(jnp.int32, (qw, qw), 1) // n)
    eye = jnp.where(t_i == s_i, 1.0, 0.0)
    same16 = (t_i // 16) == (s_i // 16)
    same32 = (t_i // 32) == (s_i // 32)
    sib32 = jnp.logical_and(same32, jnp.logical_not(same16))
    sib64 = jnp.logical_not(same32)
    incl = (s_i <= t_i, s_i >= t_i)
    strict = (s_i < t_i, s_i > t_i)

    groups = [(d, q, j) for j in range(SCAN_CHUNKS) for d in (0, 1) for q in range(HEADS // QUAD)]
    dirs = [d for d, _, _ in groups]
    ng = range(len(groups))
    row0 = lambda d, j: (SCAN_CHUNKS - 1 - j) * n if d else j * n
    refs = ((v0_ref, rt0_ref, kt0_ref, bt0_ref, kb0_ref, pc0_ref), (v1_ref, rt1_ref, kt1_ref, bt1_ref, kb1_ref, pc1_ref))
    ld = lambda k: [refs[d][k][row0(d, j):row0(d, j) + n, q * qw:(q + 1) * qw] for d, q, j in groups]
    v, rt, kt, bt_t, kb_t, pc = (ld(k) for k in range(6))

    def bd(x):
        return jnp.where(bd_mask, jnp.concatenate([x.astype(BF16)] * QUAD, axis=0), jnp.zeros((), BF16))

    mm = lambda x, w: _dot(x.astype(BF16), w)
    rows2 = lambda x, y: jnp.concatenate([x.astype(BF16), y.astype(BF16)], axis=0)

    bd2 = lambda x, y: jnp.concatenate([bd(x), bd(y)], axis=1)
    gram = [_dot(rows2(kt[g], rt[g]), bd2(bt_t[g], kb_t[g])) for g in ng]
    a_m = [jnp.where(strict[dirs[g]], gram[g][0:n, 0:qw], 0.0) for g in ng]
    c_m = [jnp.where(incl[dirs[g]], gram[g][n:2 * n, 0:qw], 0.0) for g in ng]
    b_m = [jnp.where(strict[dirs[g]], gram[g][0:n, qw:2 * qw], 0.0) for g in ng]
    e_m = [jnp.where(incl[dirs[g]], gram[g][n:2 * n, qw:2 * qw], 0.0) for g in ng]

    dd = [jnp.where(same16, a_m[g], 0.0) for g in ng]
    pw = [mm(dd[g], bd(dd[g])) for g in ng]
    t = [eye - dd[g] for g in ng]
    for _ in range(2):
        both = [_dot(rows2(t[g], pw[g]), bd(pw[g])) for g in ng]
        t = [t[g] + both[g][0:n] for g in ng]
        pw = [both[g][n:2 * n] for g in ng]
    t = [t[g] + mm(t[g], bd(pw[g])) for g in ng]
    for sib in (sib32, sib64):
        ta = [mm(t[g], bd(jnp.where(sib, a_m[g], 0.0))) for g in ng]
        t = [t[g] - mm(ta[g], bd(t[g])) for g in ng]

    bev = [_dot(jnp.concatenate([rows2(b_m[g], e_m[g]), kb_t[g]], axis=0), bd(v[g])) for g in ng]
    kv = [bev[g][2 * n:3 * n] for g in ng]
    mw = [mm(t[g], bd2(kt[g], bev[g][0:n])) for g in ng]
    m1 = [mw[g][:, 0:qw] for g in ng]
    w2 = [mw[g][:, qw:2 * qw] for g in ng]
    cq = [mm(c_m[g], bd2(m1[g], w2[g])) for g in ng]
    qh = [rt[g].astype(F32) - cq[g][:, 0:qw] for g in ng]
    y0 = [bev[g][n:2 * n] - cq[g][:, qw:2 * qw] for g in ng]

    chains = len(groups) // SCAN_CHUNKS
    h = [h_ref[s] for s in range(chains)]
    for g, (d, q, j) in enumerate(groups):
        s = g % chains
        uy = _dot(rows2(m1[g], qh[g]), bd(h[s]))
        (y0_ref, y1_ref)[d][row0(d, j):row0(d, j) + n, q * qw:(q + 1) * qw] = uy[n:2 * n] + y0[g]
        h[s] = pc[g] * (h[s] - _dot(bt_t[g], bd(uy[0:n] + w2[g])) + kv[g])
    for s in range(chains):
        h_ref[s] = h[s]


def _rwkv_scan(v, ops0, ops1, *, n_batch, seq, ctx_len):
    nt = v.shape[0]
    n = CHUNK * SCAN_CHUNKS
    assert seq % n == 0 and ctx_len % n == 0
    lat_c = seq // n
    ctx_c = ctx_len // n
    ctx0 = n_batch * lat_c

    def idx_f(bi, c):
        return (jnp.where(c < ctx_c, ctx0 + bi * ctx_c + c, bi * lat_c + (c - ctx_c)), 0)

    def idx_r(bi, c):
        return (jnp.where(c < ctx_c, ctx0 + bi * ctx_c + (ctx_c - 1 - c), bi * lat_c + (lat_c - 1 - (c - ctx_c))), 0)

    spec_f = pl.BlockSpec((n, RWKV_WIDTH), idx_f)
    spec_r = pl.BlockSpec((n, RWKV_WIDTH), idx_r)
    out = jax.ShapeDtypeStruct((nt, RWKV_WIDTH), F32)
    return pl.pallas_call(
        _scan_kernel,
        out_shape=(out, out),
        grid=(n_batch, ctx_c + lat_c),
        in_specs=[spec_f] * 6 + [spec_r] * 6,
        out_specs=(spec_f, spec_r),
        scratch_shapes=[pltpu.VMEM((2 * HEADS // QUAD, CHUNK, QUAD_W), F32)],
        compiler_params=_params(("parallel", "arbitrary")),
        name="rwkv_scan",
    )(v, *ops0, v, *ops1)


def _mix_out_kernel(*refs, lat_tiles):
    (yf_ref, yb_ref, g_ref, bon_ref, lng_ref, lnb_ref, ones_ref, woa_ref, wor_ref, xl_ref, xc_ref, mod_ref, pg_ref,
     att_ref) = refs[:14]
    o_ref = refs[-1]
    is_lat = pl.program_id(0) < lat_tiles
    ones_bd = ones_ref[...]
    y = yf_ref[...] + yb_ref[...]
    inv_n = 1.0 / RWKV_HEAD
    mu = _seg_sum(y, ones_bd) * inv_n
    yc = y - mu
    var = _seg_sum(yc * yc, ones_bd) * inv_n
    yn = yc * lax.rsqrt(var + LNX_EPS) * lng_ref[...] + lnb_ref[...]
    rw = (yn + bon_ref[...]) * g_ref[...]
    att = att_ref[...]
    if len(refs) == 16:
        att = jnp.where(is_lat, att, refs[14][...])
    mix = _dot(att, woa_ref[...]) + _dot(rw.astype(BF16), wor_ref[...])
    x = jnp.where(is_lat, xl_ref[...], xc_ref[...])
    o_ref[...] = x + mod_ref[0, 2:3, :] * _rms(mix, pg_ref[...])


def _mix_out(yf, yb, att_lat, att_ctx, g, bon, lnx_g, lnx_b, ones_bd, wo_att, wo_rw, x_lat, x_ctx, mods, post_g, *,
             rows, n_lat, seq):
    tm = TOKEN_TILE
    lat_tiles = n_lat // tm
    n_batch = n_lat // seq
    w = RWKV_WIDTH

    def mod_idx(i):
        return (jnp.where(i < lat_tiles, (i * tm) // seq, n_batch), 0, 0)

    const = lambda i: (0, 0)
    row = lambda i: (i, 0)
    in_specs = [
        pl.BlockSpec((tm, w), row),
        pl.BlockSpec((tm, w), row),
        pl.BlockSpec((tm, w), row),
        pl.BlockSpec((tm, w), row),
        pl.BlockSpec((1, w), const),
        pl.BlockSpec((1, w), const),
        pl.BlockSpec((w, w), const),
        pl.BlockSpec((w, D_MODEL), const),
        pl.BlockSpec((w, D_MODEL), const),
        *_stream_specs(tm, lat_tiles, x_ctx is x_lat),
        pl.BlockSpec((1, N_MOD, D_MODEL), mod_idx),
        pl.BlockSpec((1, D_MODEL), const),
        pl.BlockSpec((tm, w), lambda i: (jnp.minimum(i, lat_tiles - 1), 0)),
    ]
    args = [yf, yb, g, bon, lnx_g, lnx_b, ones_bd, wo_att, wo_rw, x_lat, x_ctx, mods, post_g, att_lat]
    if att_ctx is not None:
        in_specs.append(pl.BlockSpec((tm, w), lambda i: (jnp.maximum(i - lat_tiles, 0), 0)))
        args.append(att_ctx)
    return pl.pallas_call(
        functools.partial(_mix_out_kernel, lat_tiles=lat_tiles),
        out_shape=jax.ShapeDtypeStruct((rows, D_MODEL), F32),
        grid=(rows // tm,),
        in_specs=in_specs,
        out_specs=pl.BlockSpec((tm, D_MODEL), row),
        compiler_params=_params(("parallel",)),
        name="mix_out",
    )(*args)


def _route(logits_t, bias_col):
    scores = jax.nn.sigmoid(logits_t[0:N_EXPERTS, :])
    biased = scores + bias_col
    s_rows = [scores[e:e + 1, :] for e in range(N_EXPERTS)]
    b_rows = [biased[e:e + 1, :] for e in range(N_EXPERTS)]
    npg = EXPERTS_PER_GROUP
    group_scores = []
    for gi in range(N_GROUPS):
        bg = b_rows[gi * npg:(gi + 1) * npg]
        best_pair = None
        for i in range(npg):
            for j in range(i + 1, npg):
                pair = bg[i] + bg[j]
                best_pair = pair if best_pair is None else jnp.maximum(best_pair, pair)
        group_scores.append(best_pair)
    best = group_scores[0]
    best_idx = jnp.zeros(best.shape, jnp.int32)
    for gi in range(1, N_GROUPS):
        upd = group_scores[gi] > best
        best = jnp.where(upd, group_scores[gi], best)
        best_idx = jnp.where(upd, gi, best_idx)
    pick = lambda rows, j: functools.reduce(
        lambda acc, gi: jnp.where(best_idx == gi, rows[gi * npg + j], acc), range(1, N_GROUPS), rows[j])
    bb = [pick(b_rows, j) for j in range(npg)]
    ss = [pick(s_rows, j) for j in range(npg)]
    weights = []
    for j in range(npg):
        rank = jnp.zeros(best.shape, jnp.int32)
        for i in range(npg):
            if i == j:
                continue
            beats = (bb[i] > bb[j]) | ((bb[i] == bb[j]) & (i < j)) if i < j else (bb[i] > bb[j])
            rank = rank + beats.astype(jnp.int32)
        weights.append(jnp.where(rank < 2, ss[j], 0.0))
    den = weights[0] + weights[1] + weights[2] + weights[3]
    gates = [wj / den for wj in weights]
    return [jnp.where(best_idx == e // npg, gates[e % npg], 0.0) for e in range(N_EXPERTS)]


def _moe_kernel(x_ref, mod_ref, pre_ref, post_ref, rw_ref, rb_ref, wgu_ref, wd_ref, o_ref, h_scr, gate_scr, acc_scr):
    e = pl.program_id(1)
    n_e = pl.num_programs(1)
    tm = x_ref.shape[0]

    @pl.when(e == 0)
    def _():
        h = _rms(x_ref[...], pre_ref[...]) * (1.0 + mod_ref[0, 4:5, :]) + mod_ref[0, 3:4, :]
        h_hi = h.astype(BF16)
        h_scr[...] = h_hi
        h_lo = (h - h_hi.astype(F32)).astype(BF16)
        rw = rw_ref[...]
        rw_hi = rw.astype(BF16)
        rw_lo = (rw - rw_hi.astype(F32)).astype(BF16)
        logits_t = _dot_nt(rw_hi, h_hi) + _dot_nt(rw_lo, h_hi) + _dot_nt(rw_hi, h_lo)
        rows = _route(logits_t, rb_ref[...])
        sub = lax.broadcasted_iota(jnp.int32, (LANES, tm), 0)
        gates_t = jnp.where(sub == N_EXPERTS, 1.0, 0.0)
        for k in range(N_EXPERTS):
            gates_t = jnp.where(sub == k, rows[k], gates_t)
        gate_scr[...] = gates_t.T
        acc_scr[...] = jnp.zeros_like(acc_scr)

    lane = lax.broadcasted_iota(jnp.int32, (tm, LANES), 1)
    gate = jnp.sum(jnp.where(lane == e, gate_scr[...], 0.0), axis=1, keepdims=True)
    gu = _dot(h_scr[...], wgu_ref[0])
    g_part = gu[:, 0:EXPERT_HIDDEN]
    act = g_part * jax.nn.sigmoid(g_part) * gu[:, EXPERT_HIDDEN:] * gate
    acc_scr[...] += _dot(act.astype(BF16), wd_ref[0])

    @pl.when(e == n_e - 1)
    def _():
        o_ref[...] = x_ref[...] + mod_ref[0, 5:6, :] * _rms(acc_scr[...], post_ref[...])


def _moe(x_all, mods, pre_g, post_g, router_wt, router_b, wgu, wd, *, rows, n_lat, seq, tm):
    lat_tiles = n_lat // tm
    n_batch = n_lat // seq
    n_e = wgu.shape[0]

    def mod_idx(i, e):
        return (jnp.where(i < lat_tiles, (i * tm) // seq, n_batch), 0, 0)

    const = lambda i, e: (0, 0)
    row = lambda i, e: (i, 0)
    return pl.pallas_call(
        _moe_kernel,
        out_shape=jax.ShapeDtypeStruct((rows, D_MODEL), F32),
        grid=(rows // tm, n_e),
        in_specs=[
            pl.BlockSpec((tm, D_MODEL), row),
            pl.BlockSpec((1, N_MOD, D_MODEL), mod_idx),
            pl.BlockSpec((1, D_MODEL), const),
            pl.BlockSpec((1, D_MODEL), const),
            pl.BlockSpec((LANES, D_MODEL), const),
            pl.BlockSpec((N_EXPERTS, 1), const),
            pl.BlockSpec((1, D_MODEL, 2 * EXPERT_HIDDEN), lambda i, e: (e, 0, 0)),
            pl.BlockSpec((1, EXPERT_HIDDEN, D_MODEL), lambda i, e: (e, 0, 0)),
        ],
        out_specs=pl.BlockSpec((tm, D_MODEL), row),
        scratch_shapes=[
            pltpu.VMEM((tm, D_MODEL), BF16),
            pltpu.VMEM((tm, LANES), F32),
            pltpu.VMEM((tm, D_MODEL), F32),
        ],
        compiler_params=_params(("parallel", "arbitrary")),
        name="moe",
    )(x_all, mods, pre_g, post_g, router_wt, router_b, wgu, wd)


def _pack_in_proj(w_in, shift_mu):
    d = w_in.shape[0]
    z = lambda n: jnp.zeros((d, n), w_in.dtype)
    kr = w_in[:, Q_RANK + KV_RANK:MLA_COLS]
    half = MLA_ROPE // 2
    kr_sw = jnp.concatenate([kr[:, half:], kr[:, :half]], axis=1)
    pad = LANES - MLA_NOPE - MLA_ROPE
    packed = jnp.concatenate([
        w_in[:, :Q_RANK + KV_RANK],
        z(MLA_NOPE), kr, z(pad),
        z(MLA_NOPE), kr_sw, z(pad),
        w_in[:, MLA_COLS:], z(RWKV_PAD - RWKV_COLS),
    ], axis=1)
    mu = jnp.pad(shift_mu, ((0, 0), (0, RWKV_PAD - RWKV_COLS)))
    return packed.astype(BF16), mu


def _pack_mla(w_uq, w_ukv):
    half = MLA_ROPE // 2
    dq = MLA_NOPE + MLA_ROPE
    q3 = w_uq.reshape(Q_RANK, HEADS, dq)
    zq = lambda n: jnp.zeros((Q_RANK, HEADS, n), w_uq.dtype)
    pad = HEAD_SLOT - dq
    wq1 = jnp.concatenate([q3, zq(pad)], axis=2).reshape(Q_RANK, HEADS * HEAD_SLOT)
    wq2 = jnp.concatenate([zq(MLA_NOPE), q3[:, :, MLA_NOPE + half:], q3[:, :, MLA_NOPE:MLA_NOPE + half], zq(pad)],
                          axis=2).reshape(Q_RANK, HEADS * HEAD_SLOT)
    kv3 = w_ukv.reshape(KV_RANK, HEADS, MLA_NOPE + MLA_V)
    wuk = jnp.concatenate([kv3[:, :, :MLA_NOPE], jnp.zeros((KV_RANK, HEADS, HEAD_SLOT - MLA_NOPE), w_ukv.dtype)],
                          axis=2).reshape(KV_RANK, HEADS * HEAD_SLOT)
    wuv = kv3[:, :, MLA_NOPE:].reshape(KV_RANK, HEADS * MLA_V).T
    return wq1.astype(BF16), wq2.astype(BF16), wuk.astype(BF16), wuv.astype(BF16)


def _block_diag2(m):
    r, c = m.shape[1], m.shape[2]
    z = jnp.zeros((r, c), m.dtype)
    return jnp.concatenate([jnp.concatenate([m[0], z], axis=1), jnp.concatenate([z, m[1]], axis=1)], axis=0)


def _rope_tables(seq, ctx_len):
    axis_dim = MLA_ROPE // 2
    t = jnp.arange(seq, dtype=jnp.int32)
    row = (t // GRID_W).astype(F32)
    col = (t % GRID_W).astype(F32)
    inv_freq = ROPE_BASE ** (-jnp.arange(0, axis_dim, 2, dtype=F32) / axis_dim)
    ang = jnp.concatenate([row[:, None] * inv_freq, col[:, None] * inv_freq], axis=-1)
    cos = jnp.concatenate([jnp.cos(ang), jnp.ones((ctx_len, axis_dim), F32)], axis=0)
    sin = jnp.concatenate([jnp.sin(ang), jnp.zeros((ctx_len, axis_dim), F32)], axis=0)
    n = seq + ctx_len
    ones = jnp.ones((n, MLA_NOPE), F32)
    z_nope = jnp.zeros((n, MLA_NOPE), F32)
    z_pad = jnp.zeros((n, HEAD_SLOT - MLA_NOPE - MLA_ROPE), F32)
    cq = jnp.concatenate([ones, cos, cos, z_pad], axis=1)
    sq = jnp.concatenate([z_nope, -sin, sin, z_pad], axis=1)
    ck = jnp.concatenate([z_nope, cos, cos, z_pad], axis=1)
    return cq, sq, ck


def kernel(x, c, ctx, c_ctx, ada_w, ada_b, mix_pre_g, mix_post_g, ffn_pre_g, ffn_post_g, w_in, q_norm_g, kv_norm_g,
           w_uq, w_ukv, shift_mu, decay_w0, decay_w2, iclr_a0, iclr_a2, gate_g2, k_k, k_a, r_k, lnx_g, lnx_b, w_out,
           router_w, router_bias, exp_w_gate, exp_w_up, exp_w_down, sh_w_gate, sh_w_up, sh_w_down):
    n_batch, seq, d = x.shape
    ctx_len = ctx.shape[1]
    depth = ada_w.shape[0]
    assert d == D_MODEL and seq % TOKEN_TILE == 0 and ctx_len % TOKEN_TILE == 0 and seq % ctx_len == 0
    assert seq % GRID_W == 0 and n_batch + 1 <= 8
    n_lat = n_batch * seq
    n_ctx = n_batch * ctx_len
    moe_tile = 1024 if (n_lat % 1024 == 0 and n_ctx % 1024 == 0 and seq % 1024 == 0) else TOKEN_TILE

    x_lat, x_ctx = x.reshape(n_lat, d), ctx.reshape(n_ctx, d)
    cond_rows = jnp.concatenate([c, c_ctx[None], jnp.zeros((8 - n_batch - 1, d), F32)], axis=0)
    mods_all = _ada_modulation(cond_rows, ada_w, ada_b).reshape(depth, 8, N_MOD, d)

    cq_t, sq_t, ck_t = _rope_tables(seq, ctx_len)
    w = RWKV_WIDTH
    ones_bd = (jnp.arange(w)[:, None] // RWKV_HEAD == jnp.arange(w)[None, :] // RWKV_HEAD).astype(BF16)
    ti = jnp.arange(TOKEN_TILE)
    same_chunk = (ti[:, None] // CHUNK) == (ti[None, :] // CHUNK)
    ones_c = same_chunk.astype(BF16)
    tri_f = jnp.logical_and(same_chunk, ti[None, :] <= ti[:, None]).astype(BF16)
    tri_r = jnp.logical_and(same_chunk, ti[None, :] >= ti[:, None]).astype(BF16)
    router_wt = jnp.pad(router_w.T, ((0, LANES - N_EXPERTS), (0, 0)))
    router_b = router_bias.reshape(N_EXPERTS, 1)
    row1 = lambda a: a.reshape(1, -1)

    for l in range(depth):
        ctx_out = l < depth - 1
        mods = mods_all[l]
        rows = n_lat + n_ctx if ctx_out else n_lat

        w_in_p, mu = _pack_in_proj(w_in[l], shift_mu[l])
        wq1, wq2, wuk, wuv = _pack_mla(w_uq[l], w_ukv[l])
        q, k, v, p_rw = _in_proj(x_lat, x_ctx, mods, row1(mix_pre_g[l]), w_in_p, row1(q_norm_g[l]),
                                 row1(kv_norm_g[l]), wq1, wq2, wuk, wuv, cq_t, sq_t, ck_t, n_lat=n_lat, n_ctx=n_ctx,
                                 seq=seq, ctx_len=ctx_len)
        att = _attention(q, k, v, n_batch=n_batch, seq=seq, ctx_len=ctx_len, latent=True)
        att_c = _attention(q, k, v, n_batch=n_batch, seq=seq, ctx_len=ctx_len, latent=False) if ctx_out else None

        g2p = jnp.pad(gate_g2[l], ((0, GATE_PAD - GATE_LORA), (0, 0))).astype(BF16)
        prep = _rwkv_prep(
            p_rw, mu, _block_diag2(decay_w2[l]).astype(BF16), decay_w0[l].reshape(1, 2 * w),
            _block_diag2(iclr_a2[l]).astype(BF16), iclr_a0[l].reshape(1, 2 * w), g2p,
            row1(k_k[l]), row1(k_a[l]), row1(r_k[l]), ones_bd, tri_f, tri_r, ones_c,
            n_lat=n_lat, seq=seq, ctx_len=ctx_len)
        vv, g, bon = prep[0], prep[11], prep[12]
        yf, yb = _rwkv_scan(vv, prep[1:6], prep[6:11], n_batch=n_batch, seq=seq, ctx_len=ctx_len)

        wo = w_out[l].astype(BF16)
        x_all = _mix_out(yf, yb, att, att_c, g, bon, row1(lnx_g[l]), row1(lnx_b[l]), ones_bd, wo[:w], wo[w:], x_lat,
                         x_ctx, mods, row1(mix_post_g[l]), rows=rows, n_lat=n_lat, seq=seq)

        wgu = jnp.concatenate([
            jnp.concatenate([exp_w_gate[l], exp_w_up[l]], axis=2),
            jnp.concatenate([sh_w_gate[l], sh_w_up[l]], axis=1)[None]], axis=0).astype(BF16)
        wd = jnp.concatenate([exp_w_down[l], sh_w_down[l][None]], axis=0).astype(BF16)
        x_all = _moe(x_all, mods, row1(ffn_pre_g[l]), row1(ffn_post_g[l]), router_wt, router_b, wgu, wd,
                     rows=rows, n_lat=n_lat, seq=seq, tm=moe_tile)
        x_lat = x_ctx = x_all

    return x_all.reshape(n_batch, seq, d)
```

```python
import functools
import math

import jax
import jax.numpy as jnp
from jax import lax
from jax.experimental import pallas as pl
from jax.experimental.pallas import tpu as pltpu

F32 = jnp.float32
BF16 = jnp.bfloat16
HIGHEST = lax.Precision.HIGHEST

D_MODEL = 1024
N_MOD = 6
NORM_EPS = 1e-6
GRID_W = 64
ROPE_BASE = 10000.0

HEADS = 8
MLA_NOPE = 64
MLA_ROPE = 32
MLA_V = 64
Q_RANK = 256
KV_RANK = 128
MLA_COLS = Q_RANK + KV_RANK + MLA_ROPE

RWKV_HEAD = 64
RWKV_WIDTH = HEADS * RWKV_HEAD
DECAY_LORA = 64
ICLR_LORA = 64
GATE_LORA = 160
RWKV_COLS = 3 * RWKV_WIDTH + 2 * DECAY_LORA + 2 * ICLR_LORA + GATE_LORA
RWKV_PAD = 2048
GATE_PAD = 256
LNX_EPS = 64e-5

N_EXPERTS = 16
N_GROUPS = 4
EXPERTS_PER_GROUP = 4
EXPERT_HIDDEN = 256

LANES = 128
HEAD_SLOT = 128
IN_PACKED = Q_RANK + KV_RANK + 2 * LANES + RWKV_PAD
TOKEN_TILE = 256
ATTN_KV_BLOCK = 1024
ATTN_Q_TILE = 512
MOE_TILE = 512
MOE_BLOCK = 128
CHUNK = 64
VMEM_LIMIT = 48 * 1024 * 1024


def _dot(a, b):
    return jnp.dot(a, b, preferred_element_type=F32)


def _dot_nt(a, b):
    return lax.dot_general(a, b, (((1,), (1,)), ((), ())), preferred_element_type=F32)


def _dot_tn(a, b):
    return lax.dot_general(a, b, (((0,), (0,)), ((), ())), preferred_element_type=F32)


def _rms(x, g):
    return x * lax.rsqrt(jnp.mean(x * x, axis=-1, keepdims=True) + NORM_EPS) * g


def _seg_sum(x, ones_bd):
    hi = x.astype(BF16)
    lo = (x - hi.astype(F32)).astype(BF16)
    return _dot(hi, ones_bd) + _dot(lo, ones_bd)


def _params(sem):
    return pltpu.CompilerParams(dimension_semantics=sem, vmem_limit_bytes=VMEM_LIMIT)


def _ada_kernel(c_ref, w_ref, b_ref, o_ref):
    c = c_ref[...]
    cond = c * jax.nn.sigmoid(c)
    o_ref[0] = jnp.dot(cond, w_ref[0], precision=HIGHEST, preferred_element_type=F32) + b_ref[0]


def _ada_modulation(cond_rows, ada_w, ada_b):
    depth, d, n = ada_w.shape
    tn = 1536
    rows = cond_rows.shape[0]
    return pl.pallas_call(
        _ada_kernel,
        out_shape=jax.ShapeDtypeStruct((depth, rows, n), F32),
        grid=(depth, n // tn),
        in_specs=[
            pl.BlockSpec((rows, d), lambda l, j: (0, 0)),
            pl.BlockSpec((1, d, tn), lambda l, j: (l, 0, j)),
            pl.BlockSpec((1, 1, tn), lambda l, j: (l, 0, j)),
        ],
        out_specs=pl.BlockSpec((1, rows, tn), lambda l, j: (l, 0, j)),
        compiler_params=_params(("parallel", "parallel")),
        name="ada_modulation",
    )(cond_rows, ada_w, ada_b.reshape(depth, 1, n))


def _stream_specs(tm, lat_tiles, merged):
    off = lat_tiles if merged else 0
    return (pl.BlockSpec((tm, D_MODEL), lambda i: (jnp.minimum(i, lat_tiles - 1), 0)),
            pl.BlockSpec((tm, D_MODEL), lambda i: (jnp.maximum(i - lat_tiles, 0) + off, 0)))


def _in_proj_kernel(xl_ref, xc_ref, mod_ref, g_ref, win_ref, qng_ref, kvng_ref, wq1_ref, wq2_ref, wuk_ref, wuv_ref,
                    cq_ref, sq_ref, ck_ref, q_out, k_out, v_out, p_out, *, q_scale, lat_tiles):
    x = jnp.where(pl.program_id(0) < lat_tiles, xl_ref[...], xc_ref[...])
    shift = mod_ref[0, 0:1, :]
    scale = mod_ref[0, 1:2, :]
    h = _rms(x, g_ref[...]) * (1.0 + scale) + shift
    p = _dot(h.astype(BF16), win_ref[...])
    c_q = p[:, 0:Q_RANK]
    c_kv = p[:, Q_RANK:Q_RANK + KV_RANK]
    kr_a = p[:, Q_RANK + KV_RANK:Q_RANK + KV_RANK + LANES]
    kr_b = p[:, Q_RANK + KV_RANK + LANES:Q_RANK + KV_RANK + 2 * LANES]
    p_out[...] = p[:, Q_RANK + KV_RANK + 2 * LANES:]

    tile8 = lambda t: jnp.concatenate([t] * HEADS, axis=1)
    cqn = _rms(c_q, qng_ref[...]).astype(BF16)
    q = _dot(cqn, wq1_ref[...]) * tile8(cq_ref[...]) + _dot(cqn, wq2_ref[...]) * tile8(sq_ref[...])
    q_out[...] = (q * q_scale).astype(BF16)

    ckvn = _rms(c_kv, kvng_ref[...]).astype(BF16)
    k_rot = kr_a * ck_ref[...] + kr_b * sq_ref[...]
    k_out[...] = (_dot(ckvn, wuk_ref[...]) + tile8(k_rot)).astype(BF16)
    v_out[...] = _dot_nt(wuv_ref[...], ckvn).astype(BF16)


def _in_proj(x_lat, x_ctx, mods, g, w_in_p, qng, kvng, wq1, wq2, wuk, wuv, cq_t, sq_t, ck_t, *, n_lat, n_ctx, seq,
             ctx_len):
    nt = n_lat + n_ctx
    tm = TOKEN_TILE
    lat_tiles = n_lat // tm
    n_batch = n_lat // seq

    def mod_idx(i):
        return (jnp.where(i < lat_tiles, (i * tm) // seq, n_batch), 0, 0)

    def tab_idx(i):
        return (jnp.where(i < lat_tiles, i % (seq // tm), seq // tm + (i - lat_tiles) % (ctx_len // tm)), 0)

    const = lambda i: (0, 0)
    row = lambda i: (i, 0)
    qw = HEADS * HEAD_SLOT
    return pl.pallas_call(
        functools.partial(_in_proj_kernel, q_scale=float((MLA_NOPE + MLA_ROPE) ** -0.5 * math.log2(math.e)),
                          lat_tiles=lat_tiles),
        out_shape=(
            jax.ShapeDtypeStruct((nt, qw), BF16),
            jax.ShapeDtypeStruct((nt, qw), BF16),
            jax.ShapeDtypeStruct((HEADS * MLA_V, nt), BF16),
            jax.ShapeDtypeStruct((nt, RWKV_PAD), F32),
        ),
        grid=(nt // tm,),
        in_specs=[
            *_stream_specs(tm, lat_tiles, x_ctx is x_lat),
            pl.BlockSpec((1, N_MOD, D_MODEL), mod_idx),
            pl.BlockSpec((1, D_MODEL), const),
            pl.BlockSpec((D_MODEL, IN_PACKED), const),
            pl.BlockSpec((1, Q_RANK), const),
            pl.BlockSpec((1, KV_RANK), const),
            pl.BlockSpec((Q_RANK, qw), const),
            pl.BlockSpec((Q_RANK, qw), const),
            pl.BlockSpec((KV_RANK, qw), const),
            pl.BlockSpec((HEADS * MLA_V, KV_RANK), const),
            pl.BlockSpec((tm, HEAD_SLOT), tab_idx),
            pl.BlockSpec((tm, HEAD_SLOT), tab_idx),
            pl.BlockSpec((tm, HEAD_SLOT), tab_idx),
        ],
        out_specs=(
            pl.BlockSpec((tm, qw), row),
            pl.BlockSpec((tm, qw), row),
            pl.BlockSpec((HEADS * MLA_V, tm), lambda i: (0, i)),
            pl.BlockSpec((tm, RWKV_PAD), row),
        ),
        compiler_params=_params(("parallel",)),
        name="in_proj",
    )(x_lat, x_ctx, mods, g, w_in_p, qng, kvng, wq1, wq2, wuk, wuv, cq_t, sq_t, ck_t)


def _attn_kernel(q_ref, kc_ref, vc_ref, *rest):
    o_ref = rest[-1]
    blocks = [(kc_ref, vc_ref, 0, kc_ref.shape[0])]
    if len(rest) == 3:
        kl_ref, vl_ref = rest[0], rest[1]
        kb = min(ATTN_KV_BLOCK, kl_ref.shape[0])
        blocks += [(kl_ref, vl_ref, s0, kb) for s0 in range(0, kl_ref.shape[0], kb)]
    heads = range(2)
    hs = [slice(h * HEAD_SLOT, (h + 1) * HEAD_SLOT) for h in heads]
    vs = [slice(h * MLA_V, (h + 1) * MLA_V) for h in heads]
    q = [q_ref[:, hs[h]] for h in heads]

    def scores(j):
        k_ref, _, s0, size = blocks[j]
        return [_dot_nt(k_ref[s0:s0 + size, hs[h]], q[h]) for h in heads]

    m, den, acc = [None] * 2, [None] * 2, [None] * 2
    s_cur = scores(0)
    for j, (_, vt_ref, s0, size) in enumerate(blocks):
        s_next = scores(j + 1) if j + 1 < len(blocks) else None
        for h in heads:
            s = s_cur[h]
            m_blk = jnp.max(s, axis=0, keepdims=True)
            if j == 0:
                m[h] = m_blk
                p = jnp.exp2(s - m_blk)
                den[h] = jnp.sum(p, axis=0, keepdims=True)
                acc[h] = _dot(vt_ref[vs[h], s0:s0 + size], p.astype(BF16))
            else:
                m_new = jnp.maximum(m[h], m_blk)
                alpha = jnp.exp2(m[h] - m_new)
                p = jnp.exp2(s - m_new)
                den[h] = alpha * den[h] + jnp.sum(p, axis=0, keepdims=True)
                acc[h] = alpha * acc[h] + _dot(vt_ref[vs[h], s0:s0 + size], p.astype(BF16))
                m[h] = m_new
        s_cur = s_next
    out_t = jnp.concatenate([acc[h] / den[h] for h in heads], axis=0)
    o_ref[...] = out_t.T.astype(o_ref.dtype)


def _attention(q, k, v_t, *, n_batch, seq, ctx_len, latent):
    tq = ATTN_Q_TILE if latent else TOKEN_TILE
    seg = seq if latent else ctx_len
    assert seg % tq == 0
    q_tiles = seg // tq
    q_blk0 = 0 if latent else (n_batch * seq) // tq
    ctx_blk0 = (n_batch * seq) // ctx_len
    kv_ctx = lambda b, hp, j: (ctx_blk0 + b, hp)
    kv_lat = lambda b, hp, j: (b, hp)
    in_specs = [
        pl.BlockSpec((tq, 2 * HEAD_SLOT), lambda b, hp, j: (q_blk0 + b * q_tiles + j, hp)),
        pl.BlockSpec((ctx_len, 2 * HEAD_SLOT), kv_ctx),
        pl.BlockSpec((2 * MLA_V, ctx_len), lambda b, hp, j: (hp, ctx_blk0 + b)),
    ]
    args = [q, k, v_t]
    if latent:
        in_specs += [pl.BlockSpec((seq, 2 * HEAD_SLOT), kv_lat),
                     pl.BlockSpec((2 * MLA_V, seq), lambda b, hp, j: (hp, b))]
        args += [k, v_t]
    return pl.pallas_call(
        _attn_kernel,
        out_shape=jax.ShapeDtypeStruct((n_batch * seg, HEADS * MLA_V), BF16),
        grid=(n_batch, HEADS // 2, q_tiles),
        in_specs=in_specs,
        out_specs=pl.BlockSpec((tq, 2 * MLA_V), lambda b, hp, j: (b * q_tiles + j, hp)),
        compiler_params=_params(("parallel", "parallel", "arbitrary")),
        name="attention_lat" if latent else "attention_ctx",
    )(*args)


def _split2(x):
    hi = x.astype(BF16)
    return hi, (x - hi.astype(F32)).astype(BF16)


def _chunk_transpose(x):
    n = CHUNK
    xt = x.T
    rows = []
    for c in range(x.shape[0] // n):
        rows.append(jnp.concatenate(
            [xt[h * n:(h + 1) * n, c * n:(c + 1) * n] for h in range(x.shape[1] // n)], axis=1))
    return jnp.concatenate(rows, axis=0)


def _rwkv_prep_kernel(p_ref, hp_ref, hn_ref, mu_ref, w2_ref, w0_ref, a2_ref, a0_ref, g2_ref, kk_ref, ka_ref, rk_ref,
                      ones_ref, trif_ref, trir_ref, onesc_ref, v_out, rt0_out, kt0_out, bt0_out, kb0_out, pc0_out,
                      rt1_out, kt1_out, bt1_out, kb1_out, pc1_out, g_out, bon_out, *, lat_tiles, seq_tiles,
                      ctx_tiles):
    i = pl.program_id(0)
    tm = p_ref.shape[0]
    is_lat = i < lat_tiles
    local = jnp.where(is_lat, i % seq_tiles, (i - lat_tiles) % ctx_tiles)
    seg = jnp.where(is_lat, seq_tiles, ctx_tiles)
    keep_prev = jnp.where(local == 0, 0.0, 1.0)
    keep_next = jnp.where(local == seg - 1, 0.0, 1.0)

    p = p_ref[...]
    row = lax.broadcasted_iota(jnp.int32, (tm, 1), 0)
    prev = jnp.where(row == 0, hp_ref[7:8, :] * keep_prev, pltpu.roll(p, 1, 0))
    nxt = jnp.where(row == tm - 1, hn_ref[0:1, :] * keep_next, pltpu.roll(p, tm - 1, 0))
    ps = p + mu_ref[0:1, :] * (prev - p) + mu_ref[1:2, :] * (nxt - p)

    w = RWKV_WIDTH
    r = ps[:, 0:w]
    k = ps[:, w:2 * w]
    v = ps[:, 2 * w:3 * w]
    wl = ps[:, 3 * w:3 * w + 2 * DECAY_LORA]
    al = ps[:, 3 * w + 2 * DECAY_LORA:3 * w + 2 * DECAY_LORA + 2 * ICLR_LORA]
    gl = ps[:, 3 * w + 2 * DECAY_LORA + 2 * ICLR_LORA:3 * w + 2 * DECAY_LORA + 2 * ICLR_LORA + GATE_PAD]

    g_out[...] = _dot(jax.nn.sigmoid(gl).astype(BF16), g2_ref[...])
    z = w0_ref[...] + _dot(jnp.tanh(wl).astype(BF16), w2_ref[...])
    logw = (-math.exp(-0.5)) * jax.nn.sigmoid(z)
    a = jax.nn.sigmoid(a0_ref[...] + _dot(al.astype(BF16), a2_ref[...]))

    ones_bd = ones_ref[...]
    kk = k * kk_ref[...]
    kk = kk * lax.rsqrt(_seg_sum(kk * kk, ones_bd) + 1e-12)
    ka = ka_ref[...]
    v_out[...] = v.astype(BF16)

    ke_sum = None
    for d, tri_ref, (rt_out, kt_out, bt_out, kb_out, pc_out) in (
            (0, trif_ref, (rt0_out, kt0_out, bt0_out, kb0_out, pc0_out)),
            (1, trir_ref, (rt1_out, kt1_out, bt1_out, kb1_out, pc1_out))):
        lw = logw[:, d * w:(d + 1) * w]
        a_d = a[:, d * w:(d + 1) * w]
        lw_hi, lw_lo = _split2(lw)
        cum = _dot(tri_ref[...], lw_hi) + _dot(tri_ref[...], lw_lo)
        tot = _dot(onesc_ref[...], lw_hi) + _dot(onesc_ref[...], lw_lo)
        e_neg = jnp.exp(-cum)
        ke = k * (1.0 + (a_d - 1.0) * ka)
        ke_sum = ke if ke_sum is None else ke_sum + ke
        rt_out[...] = (r * jnp.exp(cum)).astype(BF16)
        kt_out[...] = (kk * jnp.exp(cum - lw)).astype(BF16)
        bt_out[...] = _chunk_transpose(a_d * kk * e_neg).astype(BF16)
        kb_out[...] = _chunk_transpose(ke * e_neg).astype(BF16)
        pc_out[...] = _chunk_transpose(jnp.exp(tot))
    bon_out[...] = _seg_sum(r * ke_sum * rk_ref[...], ones_bd) * v


def _rwkv_prep(p_rw, mu, w2bd, w0, a2bd, a0, g2p, k_k, k_a, r_k, ones_bd, tri_f, tri_r, ones_c, *, n_lat, seq,
               ctx_len):
    nt = p_rw.shape[0]
    tm = TOKEN_TILE
    n_tiles = nt // tm
    halo = 8
    blocks8 = nt // halo
    const = lambda i: (0, 0)
    row = lambda i: (i, 0)
    w = RWKV_WIDTH
    o16 = jax.ShapeDtypeStruct((nt, w), BF16)
    o32 = jax.ShapeDtypeStruct((nt, w), F32)
    return pl.pallas_call(
        functools.partial(_rwkv_prep_kernel, lat_tiles=n_lat // tm, seq_tiles=seq // tm, ctx_tiles=ctx_len // tm),
        out_shape=(o16, o16, o16, o16, o16, o32, o16, o16, o16, o16, o32, o32, o32),
        grid=(n_tiles,),
        in_specs=[
            pl.BlockSpec((tm, RWKV_PAD), row),
            pl.BlockSpec((halo, RWKV_PAD), lambda i: (jnp.maximum(i * (tm // halo) - 1, 0), 0)),
            pl.BlockSpec((halo, RWKV_PAD), lambda i: (jnp.minimum((i + 1) * (tm // halo), blocks8 - 1), 0)),
            pl.BlockSpec((2, RWKV_PAD), const),
            pl.BlockSpec((2 * DECAY_LORA, 2 * w), const),
            pl.BlockSpec((1, 2 * w), const),
            pl.BlockSpec((2 * ICLR_LORA, 2 * w), const),
            pl.BlockSpec((1, 2 * w), const),
            pl.BlockSpec((GATE_PAD, w), const),
            pl.BlockSpec((1, w), const),
            pl.BlockSpec((1, w), const),
            pl.BlockSpec((1, w), const),
            pl.BlockSpec((w, w), const),
            pl.BlockSpec((tm, tm), const),
            pl.BlockSpec((tm, tm), const),
            pl.BlockSpec((tm, tm), const),
        ],
        out_specs=(pl.BlockSpec((tm, w), row),) * 13,
        compiler_params=_params(("parallel",)),
        name="rwkv_prep",
    )(p_rw, p_rw, p_rw, mu, w2bd, w0, a2bd, a0, g2p, k_k, k_a, r_k, ones_bd, tri_f, tri_r, ones_c)


QUAD = 2
QUAD_W = QUAD * RWKV_HEAD
SCAN_CHUNKS = 4


def _scan_kernel(v0_ref, rt0_ref, kt0_ref, bt0_ref, kb0_ref, pc0_ref, v1_ref, rt1_ref, kt1_ref, bt1_ref, kb1_ref,
                 pc1_ref, y0_ref, y1_ref, h_ref):
    @pl.when(pl.program_id(1) == 0)
    def _():
        h_ref[...] = jnp.zeros_like(h_ref)

    n, qw = CHUNK, QUAD_W
    t_i = lax.broadcasted_iota(jnp.int32, (n, qw), 0)
    s_i = lax.broadcasted_iota(jnp.int32, (n, qw), 1) % n
    bd_mask = (lax.broadcasted_iota(jnp.int32, (qw, qw), 0) // n) == (lax.broadcasted_iota(jnp.int32, (qw, qw), 1) // n)
    eye = jnp.where(t_i == s_i, 1.0, 0.0)
    same16 = (t_i // 16) == (s_i // 16)
    same32 = (t_i // 32) == (s_i // 32)
    sib32 = jnp.logical_and(same32, jnp.logical_not(same16))
    sib64 = jnp.logical_not(same32)
    incl = (s_i <= t_i, s_i >= t_i)
    strict = (s_i < t_i, s_i > t_i)

    groups = [(d, q, j) for j in range(SCAN_CHUNKS) for d in (0, 1) for q in range(HEADS // QUAD)]
    dirs = [d for d, _, _ in groups]
    ng = range(len(groups))
    row0 = lambda d, j: (SCAN_CHUNKS - 1 - j) * n if d else j * n
    refs = ((v0_ref, rt0_ref, kt0_ref, bt0_ref, kb0_ref, pc0_ref), (v1_ref, rt1_ref, kt1_ref, bt1_ref, kb1_ref, pc1_ref))
    ld = lambda k: [refs[d][k][row0(d, j):row0(d, j) + n, q * qw:(q + 1) * qw] for d, q, j in groups]
    v, rt, kt, bt_t, kb_t, pc = (ld(k) for k in range(6))

    def bd(x):
        return jnp.where(bd_mask, jnp.concatenate([x.astype(BF16)] * QUAD, axis=0), jnp.zeros((), BF16))

    mm = lambda x, w: _dot(x.astype(BF16), w)
    rows2 = lambda x, y: jnp.concatenate([x.astype(BF16), y.astype(BF16)], axis=0)

    bd2 = lambda x, y: jnp.concatenate([bd(x), bd(y)], axis=1)
    gram = [_dot(rows2(kt[g], rt[g]), bd2(bt_t[g], kb_t[g])) for g in ng]
    a_m = [jnp.where(strict[dirs[g]], gram[g][0:n, 0:qw], 0.0) for g in ng]
    c_m = [jnp.where(incl[dirs[g]], gram[g][n:2 * n, 0:qw], 0.0) for g in ng]
    b_m = [jnp.where(strict[dirs[g]], gram[g][0:n, qw:2 * qw], 0.0) for g in ng]
    e_m = [jnp.where(incl[dirs[g]], gram[g][n:2 * n, qw:2 * qw], 0.0) for g in ng]

    dd = [jnp.where(same16, a_m[g], 0.0) for g in ng]
    pw = [mm(dd[g], bd(dd[g])) for g in ng]
    t = [eye - dd[g] for g in ng]
    for _ in range(2):
        both = [_dot(rows2(t[g], pw[g]), bd(pw[g])) for g in ng]
        t = [t[g] + both[g][0:n] for g in ng]
        pw = [both[g][n:2 * n] for g in ng]
    t = [t[g] + mm(t[g], bd(pw[g])) for g in ng]
    for sib in (sib32, sib64):
        ta = [mm(t[g], bd(jnp.where(sib, a_m[g], 0.0))) for g in ng]
        t = [t[g] - mm(ta[g], bd(t[g])) for g in ng]

    bev = [_dot(jnp.concatenate([rows2(b_m[g], e_m[g]), kb_t[g]], axis=0), bd(v[g])) for g in ng]
    kv = [bev[g][2 * n:3 * n] for g in ng]
    mw = [mm(t[g], bd2(kt[g], bev[g][0:n])) for g in ng]
    m1 = [mw[g][:, 0:qw] for g in ng]
    w2 = [mw[g][:, qw:2 * qw] for g in ng]
    cq = [mm(c_m[g], bd2(m1[g], w2[g])) for g in ng]
    qh = [rt[g].astype(F32) - cq[g][:, 0:qw] for g in ng]
    y0 = [bev[g][n:2 * n] - cq[g][:, qw:2 * qw] for g in ng]

    chains = len(groups) // SCAN_CHUNKS
    h = [h_ref[s] for s in range(chains)]
    for g, (d, q, j) in enumerate(groups):
        s = g % chains
        uy = _dot(rows2(m1[g], qh[g]), bd(h[s]))
        (y0_ref, y1_ref)[d][row0(d, j):row0(d, j) + n, q * qw:(q + 1) * qw] = uy[n:2 * n] + y0[g]
        h[s] = pc[g] * (h[s] - _dot(bt_t[g], bd(uy[0:n] + w2[g])) + kv[g])
    for s in range(chains):
        h_ref[s] = h[s]


def _rwkv_scan(v, ops0, ops1, *, n_batch, seq, ctx_len):
    nt = v.shape[0]
    n = CHUNK * SCAN_CHUNKS
    assert seq % n == 0 and ctx_len % n == 0
    lat_c = seq // n
    ctx_c = ctx_len // n
    ctx0 = n_batch * lat_c

    def idx_f(bi, c):
        return (jnp.where(c < ctx_c, ctx0 + bi * ctx_c + c, bi * lat_c + (c - ctx_c)), 0)

    def idx_r(bi, c):
        return (jnp.where(c < ctx_c, ctx0 + bi * ctx_c + (ctx_c - 1 - c), bi * lat_c + (lat_c - 1 - (c - ctx_c))), 0)

    spec_f = pl.BlockSpec((n, RWKV_WIDTH), idx_f)
    spec_r = pl.BlockSpec((n, RWKV_WIDTH), idx_r)
    out = jax.ShapeDtypeStruct((nt, RWKV_WIDTH), F32)
    return pl.pallas_call(
        _scan_kernel,
        out_shape=(out, out),
        grid=(n_batch, ctx_c + lat_c),
        in_specs=[spec_f] * 6 + [spec_r] * 6,
        out_specs=(spec_f, spec_r),
        scratch_shapes=[pltpu.VMEM((2 * HEADS // QUAD, CHUNK, QUAD_W), F32)],
        compiler_params=_params(("parallel", "arbitrary")),
        name="rwkv_scan",
    )(v, *ops0, v, *ops1)


def _mix_out_kernel(*refs, lat_tiles):
    (yf_ref, yb_ref, g_ref, bon_ref, lng_ref, lnb_ref, ones_ref, woa_ref, wor_ref, xl_ref, xc_ref, mod_ref, pg_ref,
     att_ref) = refs[:14]
    o_ref = refs[-1]
    is_lat = pl.program_id(0) < lat_tiles
    ones_bd = ones_ref[...]
    y = yf_ref[...] + yb_ref[...]
    inv_n = 1.0 / RWKV_HEAD
    mu = _seg_sum(y, ones_bd) * inv_n
    yc = y - mu
    var = _seg_sum(yc * yc, ones_bd) * inv_n
    yn = yc * lax.rsqrt(var + LNX_EPS) * lng_ref[...] + lnb_ref[...]
    rw = (yn + bon_ref[...]) * g_ref[...]
    att = att_ref[...]
    if len(refs) == 16:
        att = jnp.where(is_lat, att, refs[14][...])
    mix = _dot(att, woa_ref[...]) + _dot(rw.astype(BF16), wor_ref[...])
    x = jnp.where(is_lat, xl_ref[...], xc_ref[...])
    o_ref[...] = x + mod_ref[0, 2:3, :] * _rms(mix, pg_ref[...])


def _mix_out(yf, yb, att_lat, att_ctx, g, bon, lnx_g, lnx_b, ones_bd, wo_att, wo_rw, x_lat, x_ctx, mods, post_g, *,
             rows, n_lat, seq):
    tm = TOKEN_TILE
    lat_tiles = n_lat // tm
    n_batch = n_lat // seq
    w = RWKV_WIDTH

    def mod_idx(i):
        return (jnp.where(i < lat_tiles, (i * tm) // seq, n_batch), 0, 0)

    const = lambda i: (0, 0)
    row = lambda i: (i, 0)
    in_specs = [
        pl.BlockSpec((tm, w), row),
        pl.BlockSpec((tm, w), row),
        pl.BlockSpec((tm, w), row),
        pl.BlockSpec((tm, w), row),
        pl.BlockSpec((1, w), const),
        pl.BlockSpec((1, w), const),
        pl.BlockSpec((w, w), const),
        pl.BlockSpec((w, D_MODEL), const),
        pl.BlockSpec((w, D_MODEL), const),
        *_stream_specs(tm, lat_tiles, x_ctx is x_lat),
        pl.BlockSpec((1, N_MOD, D_MODEL), mod_idx),
        pl.BlockSpec((1, D_MODEL), const),
        pl.BlockSpec((tm, w), lambda i: (jnp.minimum(i, lat_tiles - 1), 0)),
    ]
    args = [yf, yb, g, bon, lnx_g, lnx_b, ones_bd, wo_att, wo_rw, x_lat, x_ctx, mods, post_g, att_lat]
    if att_ctx is not None:
        in_specs.append(pl.BlockSpec((tm, w), lambda i: (jnp.maximum(i - lat_tiles, 0), 0)))
        args.append(att_ctx)
    return pl.pallas_call(
        functools.partial(_mix_out_kernel, lat_tiles=lat_tiles),
        out_shape=jax.ShapeDtypeStruct((rows, D_MODEL), F32),
        grid=(rows // tm,),
        in_specs=in_specs,
        out_specs=pl.BlockSpec((tm, D_MODEL), row),
        compiler_params=_params(("parallel",)),
        name="mix_out",
    )(*args)


def _route(logits_t, bias_col):
    scores = jax.nn.sigmoid(logits_t[0:N_EXPERTS, :])
    biased = scores + bias_col
    s_rows = [scores[e:e + 1, :] for e in range(N_EXPERTS)]
    b_rows = [biased[e:e + 1, :] for e in range(N_EXPERTS)]
    npg = EXPERTS_PER_GROUP
    group_scores = []
    for gi in range(N_GROUPS):
        bg = b_rows[gi * npg:(gi + 1) * npg]
        best_pair = None
        for i in range(npg):
            for j in range(i + 1, npg):
                pair = bg[i] + bg[j]
                best_pair = pair if best_pair is None else jnp.maximum(best_pair, pair)
        group_scores.append(best_pair)
    best = group_scores[0]
    best_idx = jnp.zeros(best.shape, jnp.int32)
    for gi in range(1, N_GROUPS):
        upd = group_scores[gi] > best
        best = jnp.where(upd, group_scores[gi], best)
        best_idx = jnp.where(upd, gi, best_idx)
    pick = lambda rows, j: functools.reduce(
        lambda acc, gi: jnp.where(best_idx == gi, rows[gi * npg + j], acc), range(1, N_GROUPS), rows[j])
    bb = [pick(b_rows, j) for j in range(npg)]
    ss = [pick(s_rows, j) for j in range(npg)]
    weights = []
    for j in range(npg):
        rank = jnp.zeros(best.shape, jnp.int32)
        for i in range(npg):
            if i == j:
                continue
            beats = (bb[i] > bb[j]) | ((bb[i] == bb[j]) & (i < j)) if i < j else (bb[i] > bb[j])
            rank = rank + beats.astype(jnp.int32)
        weights.append(jnp.where(rank < 2, ss[j], 0.0))
    den = weights[0] + weights[1] + weights[2] + weights[3]
    return [wj / den for wj in weights], best_idx


def _swiglu(h_b, wgu, wd, gate=None):
    gu = _dot(h_b, wgu)
    g_part = gu[:, 0:EXPERT_HIDDEN]
    act = g_part * jax.nn.sigmoid(g_part) * gu[:, EXPERT_HIDDEN:]
    if gate is not None:
        act = act * gate
    return _dot(act.astype(BF16), wd)


def _moe_kernel(x_ref, mod_ref, pre_ref, post_ref, rw_ref, rb_ref, before_ref, wgu_ref, wd_ref, o_ref, sorted_scr):
    tm = x_ref.shape[0]
    blk = MOE_BLOCK
    n_blocks = sorted_scr.shape[0] // blk
    x = x_ref[...]
    h = _rms(x, pre_ref[...]) * (1.0 + mod_ref[0, 4:5, :]) + mod_ref[0, 3:4, :]
    h_hi = h.astype(BF16)
    h_lo = (h - h_hi.astype(F32)).astype(BF16)
    rw = rw_ref[...]
    rw_hi = rw.astype(BF16)
    rw_lo = (rw - rw_hi.astype(F32)).astype(BF16)
    logits_t = _dot_nt(rw_hi, h_hi) + _dot_nt(rw_lo, h_hi) + _dot_nt(rw_hi, h_lo)
    gates, group = _route(logits_t, rb_ref[...])

    sub8 = lax.broadcasted_iota(jnp.int32, (8, tm), 0)
    onehot = jnp.where(sub8 == group, 1.0, 0.0)
    before = _dot(onehot.astype(BF16), before_ref[...])
    counts = [jnp.sum(onehot[g:g + 1, :]).astype(jnp.int32) for g in range(N_GROUPS)]
    ends, acc = [], jnp.int32(0)
    for g in range(N_GROUPS):
        acc = acc + ((counts[g] + (blk - 1)) // blk) * blk
        ends.append(acc)
    starts = [jnp.int32(0)] + ends[:-1]
    pos = sum(onehot[g:g + 1, :] * (before[g:g + 1, :] + starts[g].astype(F32)) for g in range(N_GROUPS))

    sub = lax.broadcasted_iota(jnp.int32, (LANES, tm), 0)
    stack = jnp.where(sub == EXPERTS_PER_GROUP, pos, 0.0)
    for j in range(EXPERTS_PER_GROUP):
        stack = jnp.where(sub == j, gates[j], stack)
    cols = stack.T
    g_hi = cols.astype(BF16)
    g_lo = (cols - g_hi.astype(F32)).astype(BF16)
    pos_row = pos.astype(jnp.int32)
    pos_col = cols[:, EXPERTS_PER_GROUP:EXPERTS_PER_GROUP + 1].astype(jnp.int32)

    f_shared = _swiglu(h_hi, wgu_ref[N_EXPERTS], wd_ref[N_EXPERTS])

    for s in range(n_blocks):
        r0 = s * blk
        rows = pl.ds(r0, blk)
        grp = sum((r0 >= ends[g]).astype(jnp.int32) for g in range(N_GROUPS - 1))

        @pl.when(r0 < ends[-1])
        def _():
            r_i = lax.broadcasted_iota(jnp.int32, (blk, tm), 0) + r0
            take = jnp.where(r_i == pos_row, 1.0, 0.0).astype(BF16)
            h_s = _dot(take, h_hi).astype(BF16)
            g_s = _dot(take, g_hi) + _dot(take, g_lo)
            out = None
            for j in range(EXPERTS_PER_GROUP):
                e = grp * EXPERTS_PER_GROUP + j
                y = _swiglu(h_s, wgu_ref[e], wd_ref[e], g_s[:, j:j + 1])
                out = y if out is None else out + y
            sorted_scr[rows, :] = out.astype(BF16)

        @pl.when(r0 >= ends[-1])
        def _():
            sorted_scr[rows, :] = jnp.zeros((blk, D_MODEL), BF16)

    c_i = lax.broadcasted_iota(jnp.int32, (tm, n_blocks * blk), 1)
    put = jnp.where(c_i == pos_col, 1.0, 0.0).astype(BF16)
    f = f_shared + _dot(put, sorted_scr[...])
    o_ref[...] = x + mod_ref[0, 5:6, :] * _rms(f, post_ref[...])


def _moe(x_all, mods, pre_g, post_g, router_wt, router_b, before_m, wgu, wd, *, rows, n_lat, seq):
    tm = MOE_TILE
    lat_tiles = n_lat // tm
    n_batch = n_lat // seq
    n_e = wgu.shape[0]

    def mod_idx(i):
        return (jnp.where(i < lat_tiles, (i * tm) // seq, n_batch), 0, 0)

    const = lambda i: (0, 0)
    const3 = lambda i: (0, 0, 0)
    row = lambda i: (i, 0)
    resident = pl.Buffered(1)
    return pl.pallas_call(
        _moe_kernel,
        out_shape=jax.ShapeDtypeStruct((rows, D_MODEL), F32),
        grid=(rows // tm,),
        in_specs=[
            pl.BlockSpec((tm, D_MODEL), row),
            pl.BlockSpec((1, N_MOD, D_MODEL), mod_idx),
            pl.BlockSpec((1, D_MODEL), const),
            pl.BlockSpec((1, D_MODEL), const),
            pl.BlockSpec((LANES, D_MODEL), const),
            pl.BlockSpec((N_EXPERTS, 1), const),
            pl.BlockSpec((tm, tm), const),
            pl.BlockSpec((n_e, D_MODEL, 2 * EXPERT_HIDDEN), const3, pipeline_mode=resident),
            pl.BlockSpec((n_e, EXPERT_HIDDEN, D_MODEL), const3, pipeline_mode=resident),
        ],
        out_specs=pl.BlockSpec((tm, D_MODEL), row),
        scratch_shapes=[pltpu.VMEM((tm + N_GROUPS * MOE_BLOCK, D_MODEL), BF16)],
        compiler_params=_params(("parallel",)),
        name="moe",
    )(x_all, mods, pre_g, post_g, router_wt, router_b, before_m, wgu, wd)


def _pack_in_proj(w_in, shift_mu):
    d = w_in.shape[0]
    z = lambda n: jnp.zeros((d, n), w_in.dtype)
    kr = w_in[:, Q_RANK + KV_RANK:MLA_COLS]
    half = MLA_ROPE // 2
    kr_sw = jnp.concatenate([kr[:, half:], kr[:, :half]], axis=1)
    pad = LANES - MLA_NOPE - MLA_ROPE
    packed = jnp.concatenate([
        w_in[:, :Q_RANK + KV_RANK],
        z(MLA_NOPE), kr, z(pad),
        z(MLA_NOPE), kr_sw, z(pad),
        w_in[:, MLA_COLS:], z(RWKV_PAD - RWKV_COLS),
    ], axis=1)
    mu = jnp.pad(shift_mu, ((0, 0), (0, RWKV_PAD - RWKV_COLS)))
    return packed.astype(BF16), mu


def _pack_mla(w_uq, w_ukv):
    half = MLA_ROPE // 2
    dq = MLA_NOPE + MLA_ROPE
    q3 = w_uq.reshape(Q_RANK, HEADS, dq)
    zq = lambda n: jnp.zeros((Q_RANK, HEADS, n), w_uq.dtype)
    pad = HEAD_SLOT - dq
    wq1 = jnp.concatenate([q3, zq(pad)], axis=2).reshape(Q_RANK, HEADS * HEAD_SLOT)
    wq2 = jnp.concatenate([zq(MLA_NOPE), q3[:, :, MLA_NOPE + half:], q3[:, :, MLA_NOPE:MLA_NOPE + half], zq(pad)],
                          axis=2).reshape(Q_RANK, HEADS * HEAD_SLOT)
    kv3 = w_ukv.reshape(KV_RANK, HEADS, MLA_NOPE + MLA_V)
    wuk = jnp.concatenate([kv3[:, :, :MLA_NOPE], jnp.zeros((KV_RANK, HEADS, HEAD_SLOT - MLA_NOPE), w_ukv.dtype)],
                          axis=2).reshape(KV_RANK, HEADS * HEAD_SLOT)
    wuv = kv3[:, :, MLA_NOPE:].reshape(KV_RANK, HEADS * MLA_V).T
    return wq1.astype(BF16), wq2.astype(BF16), wuk.astype(BF16), wuv.astype(BF16)


def _block_diag2(m):
    r, c = m.shape[1], m.shape[2]
    z = jnp.zeros((r, c), m.dtype)
    return jnp.concatenate([jnp.concatenate([m[0], z], axis=1), jnp.concatenate([z, m[1]], axis=1)], axis=0)


def _rope_tables(seq, ctx_len):
    axis_dim = MLA_ROPE // 2
    t = jnp.arange(seq, dtype=jnp.int32)
    row = (t // GRID_W).astype(F32)
    col = (t % GRID_W).astype(F32)
    inv_freq = ROPE_BASE ** (-jnp.arange(0, axis_dim, 2, dtype=F32) / axis_dim)
    ang = jnp.concatenate([row[:, None] * inv_freq, col[:, None] * inv_freq], axis=-1)
    cos = jnp.concatenate([jnp.cos(ang), jnp.ones((ctx_len, axis_dim), F32)], axis=0)
    sin = jnp.concatenate([jnp.sin(ang), jnp.zeros((ctx_len, axis_dim), F32)], axis=0)
    n = seq + ctx_len
    ones = jnp.ones((n, MLA_NOPE), F32)
    z_nope = jnp.zeros((n, MLA_NOPE), F32)
    z_pad = jnp.zeros((n, HEAD_SLOT - MLA_NOPE - MLA_ROPE), F32)
    cq = jnp.concatenate([ones, cos, cos, z_pad], axis=1)
    sq = jnp.concatenate([z_nope, -sin, sin, z_pad], axis=1)
    ck = jnp.concatenate([z_nope, cos, cos, z_pad], axis=1)
    return cq, sq, ck


def kernel(x, c, ctx, c_ctx, ada_w, ada_b, mix_pre_g, mix_post_g, ffn_pre_g, ffn_post_g, w_in, q_norm_g, kv_norm_g,
           w_uq, w_ukv, shift_mu, decay_w0, decay_w2, iclr_a0, iclr_a2, gate_g2, k_k, k_a, r_k, lnx_g, lnx_b, w_out,
           router_w, router_bias, exp_w_gate, exp_w_up, exp_w_down, sh_w_gate, sh_w_up, sh_w_down):
    n_batch, seq, d = x.shape
    ctx_len = ctx.shape[1]
    depth = ada_w.shape[0]
    assert d == D_MODEL and seq % TOKEN_TILE == 0 and ctx_len % TOKEN_TILE == 0 and seq % ctx_len == 0
    assert seq % GRID_W == 0 and n_batch + 1 <= 8
    n_lat = n_batch * seq
    n_ctx = n_batch * ctx_len
    assert seq % MOE_TILE == 0 and n_ctx % MOE_TILE == 0

    x_lat, x_ctx = x.reshape(n_lat, d), ctx.reshape(n_ctx, d)
    cond_rows = jnp.concatenate([c, c_ctx[None], jnp.zeros((8 - n_batch - 1, d), F32)], axis=0)
    mods_all = _ada_modulation(cond_rows, ada_w, ada_b).reshape(depth, 8, N_MOD, d)

    cq_t, sq_t, ck_t = _rope_tables(seq, ctx_len)
    w = RWKV_WIDTH
    ones_bd = (jnp.arange(w)[:, None] // RWKV_HEAD == jnp.arange(w)[None, :] // RWKV_HEAD).astype(BF16)
    ti = jnp.arange(TOKEN_TILE)
    same_chunk = (ti[:, None] // CHUNK) == (ti[None, :] // CHUNK)
    ones_c = same_chunk.astype(BF16)
    tri_f = jnp.logical_and(same_chunk, ti[None, :] <= ti[:, None]).astype(BF16)
    tri_r = jnp.logical_and(same_chunk, ti[None, :] >= ti[:, None]).astype(BF16)
    router_wt = jnp.pad(router_w.T, ((0, LANES - N_EXPERTS), (0, 0)))
    router_b = router_bias.reshape(N_EXPERTS, 1)
    tj = jnp.arange(MOE_TILE)
    before_m = (tj[:, None] < tj[None, :]).astype(BF16)
    row1 = lambda a: a.reshape(1, -1)

    for l in range(depth):
        ctx_out = l < depth - 1
        mods = mods_all[l]
        rows = n_lat + n_ctx if ctx_out else n_lat

        w_in_p, mu = _pack_in_proj(w_in[l], shift_mu[l])
        wq1, wq2, wuk, wuv = _pack_mla(w_uq[l], w_ukv[l])
        q, k, v, p_rw = _in_proj(x_lat, x_ctx, mods, row1(mix_pre_g[l]), w_in_p, row1(q_norm_g[l]),
                                 row1(kv_norm_g[l]), wq1, wq2, wuk, wuv, cq_t, sq_t, ck_t, n_lat=n_lat, n_ctx=n_ctx,
                                 seq=seq, ctx_len=ctx_len)
        att = _attention(q, k, v, n_batch=n_batch, seq=seq, ctx_len=ctx_len, latent=True)
        att_c = _attention(q, k, v, n_batch=n_batch, seq=seq, ctx_len=ctx_len, latent=False) if ctx_out else None

        g2p = jnp.pad(gate_g2[l], ((0, GATE_PAD - GATE_LORA), (0, 0))).astype(BF16)
        prep = _rwkv_prep(
            p_rw, mu, _block_diag2(decay_w2[l]).astype(BF16), decay_w0[l].reshape(1, 2 * w),
            _block_diag2(iclr_a2[l]).astype(BF16), iclr_a0[l].reshape(1, 2 * w), g2p,
            row1(k_k[l]), row1(k_a[l]), row1(r_k[l]), ones_bd, tri_f, tri_r, ones_c,
            n_lat=n_lat, seq=seq, ctx_len=ctx_len)
        vv, g, bon = prep[0], prep[11], prep[12]
        yf, yb = _rwkv_scan(vv, prep[1:6], prep[6:11], n_batch=n_batch, seq=seq, ctx_len=ctx_len)

        wo = w_out[l].astype(BF16)
        x_all = _mix_out(yf, yb, att, att_c, g, bon, row1(lnx_g[l]), row1(lnx_b[l]), ones_bd, wo[:w], wo[w:], x_lat,
                         x_ctx, mods, row1(mix_post_g[l]), rows=rows, n_lat=n_lat, seq=seq)

        wgu = jnp.concatenate([
            jnp.concatenate([exp_w_gate[l], exp_w_up[l]], axis=2),
            jnp.concatenate([sh_w_gate[l], sh_w_up[l]], axis=1)[None]], axis=0).astype(BF16)
        wd = jnp.concatenate([exp_w_down[l], sh_w_down[l][None]], axis=0).astype(BF16)
        x_all = _moe(x_all, mods, row1(ffn_pre_g[l]), row1(ffn_post_g[l]), router_wt, router_b, before_m, wgu, wd,
                     rows=rows, n_lat=n_lat, seq=seq)
        x_lat = x_ctx = x_all

    return x_all.reshape(n_batch, seq, d)
```

```python
import functools
import math

import jax
import jax.numpy as jnp
from jax import lax
from jax.experimental import pallas as pl
from jax.experimental.pallas import tpu as pltpu

F32 = jnp.float32
BF16 = jnp.bfloat16
HIGHEST = lax.Precision.HIGHEST

D_MODEL = 1024
N_MOD = 6
NORM_EPS = 1e-6
GRID_W = 64
ROPE_BASE = 10000.0

HEADS = 8
MLA_NOPE = 64
MLA_ROPE = 32
MLA_V = 64
Q_RANK = 256
KV_RANK = 128
MLA_COLS = Q_RANK + KV_RANK + MLA_ROPE

RWKV_HEAD = 64
RWKV_WIDTH = HEADS * RWKV_HEAD
DECAY_LORA = 64
ICLR_LORA = 64
GATE_LORA = 160
RWKV_COLS = 3 * RWKV_WIDTH + 2 * DECAY_LORA + 2 * ICLR_LORA + GATE_LORA
RWKV_PAD = 2048
GATE_PAD = 256
LNX_EPS = 64e-5

N_EXPERTS = 16
N_GROUPS = 4
EXPERTS_PER_GROUP = 4
EXPERT_HIDDEN = 256

LANES = 128
HEAD_SLOT = 128
IN_PACKED = Q_RANK + KV_RANK + 2 * LANES + RWKV_PAD
TOKEN_TILE = 256
ATTN_KV_BLOCK = 1024
ATTN_Q_TILE = 512
MOE_TILE = 512
MOE_BLOCK = 160
MOE_SORTED_ROWS = -(-(MOE_TILE + N_GROUPS * (MOE_BLOCK - 1)) // MOE_BLOCK) * MOE_BLOCK
MOE_MAIN_ROWS = -(-(N_GROUPS * MOE_BLOCK) // 256) * 256
CHUNK = 64
VMEM_LIMIT = 48 * 1024 * 1024


def _dot(a, b):
    return jnp.dot(a, b, preferred_element_type=F32)


def _dot_nt(a, b):
    return lax.dot_general(a, b, (((1,), (1,)), ((), ())), preferred_element_type=F32)


def _dot_tn(a, b):
    return lax.dot_general(a, b, (((0,), (0,)), ((), ())), preferred_element_type=F32)


def _rms(x, g):
    return x * lax.rsqrt(jnp.mean(x * x, axis=-1, keepdims=True) + NORM_EPS) * g


def _seg_sum(x, ones_bd):
    hi = x.astype(BF16)
    lo = (x - hi.astype(F32)).astype(BF16)
    return _dot(hi, ones_bd) + _dot(lo, ones_bd)


def _params(sem):
    return pltpu.CompilerParams(dimension_semantics=sem, vmem_limit_bytes=VMEM_LIMIT)


def _ada_kernel(c_ref, w_ref, b_ref, o_ref):
    c = c_ref[...]
    cond = c * jax.nn.sigmoid(c)
    o_ref[0] = jnp.dot(cond, w_ref[0], precision=HIGHEST, preferred_element_type=F32) + b_ref[0]


def _ada_modulation(cond_rows, ada_w, ada_b):
    depth, d, n = ada_w.shape
    tn = 1536
    rows = cond_rows.shape[0]
    return pl.pallas_call(
        _ada_kernel,
        out_shape=jax.ShapeDtypeStruct((depth, rows, n), F32),
        grid=(depth, n // tn),
        in_specs=[
            pl.BlockSpec((rows, d), lambda l, j: (0, 0)),
            pl.BlockSpec((1, d, tn), lambda l, j: (l, 0, j)),
            pl.BlockSpec((1, 1, tn), lambda l, j: (l, 0, j)),
        ],
        out_specs=pl.BlockSpec((1, rows, tn), lambda l, j: (l, 0, j)),
        compiler_params=_params(("parallel", "parallel")),
        name="ada_modulation",
    )(cond_rows, ada_w, ada_b.reshape(depth, 1, n))


def _stream_specs(tm, lat_tiles, merged):
    off = lat_tiles if merged else 0
    return (pl.BlockSpec((tm, D_MODEL), lambda i: (jnp.minimum(i, lat_tiles - 1), 0)),
            pl.BlockSpec((tm, D_MODEL), lambda i: (jnp.maximum(i - lat_tiles, 0) + off, 0)))


def _in_proj_kernel(xl_ref, xc_ref, mod_ref, g_ref, win_ref, qng_ref, kvng_ref, wq1_ref, wq2_ref, wuk_ref, wuv_ref,
                    cq_ref, sq_ref, ck_ref, q_out, k_out, v_out, p_out, *, q_scale, lat_tiles):
    x = jnp.where(pl.program_id(0) < lat_tiles, xl_ref[...], xc_ref[...])
    shift = mod_ref[0, 0:1, :]
    scale = mod_ref[0, 1:2, :]
    h = _rms(x, g_ref[...]) * (1.0 + scale) + shift
    p = _dot(h.astype(BF16), win_ref[...])
    c_q = p[:, 0:Q_RANK]
    c_kv = p[:, Q_RANK:Q_RANK + KV_RANK]
    kr_a = p[:, Q_RANK + KV_RANK:Q_RANK + KV_RANK + LANES]
    kr_b = p[:, Q_RANK + KV_RANK + LANES:Q_RANK + KV_RANK + 2 * LANES]
    p_out[...] = p[:, Q_RANK + KV_RANK + 2 * LANES:]

    tile8 = lambda t: jnp.concatenate([t] * HEADS, axis=1)
    cqn = _rms(c_q, qng_ref[...]).astype(BF16)
    q = _dot(cqn, wq1_ref[...]) * tile8(cq_ref[...]) + _dot(cqn, wq2_ref[...]) * tile8(sq_ref[...])
    q_out[...] = (q * q_scale).astype(BF16)

    ckvn = _rms(c_kv, kvng_ref[...]).astype(BF16)
    k_rot = kr_a * ck_ref[...] + kr_b * sq_ref[...]
    k_out[...] = (_dot(ckvn, wuk_ref[...]) + tile8(k_rot)).astype(BF16)
    v_out[...] = _dot_nt(wuv_ref[...], ckvn).astype(BF16)


def _in_proj(x_lat, x_ctx, mods, g, w_in_p, qng, kvng, wq1, wq2, wuk, wuv, cq_t, sq_t, ck_t, *, n_lat, n_ctx, seq,
             ctx_len):
    nt = n_lat + n_ctx
    tm = TOKEN_TILE
    lat_tiles = n_lat // tm
    n_batch = n_lat // seq

    def mod_idx(i):
        return (jnp.where(i < lat_tiles, (i * tm) // seq, n_batch), 0, 0)

    def tab_idx(i):
        return (jnp.where(i < lat_tiles, i % (seq // tm), seq // tm + (i - lat_tiles) % (ctx_len // tm)), 0)

    const = lambda i: (0, 0)
    row = lambda i: (i, 0)
    qw = HEADS * HEAD_SLOT
    return pl.pallas_call(
        functools.partial(_in_proj_kernel, q_scale=float((MLA_NOPE + MLA_ROPE) ** -0.5 * math.log2(math.e)),
                          lat_tiles=lat_tiles),
        out_shape=(
            jax.ShapeDtypeStruct((nt, qw), BF16),
            jax.ShapeDtypeStruct((nt, qw), BF16),
            jax.ShapeDtypeStruct((HEADS * MLA_V, nt), BF16),
            jax.ShapeDtypeStruct((nt, RWKV_PAD), F32),
        ),
        grid=(nt // tm,),
        in_specs=[
            *_stream_specs(tm, lat_tiles, x_ctx is x_lat),
            pl.BlockSpec((1, N_MOD, D_MODEL), mod_idx),
            pl.BlockSpec((1, D_MODEL), const),
            pl.BlockSpec((D_MODEL, IN_PACKED), const),
            pl.BlockSpec((1, Q_RANK), const),
            pl.BlockSpec((1, KV_RANK), const),
            pl.BlockSpec((Q_RANK, qw), const),
            pl.BlockSpec((Q_RANK, qw), const),
            pl.BlockSpec((KV_RANK, qw), const),
            pl.BlockSpec((HEADS * MLA_V, KV_RANK), const),
            pl.BlockSpec((tm, HEAD_SLOT), tab_idx),
            pl.BlockSpec((tm, HEAD_SLOT), tab_idx),
            pl.BlockSpec((tm, HEAD_SLOT), tab_idx),
        ],
        out_specs=(
            pl.BlockSpec((tm, qw), row),
            pl.BlockSpec((tm, qw), row),
            pl.BlockSpec((HEADS * MLA_V, tm), lambda i: (0, i)),
            pl.BlockSpec((tm, RWKV_PAD), row),
        ),
        compiler_params=_params(("parallel",)),
        name="in_proj",
    )(x_lat, x_ctx, mods, g, w_in_p, qng, kvng, wq1, wq2, wuk, wuv, cq_t, sq_t, ck_t)


def _attn_kernel(q_ref, kc_ref, vc_ref, *rest):
    o_ref = rest[-1]
    blocks = [(kc_ref, vc_ref, 0, kc_ref.shape[0])]
    if len(rest) == 3:
        kl_ref, vl_ref = rest[0], rest[1]
        kb = min(ATTN_KV_BLOCK, kl_ref.shape[0])
        blocks += [(kl_ref, vl_ref, s0, kb) for s0 in range(0, kl_ref.shape[0], kb)]
    heads = range(2)
    hs = [slice(h * HEAD_SLOT, (h + 1) * HEAD_SLOT) for h in heads]
    vs = [slice(h * MLA_V, (h + 1) * MLA_V) for h in heads]
    q = [q_ref[:, hs[h]] for h in heads]

    def scores(j):
        k_ref, _, s0, size = blocks[j]
        return [_dot_nt(k_ref[s0:s0 + size, hs[h]], q[h]) for h in heads]

    m, den, acc = [None] * 2, [None] * 2, [None] * 2
    s_cur = scores(0)
    for j, (_, vt_ref, s0, size) in enumerate(blocks):
        s_next = scores(j + 1) if j + 1 < len(blocks) else None
        for h in heads:
            s = s_cur[h]
            m_blk = jnp.max(s, axis=0, keepdims=True)
            if j == 0:
                m[h] = m_blk
                p = jnp.exp2(s - m_blk)
                den[h] = jnp.sum(p, axis=0, keepdims=True)
                acc[h] = _dot(vt_ref[vs[h], s0:s0 + size], p.astype(BF16))
            else:
                m_new = jnp.maximum(m[h], m_blk)
                alpha = jnp.exp2(m[h] - m_new)
                p = jnp.exp2(s - m_new)
                den[h] = alpha * den[h] + jnp.sum(p, axis=0, keepdims=True)
                acc[h] = alpha * acc[h] + _dot(vt_ref[vs[h], s0:s0 + size], p.astype(BF16))
                m[h] = m_new
        s_cur = s_next
    out_t = jnp.concatenate([acc[h] / den[h] for h in heads], axis=0)
    o_ref[...] = out_t.T.astype(o_ref.dtype)


def _attention(q, k, v_t, *, n_batch, seq, ctx_len, latent):
    tq = ATTN_Q_TILE if latent else TOKEN_TILE
    seg = seq if latent else ctx_len
    assert seg % tq == 0
    q_tiles = seg // tq
    q_blk0 = 0 if latent else (n_batch * seq) // tq
    ctx_blk0 = (n_batch * seq) // ctx_len
    kv_ctx = lambda b, hp, j: (ctx_blk0 + b, hp)
    kv_lat = lambda b, hp, j: (b, hp)
    in_specs = [
        pl.BlockSpec((tq, 2 * HEAD_SLOT), lambda b, hp, j: (q_blk0 + b * q_tiles + j, hp)),
        pl.BlockSpec((ctx_len, 2 * HEAD_SLOT), kv_ctx),
        pl.BlockSpec((2 * MLA_V, ctx_len), lambda b, hp, j: (hp, ctx_blk0 + b)),
    ]
    args = [q, k, v_t]
    if latent:
        in_specs += [pl.BlockSpec((seq, 2 * HEAD_SLOT), kv_lat),
                     pl.BlockSpec((2 * MLA_V, seq), lambda b, hp, j: (hp, b))]
        args += [k, v_t]
    return pl.pallas_call(
        _attn_kernel,
        out_shape=jax.ShapeDtypeStruct((n_batch * seg, HEADS * MLA_V), BF16),
        grid=(n_batch, HEADS // 2, q_tiles),
        in_specs=in_specs,
        out_specs=pl.BlockSpec((tq, 2 * MLA_V), lambda b, hp, j: (b * q_tiles + j, hp)),
        compiler_params=_params(("parallel", "parallel", "arbitrary")),
        name="attention_lat" if latent else "attention_ctx",
    )(*args)


def _split2(x):
    hi = x.astype(BF16)
    return hi, (x - hi.astype(F32)).astype(BF16)


def _chunk_transpose(x):
    n = CHUNK
    xt = x.T
    rows = []
    for c in range(x.shape[0] // n):
        rows.append(jnp.concatenate(
            [xt[h * n:(h + 1) * n, c * n:(c + 1) * n] for h in range(x.shape[1] // n)], axis=1))
    return jnp.concatenate(rows, axis=0)


def _rwkv_prep_kernel(p_ref, hp_ref, hn_ref, mu_ref, w2_ref, w0_ref, a2_ref, a0_ref, g2_ref, kk_ref, ka_ref, rk_ref,
                      ones_ref, trif_ref, trir_ref, onesc_ref, v_out, rt0_out, kt0_out, bt0_out, kb0_out, pc0_out,
                      rt1_out, kt1_out, bt1_out, kb1_out, pc1_out, g_out, bon_out, *, lat_tiles, seq_tiles,
                      ctx_tiles):
    i = pl.program_id(0)
    tm = p_ref.shape[0]
    is_lat = i < lat_tiles
    local = jnp.where(is_lat, i % seq_tiles, (i - lat_tiles) % ctx_tiles)
    seg = jnp.where(is_lat, seq_tiles, ctx_tiles)
    keep_prev = jnp.where(local == 0, 0.0, 1.0)
    keep_next = jnp.where(local == seg - 1, 0.0, 1.0)

    p = p_ref[...]
    row = lax.broadcasted_iota(jnp.int32, (tm, 1), 0)
    prev = jnp.where(row == 0, hp_ref[7:8, :] * keep_prev, pltpu.roll(p, 1, 0))
    nxt = jnp.where(row == tm - 1, hn_ref[0:1, :] * keep_next, pltpu.roll(p, tm - 1, 0))
    ps = p + mu_ref[0:1, :] * (prev - p) + mu_ref[1:2, :] * (nxt - p)

    w = RWKV_WIDTH
    r = ps[:, 0:w]
    k = ps[:, w:2 * w]
    v = ps[:, 2 * w:3 * w]
    wl = ps[:, 3 * w:3 * w + 2 * DECAY_LORA]
    al = ps[:, 3 * w + 2 * DECAY_LORA:3 * w + 2 * DECAY_LORA + 2 * ICLR_LORA]
    gl = ps[:, 3 * w + 2 * DECAY_LORA + 2 * ICLR_LORA:3 * w + 2 * DECAY_LORA + 2 * ICLR_LORA + GATE_PAD]

    g_out[...] = _dot(jax.nn.sigmoid(gl).astype(BF16), g2_ref[...])
    z = w0_ref[...] + _dot(jnp.tanh(wl).astype(BF16), w2_ref[...])
    logw = (-math.exp(-0.5)) * jax.nn.sigmoid(z)
    a = jax.nn.sigmoid(a0_ref[...] + _dot(al.astype(BF16), a2_ref[...]))

    ones_bd = ones_ref[...]
    kk = k * kk_ref[...]
    kk = kk * lax.rsqrt(_seg_sum(kk * kk, ones_bd) + 1e-12)
    ka = ka_ref[...]
    v_out[...] = v.astype(BF16)

    ke_sum = None
    for d, tri_ref, (rt_out, kt_out, bt_out, kb_out, pc_out) in (
            (0, trif_ref, (rt0_out, kt0_out, bt0_out, kb0_out, pc0_out)),
            (1, trir_ref, (rt1_out, kt1_out, bt1_out, kb1_out, pc1_out))):
        lw = logw[:, d * w:(d + 1) * w]
        a_d = a[:, d * w:(d + 1) * w]
        lw_hi, lw_lo = _split2(lw)
        cum = _dot(tri_ref[...], lw_hi) + _dot(tri_ref[...], lw_lo)
        tot = _dot(onesc_ref[...], lw_hi) + _dot(onesc_ref[...], lw_lo)
        e_neg = jnp.exp(-cum)
        ke = k * (1.0 + (a_d - 1.0) * ka)
        ke_sum = ke if ke_sum is None else ke_sum + ke
        rt_out[...] = (r * jnp.exp(cum)).astype(BF16)
        kt_out[...] = (kk * jnp.exp(cum - lw)).astype(BF16)
        bt_out[...] = _chunk_transpose(a_d * kk * e_neg).astype(BF16)
        kb_out[...] = _chunk_transpose(ke * e_neg).astype(BF16)
        pc_out[...] = _chunk_transpose(jnp.exp(tot))
    bon_out[...] = _seg_sum(r * ke_sum * rk_ref[...], ones_bd) * v


def _rwkv_prep(p_rw, mu, w2bd, w0, a2bd, a0, g2p, k_k, k_a, r_k, ones_bd, tri_f, tri_r, ones_c, *, n_lat, seq,
               ctx_len):
    nt = p_rw.shape[0]
    tm = TOKEN_TILE
    n_tiles = nt // tm
    halo = 8
    blocks8 = nt // halo
    const = lambda i: (0, 0)
    row = lambda i: (i, 0)
    w = RWKV_WIDTH
    o16 = jax.ShapeDtypeStruct((nt, w), BF16)
    o32 = jax.ShapeDtypeStruct((nt, w), F32)
    return pl.pallas_call(
        functools.partial(_rwkv_prep_kernel, lat_tiles=n_lat // tm, seq_tiles=seq // tm, ctx_tiles=ctx_len // tm),
        out_shape=(o16, o16, o16, o16, o16, o32, o16, o16, o16, o16, o32, o32, o32),
        grid=(n_tiles,),
        in_specs=[
            pl.BlockSpec((tm, RWKV_PAD), row),
            pl.BlockSpec((halo, RWKV_PAD), lambda i: (jnp.maximum(i * (tm // halo) - 1, 0), 0)),
            pl.BlockSpec((halo, RWKV_PAD), lambda i: (jnp.minimum((i + 1) * (tm // halo), blocks8 - 1), 0)),
            pl.BlockSpec((2, RWKV_PAD), const),
            pl.BlockSpec((2 * DECAY_LORA, 2 * w), const),
            pl.BlockSpec((1, 2 * w), const),
            pl.BlockSpec((2 * ICLR_LORA, 2 * w), const),
            pl.BlockSpec((1, 2 * w), const),
            pl.BlockSpec((GATE_PAD, w), const),
            pl.BlockSpec((1, w), const),
            pl.BlockSpec((1, w), const),
            pl.BlockSpec((1, w), const),
            pl.BlockSpec((w, w), const),
            pl.BlockSpec((tm, tm), const),
            pl.BlockSpec((tm, tm), const),
            pl.BlockSpec((tm, tm), const),
        ],
        out_specs=(pl.BlockSpec((tm, w), row),) * 13,
        compiler_params=_params(("parallel",)),
        name="rwkv_prep",
    )(p_rw, p_rw, p_rw, mu, w2bd, w0, a2bd, a0, g2p, k_k, k_a, r_k, ones_bd, tri_f, tri_r, ones_c)


QUAD = 2
QUAD_W = QUAD * RWKV_HEAD
SCAN_CHUNKS = 4


def _scan_kernel(v0_ref, rt0_ref, kt0_ref, bt0_ref, kb0_ref, pc0_ref, v1_ref, rt1_ref, kt1_ref, bt1_ref, kb1_ref,
                 pc1_ref, y0_ref, y1_ref, h_ref):
    @pl.when(pl.program_id(1) == 0)
    def _():
        h_ref[...] = jnp.zeros_like(h_ref)

    n, qw = CHUNK, QUAD_W
    t_i = lax.broadcasted_iota(jnp.int32, (n, qw), 0)
    s_i = lax.broadcasted_iota(jnp.int32, (n, qw), 1) % n
    bd_mask = (lax.broadcasted_iota(jnp.int32, (qw, qw), 0) // n) == (lax.broadcasted_iota(jnp.int32, (qw, qw), 1) // n)
    eye = jnp.where(t_i == s_i, 1.0, 0.0)
    same16 = (t_i // 16) == (s_i // 16)
    same32 = (t_i // 32) == (s_i // 32)
    sib32 = jnp.logical_and(same32, jnp.logical_not(same16))
    sib64 = jnp.logical_not(same32)
    incl = (s_i <= t_i, s_i >= t_i)
    strict = (s_i < t_i, s_i > t_i)

    groups = [(d, q, j) for j in range(SCAN_CHUNKS) for d in (0, 1) for q in range(HEADS // QUAD)]
    dirs = [d for d, _, _ in groups]
    ng = range(len(groups))
    row0 = lambda d, j: (SCAN_CHUNKS - 1 - j) * n if d else j * n
    refs = ((v0_ref, rt0_ref, kt0_ref, bt0_ref, kb0_ref, pc0_ref), (v1_ref, rt1_ref, kt1_ref, bt1_ref, kb1_ref, pc1_ref))
    ld = lambda k: [refs[d][k][row0(d, j):row0(d, j) + n, q * qw:(q + 1) * qw] for d, q, j in groups]
    v, rt, kt, bt_t, kb_t, pc = (ld(k) for k in range(6))

    def bd(x):
        return jnp.where(bd_mask, jnp.concatenate([x.astype(BF16)] * QUAD, axis=0), jnp.zeros((), BF16))

    mm = lambda x, w: _dot(x.astype(BF16), w)
    rows2 = lambda x, y: jnp.concatenate([x.astype(BF16), y.astype(BF16)], axis=0)

    bd2 = lambda x, y: jnp.concatenate([bd(x), bd(y)], axis=1)
    gram = [_dot(rows2(kt[g], rt[g]), bd2(bt_t[g], kb_t[g])) for g in ng]
    a_m = [jnp.where(strict[dirs[g]], gram[g][0:n, 0:qw], 0.0) for g in ng]
    c_m = [jnp.where(incl[dirs[g]], gram[g][n:2 * n, 0:qw], 0.0) for g in ng]
    b_m = [jnp.where(strict[dirs[g]], gram[g][0:n, qw:2 * qw], 0.0) for g in ng]
    e_m = [jnp.where(incl[dirs[g]], gram[g][n:2 * n, qw:2 * qw], 0.0) for g in ng]

    dd = [jnp.where(same16, a_m[g], 0.0) for g in ng]
    pw = [mm(dd[g], bd(dd[g])) for g in ng]
    t = [eye - dd[g] for g in ng]
    for _ in range(2):
        both = [_dot(rows2(t[g], pw[g]), bd(pw[g])) for g in ng]
        t = [t[g] + both[g][0:n] for g in ng]
        pw = [both[g][n:2 * n] for g in ng]
    t = [t[g] + mm(t[g], bd(pw[g])) for g in ng]
    for sib in (sib32, sib64):
        ta = [mm(t[g], bd(jnp.where(sib, a_m[g], 0.0))) for g in ng]
        t = [t[g] - mm(ta[g], bd(t[g])) for g in ng]

    bev = [_dot(jnp.concatenate([rows2(b_m[g], e_m[g]), kb_t[g]], axis=0), bd(v[g])) for g in ng]
    kv = [bev[g][2 * n:3 * n] for g in ng]
    mw = [mm(t[g], bd2(kt[g], bev[g][0:n])) for g in ng]
    m1 = [mw[g][:, 0:qw] for g in ng]
    w2 = [mw[g][:, qw:2 * qw] for g in ng]
    cq = [mm(c_m[g], bd2(m1[g], w2[g])) for g in ng]
    qh = [rt[g].astype(F32) - cq[g][:, 0:qw] for g in ng]
    y0 = [bev[g][n:2 * n] - cq[g][:, qw:2 * qw] for g in ng]

    chains = len(groups) // SCAN_CHUNKS
    h = [h_ref[s] for s in range(chains)]
    for g, (d, q, j) in enumerate(groups):
        s = g % chains
        uy = _dot(rows2(m1[g], qh[g]), bd(h[s]))
        (y0_ref, y1_ref)[d][row0(d, j):row0(d, j) + n, q * qw:(q + 1) * qw] = uy[n:2 * n] + y0[g]
        h[s] = pc[g] * (h[s] - _dot(bt_t[g], bd(uy[0:n] + w2[g])) + kv[g])
    for s in range(chains):
        h_ref[s] = h[s]


def _rwkv_scan(v, ops0, ops1, *, n_batch, seq, ctx_len):
    nt = v.shape[0]
    n = CHUNK * SCAN_CHUNKS
    assert seq % n == 0 and ctx_len % n == 0
    lat_c = seq // n
    ctx_c = ctx_len // n
    ctx0 = n_batch * lat_c

    def idx_f(bi, c):
        return (jnp.where(c < ctx_c, ctx0 + bi * ctx_c + c, bi * lat_c + (c - ctx_c)), 0)

    def idx_r(bi, c):
        return (jnp.where(c < ctx_c, ctx0 + bi * ctx_c + (ctx_c - 1 - c), bi * lat_c + (lat_c - 1 - (c - ctx_c))), 0)

    spec_f = pl.BlockSpec((n, RWKV_WIDTH), idx_f)
    spec_r = pl.BlockSpec((n, RWKV_WIDTH), idx_r)
    out = jax.ShapeDtypeStruct((nt, RWKV_WIDTH), F32)
    return pl.pallas_call(
        _scan_kernel,
        out_shape=(out, out),
        grid=(n_batch, ctx_c + lat_c),
        in_specs=[spec_f] * 6 + [spec_r] * 6,
        out_specs=(spec_f, spec_r),
        scratch_shapes=[pltpu.VMEM((2 * HEADS // QUAD, CHUNK, QUAD_W), F32)],
        compiler_params=_params(("parallel", "arbitrary")),
        name="rwkv_scan",
    )(v, *ops0, v, *ops1)


def _mix_out_kernel(*refs, lat_tiles):
    (yf_ref, yb_ref, g_ref, bon_ref, lng_ref, lnb_ref, ones_ref, woa_ref, wor_ref, xl_ref, xc_ref, mod_ref, pg_ref,
     att_ref) = refs[:14]
    o_ref = refs[-1]
    is_lat = pl.program_id(0) < lat_tiles
    ones_bd = ones_ref[...]
    y = yf_ref[...] + yb_ref[...]
    inv_n = 1.0 / RWKV_HEAD
    mu = _seg_sum(y, ones_bd) * inv_n
    yc = y - mu
    var = _seg_sum(yc * yc, ones_bd) * inv_n
    yn = yc * lax.rsqrt(var + LNX_EPS) * lng_ref[...] + lnb_ref[...]
    rw = (yn + bon_ref[...]) * g_ref[...]
    att = att_ref[...]
    if len(refs) == 16:
        att = jnp.where(is_lat, att, refs[14][...])
    mix = _dot(att, woa_ref[...]) + _dot(rw.astype(BF16), wor_ref[...])
    x = jnp.where(is_lat, xl_ref[...], xc_ref[...])
    o_ref[...] = x + mod_ref[0, 2:3, :] * _rms(mix, pg_ref[...])


def _mix_out(yf, yb, att_lat, att_ctx, g, bon, lnx_g, lnx_b, ones_bd, wo_att, wo_rw, x_lat, x_ctx, mods, post_g, *,
             rows, n_lat, seq):
    tm = TOKEN_TILE
    lat_tiles = n_lat // tm
    n_batch = n_lat // seq
    w = RWKV_WIDTH

    def mod_idx(i):
        return (jnp.where(i < lat_tiles, (i * tm) // seq, n_batch), 0, 0)

    const = lambda i: (0, 0)
    row = lambda i: (i, 0)
    in_specs = [
        pl.BlockSpec((tm, w), row),
        pl.BlockSpec((tm, w), row),
        pl.BlockSpec((tm, w), row),
        pl.BlockSpec((tm, w), row),
        pl.BlockSpec((1, w), const),
        pl.BlockSpec((1, w), const),
        pl.BlockSpec((w, w), const),
        pl.BlockSpec((w, D_MODEL), const),
        pl.BlockSpec((w, D_MODEL), const),
        *_stream_specs(tm, lat_tiles, x_ctx is x_lat),
        pl.BlockSpec((1, N_MOD, D_MODEL), mod_idx),
        pl.BlockSpec((1, D_MODEL), const),
        pl.BlockSpec((tm, w), lambda i: (jnp.minimum(i, lat_tiles - 1), 0)),
    ]
    args = [yf, yb, g, bon, lnx_g, lnx_b, ones_bd, wo_att, wo_rw, x_lat, x_ctx, mods, post_g, att_lat]
    if att_ctx is not None:
        in_specs.append(pl.BlockSpec((tm, w), lambda i: (jnp.maximum(i - lat_tiles, 0), 0)))
        args.append(att_ctx)
    return pl.pallas_call(
        functools.partial(_mix_out_kernel, lat_tiles=lat_tiles),
        out_shape=jax.ShapeDtypeStruct((rows, D_MODEL), F32),
        grid=(rows // tm,),
        in_specs=in_specs,
        out_specs=pl.BlockSpec((tm, D_MODEL), row),
        compiler_params=_params(("parallel",)),
        name="mix_out",
    )(*args)


def _route(logits_t, bias_col):
    scores = jax.nn.sigmoid(logits_t[0:N_EXPERTS, :])
    biased = scores + bias_col
    s_rows = [scores[e:e + 1, :] for e in range(N_EXPERTS)]
    b_rows = [biased[e:e + 1, :] for e in range(N_EXPERTS)]
    npg = EXPERTS_PER_GROUP
    group_scores = []
    for gi in range(N_GROUPS):
        bg = b_rows[gi * npg:(gi + 1) * npg]
        best_pair = None
        for i in range(npg):
            for j in range(i + 1, npg):
                pair = bg[i] + bg[j]
                best_pair = pair if best_pair is None else jnp.maximum(best_pair, pair)
        group_scores.append(best_pair)
    best = group_scores[0]
    best_idx = jnp.zeros(best.shape, jnp.int32)
    for gi in range(1, N_GROUPS):
        upd = group_scores[gi] > best
        best = jnp.where(upd, group_scores[gi], best)
        best_idx = jnp.where(upd, gi, best_idx)
    pick = lambda rows, j: functools.reduce(
        lambda acc, gi: jnp.where(best_idx == gi, rows[gi * npg + j], acc), range(1, N_GROUPS), rows[j])
    bb = [pick(b_rows, j) for j in range(npg)]
    ss = [pick(s_rows, j) for j in range(npg)]
    weights = []
    for j in range(npg):
        rank = jnp.zeros(best.shape, jnp.int32)
        for i in range(npg):
            if i == j:
                continue
            beats = (bb[i] > bb[j]) | ((bb[i] == bb[j]) & (i < j)) if i < j else (bb[i] > bb[j])
            rank = rank + beats.astype(jnp.int32)
        weights.append(jnp.where(rank < 2, ss[j], 0.0))
    den = weights[0] + weights[1] + weights[2] + weights[3]
    return [wj / den for wj in weights], best_idx


def _swiglu(h_b, wgu, wd, gate=None):
    gu = _dot(h_b, wgu)
    g_part = gu[:, 0:EXPERT_HIDDEN]
    act = g_part * jax.nn.sigmoid(g_part) * gu[:, EXPERT_HIDDEN:]
    if gate is not None:
        act = act * gate
    return _dot(act.astype(BF16), wd)


def _moe_kernel(x_ref, mod_ref, pre_ref, post_ref, rw_ref, rb_ref, before_ref, wgu_ref, wd_ref, o_ref, sorted_scr,
                f_scr):
    tm = x_ref.shape[0]
    blk = MOE_BLOCK
    n_blocks = sorted_scr.shape[0] // blk
    x = x_ref[...]
    h = _rms(x, pre_ref[...]) * (1.0 + mod_ref[0, 4:5, :]) + mod_ref[0, 3:4, :]
    h_hi = h.astype(BF16)
    h_lo = (h - h_hi.astype(F32)).astype(BF16)
    rw = rw_ref[...]
    rw_hi = rw.astype(BF16)
    rw_lo = (rw - rw_hi.astype(F32)).astype(BF16)
    logits_t = _dot_nt(rw_hi, h_hi) + _dot_nt(rw_lo, h_hi) + _dot_nt(rw_hi, h_lo)
    gates, group = _route(logits_t, rb_ref[...])

    sub8 = lax.broadcasted_iota(jnp.int32, (8, tm), 0)
    onehot = jnp.where(sub8 == group, 1.0, 0.0)
    before = _dot(onehot.astype(BF16), before_ref[...])
    counts = [jnp.sum(onehot[g:g + 1, :]).astype(jnp.int32) for g in range(N_GROUPS)]
    ends, acc = [], jnp.int32(0)
    for g in range(N_GROUPS):
        acc = acc + ((counts[g] + (blk - 1)) // blk) * blk
        ends.append(acc)
    starts = [jnp.int32(0)] + ends[:-1]
    pos = sum(onehot[g:g + 1, :] * (before[g:g + 1, :] + starts[g].astype(F32)) for g in range(N_GROUPS))

    sub = lax.broadcasted_iota(jnp.int32, (LANES, tm), 0)
    stack = jnp.where(sub == EXPERTS_PER_GROUP, pos, 0.0)
    for j in range(EXPERTS_PER_GROUP):
        stack = jnp.where(sub == j, gates[j], stack)
    cols = stack.T
    g_hi = cols.astype(BF16)
    g_lo = (cols - g_hi.astype(F32)).astype(BF16)
    pos_row = pos.astype(jnp.int32)
    pos_col = cols[:, EXPERTS_PER_GROUP:EXPERTS_PER_GROUP + 1].astype(jnp.int32)

    f_shared = _swiglu(h_hi, wgu_ref[N_EXPERTS], wd_ref[N_EXPERTS])

    for s in range(n_blocks):
        r0 = s * blk
        rows = pl.ds(r0, blk)
        grp = sum((r0 >= ends[g]).astype(jnp.int32) for g in range(N_GROUPS - 1))

        @pl.when(r0 < ends[-1])
        def _():
            r_i = lax.broadcasted_iota(jnp.int32, (blk, tm), 0) + r0
            take = jnp.where(r_i == pos_row, 1.0, 0.0).astype(BF16)
            h_s = _dot(take, h_hi).astype(BF16)
            g_s = _dot(take, g_hi) + _dot(take, g_lo)
            out = None
            for j in range(EXPERTS_PER_GROUP):
                e = grp * EXPERTS_PER_GROUP + j
                y = _swiglu(h_s, wgu_ref[e], wd_ref[e], g_s[:, j:j + 1])
                out = y if out is None else out + y
            sorted_scr[rows, :] = out.astype(BF16)

        @pl.when(r0 >= ends[-1])
        def _():
            sorted_scr[rows, :] = jnp.zeros((blk, D_MODEL), BF16)

    main, total = MOE_MAIN_ROWS, n_blocks * blk

    def put_back(c0, c1):
        c_i = lax.broadcasted_iota(jnp.int32, (tm, c1 - c0), 1) + c0
        put = jnp.where(c_i == pos_col, 1.0, 0.0).astype(BF16)
        return _dot(put, sorted_scr[c0:c1, :])

    f_scr[...] = f_shared + put_back(0, main)

    @pl.when(ends[-1] > main)
    def _():
        f_scr[...] += put_back(main, total)

    o_ref[...] = x + mod_ref[0, 5:6, :] * _rms(f_scr[...], post_ref[...])


def _moe(x_all, mods, pre_g, post_g, router_wt, router_b, before_m, wgu, wd, *, rows, n_lat, seq):
    tm = MOE_TILE
    lat_tiles = n_lat // tm
    n_batch = n_lat // seq
    n_e = wgu.shape[0]

    def mod_idx(i):
        return (jnp.where(i < lat_tiles, (i * tm) // seq, n_batch), 0, 0)

    const = lambda i: (0, 0)
    const3 = lambda i: (0, 0, 0)
    row = lambda i: (i, 0)
    resident = pl.Buffered(1)
    return pl.pallas_call(
        _moe_kernel,
        out_shape=jax.ShapeDtypeStruct((rows, D_MODEL), F32),
        grid=(rows // tm,),
        in_specs=[
            pl.BlockSpec((tm, D_MODEL), row),
            pl.BlockSpec((1, N_MOD, D_MODEL), mod_idx),
            pl.BlockSpec((1, D_MODEL), const),
            pl.BlockSpec((1, D_MODEL), const),
            pl.BlockSpec((LANES, D_MODEL), const),
            pl.BlockSpec((N_EXPERTS, 1), const),
            pl.BlockSpec((tm, tm), const),
            pl.BlockSpec((n_e, D_MODEL, 2 * EXPERT_HIDDEN), const3, pipeline_mode=resident),
            pl.BlockSpec((n_e, EXPERT_HIDDEN, D_MODEL), const3, pipeline_mode=resident),
        ],
        out_specs=pl.BlockSpec((tm, D_MODEL), row),
        scratch_shapes=[pltpu.VMEM((MOE_SORTED_ROWS, D_MODEL), BF16), pltpu.VMEM((tm, D_MODEL), F32)],
        compiler_params=_params(("parallel",)),
        name="moe",
    )(x_all, mods, pre_g, post_g, router_wt, router_b, before_m, wgu, wd)


def _pack_in_proj(w_in, shift_mu):
    d = w_in.shape[0]
    z = lambda n: jnp.zeros((d, n), w_in.dtype)
    kr = w_in[:, Q_RANK + KV_RANK:MLA_COLS]
    half = MLA_ROPE // 2
    kr_sw = jnp.concatenate([kr[:, half:], kr[:, :half]], axis=1)
    pad = LANES - MLA_NOPE - MLA_ROPE
    packed = jnp.concatenate([
        w_in[:, :Q_RANK + KV_RANK],
        z(MLA_NOPE), kr, z(pad),
        z(MLA_NOPE), kr_sw, z(pad),
        w_in[:, MLA_COLS:], z(RWKV_PAD - RWKV_COLS),
    ], axis=1)
    mu = jnp.pad(shift_mu, ((0, 0), (0, RWKV_PAD - RWKV_COLS)))
    return packed.astype(BF16), mu


def _pack_mla(w_uq, w_ukv):
    half = MLA_ROPE // 2
    dq = MLA_NOPE + MLA_ROPE
    q3 = w_uq.reshape(Q_RANK, HEADS, dq)
    zq = lambda n: jnp.zeros((Q_RANK, HEADS, n), w_uq.dtype)
    pad = HEAD_SLOT - dq
    wq1 = jnp.concatenate([q3, zq(pad)], axis=2).reshape(Q_RANK, HEADS * HEAD_SLOT)
    wq2 = jnp.concatenate([zq(MLA_NOPE), q3[:, :, MLA_NOPE + half:], q3[:, :, MLA_NOPE:MLA_NOPE + half], zq(pad)],
                          axis=2).reshape(Q_RANK, HEADS * HEAD_SLOT)
    kv3 = w_ukv.reshape(KV_RANK, HEADS, MLA_NOPE + MLA_V)
    wuk = jnp.concatenate([kv3[:, :, :MLA_NOPE], jnp.zeros((KV_RANK, HEADS, HEAD_SLOT - MLA_NOPE), w_ukv.dtype)],
                          axis=2).reshape(KV_RANK, HEADS * HEAD_SLOT)
    wuv = kv3[:, :, MLA_NOPE:].reshape(KV_RANK, HEADS * MLA_V).T
    return wq1.astype(BF16), wq2.astype(BF16), wuk.astype(BF16), wuv.astype(BF16)


def _block_diag2(m):
    r, c = m.shape[1], m.shape[2]
    z = jnp.zeros((r, c), m.dtype)
    return jnp.concatenate([jnp.concatenate([m[0], z], axis=1), jnp.concatenate([z, m[1]], axis=1)], axis=0)


def _rope_tables(seq, ctx_len):
    axis_dim = MLA_ROPE // 2
    t = jnp.arange(seq, dtype=jnp.int32)
    row = (t // GRID_W).astype(F32)
    col = (t % GRID_W).astype(F32)
    inv_freq = ROPE_BASE ** (-jnp.arange(0, axis_dim, 2, dtype=F32) / axis_dim)
    ang = jnp.concatenate([row[:, None] * inv_freq, col[:, None] * inv_freq], axis=-1)
    cos = jnp.concatenate([jnp.cos(ang), jnp.ones((ctx_len, axis_dim), F32)], axis=0)
    sin = jnp.concatenate([jnp.sin(ang), jnp.zeros((ctx_len, axis_dim), F32)], axis=0)
    n = seq + ctx_len
    ones = jnp.ones((n, MLA_NOPE), F32)
    z_nope = jnp.zeros((n, MLA_NOPE), F32)
    z_pad = jnp.zeros((n, HEAD_SLOT - MLA_NOPE - MLA_ROPE), F32)
    cq = jnp.concatenate([ones, cos, cos, z_pad], axis=1)
    sq = jnp.concatenate([z_nope, -sin, sin, z_pad], axis=1)
    ck = jnp.concatenate([z_nope, cos, cos, z_pad], axis=1)
    return cq, sq, ck


def kernel(x, c, ctx, c_ctx, ada_w, ada_b, mix_pre_g, mix_post_g, ffn_pre_g, ffn_post_g, w_in, q_norm_g, kv_norm_g,
           w_uq, w_ukv, shift_mu, decay_w0, decay_w2, iclr_a0, iclr_a2, gate_g2, k_k, k_a, r_k, lnx_g, lnx_b, w_out,
           router_w, router_bias, exp_w_gate, exp_w_up, exp_w_down, sh_w_gate, sh_w_up, sh_w_down):
    n_batch, seq, d = x.shape
    ctx_len = ctx.shape[1]
    depth = ada_w.shape[0]
    assert d == D_MODEL and seq % TOKEN_TILE == 0 and ctx_len % TOKEN_TILE == 0 and seq % ctx_len == 0
    assert seq % GRID_W == 0 and n_batch + 1 <= 8
    n_lat = n_batch * seq
    n_ctx = n_batch * ctx_len
    assert seq % MOE_TILE == 0 and n_ctx % MOE_TILE == 0

    x_lat, x_ctx = x.reshape(n_lat, d), ctx.reshape(n_ctx, d)
    cond_rows = jnp.concatenate([c, c_ctx[None], jnp.zeros((8 - n_batch - 1, d), F32)], axis=0)
    mods_all = _ada_modulation(cond_rows, ada_w, ada_b).reshape(depth, 8, N_MOD, d)

    cq_t, sq_t, ck_t = _rope_tables(seq, ctx_len)
    w = RWKV_WIDTH
    ones_bd = (jnp.arange(w)[:, None] // RWKV_HEAD == jnp.arange(w)[None, :] // RWKV_HEAD).astype(BF16)
    ti = jnp.arange(TOKEN_TILE)
    same_chunk = (ti[:, None] // CHUNK) == (ti[None, :] // CHUNK)
    ones_c = same_chunk.astype(BF16)
    tri_f = jnp.logical_and(same_chunk, ti[None, :] <= ti[:, None]).astype(BF16)
    tri_r = jnp.logical_and(same_chunk, ti[None, :] >= ti[:, None]).astype(BF16)
    router_wt = jnp.pad(router_w.T, ((0, LANES - N_EXPERTS), (0, 0)))
    router_b = router_bias.reshape(N_EXPERTS, 1)
    tj = jnp.arange(MOE_TILE)
    before_m = (tj[:, None] < tj[None, :]).astype(BF16)
    row1 = lambda a: a.reshape(1, -1)

    for l in range(depth):
        ctx_out = l < depth - 1
        mods = mods_all[l]
        rows = n_lat + n_ctx if ctx_out else n_lat

        w_in_p, mu = _pack_in_proj(w_in[l], shift_mu[l])
        wq1, wq2, wuk, wuv = _pack_mla(w_uq[l], w_ukv[l])
        q, k, v, p_rw = _in_proj(x_lat, x_ctx, mods, row1(mix_pre_g[l]), w_in_p, row1(q_norm_g[l]),
                                 row1(kv_norm_g[l]), wq1, wq2, wuk, wuv, cq_t, sq_t, ck_t, n_lat=n_lat, n_ctx=n_ctx,
                                 seq=seq, ctx_len=ctx_len)
        att = _attention(q, k, v, n_batch=n_batch, seq=seq, ctx_len=ctx_len, latent=True)
        att_c = _attention(q, k, v, n_batch=n_batch, seq=seq, ctx_len=ctx_len, latent=False) if ctx_out else None

        g2p = jnp.pad(gate_g2[l], ((0, GATE_PAD - GATE_LORA), (0, 0))).astype(BF16)
        prep = _rwkv_prep(
            p_rw, mu, _block_diag2(decay_w2[l]).astype(BF16), decay_w0[l].reshape(1, 2 * w),
            _block_diag2(iclr_a2[l]).astype(BF16), iclr_a0[l].reshape(1, 2 * w), g2p,
            row1(k_k[l]), row1(k_a[l]), row1(r_k[l]), ones_bd, tri_f, tri_r, ones_c,
            n_lat=n_lat, seq=seq, ctx_len=ctx_len)
        vv, g, bon = prep[0], prep[11], prep[12]
        yf, yb = _rwkv_scan(vv, prep[1:6], prep[6:11], n_batch=n_batch, seq=seq, ctx_len=ctx_len)

        wo = w_out[l].astype(BF16)
        x_all = _mix_out(yf, yb, att, att_c, g, bon, row1(lnx_g[l]), row1(lnx_b[l]), ones_bd, wo[:w], wo[w:], x_lat,
                         x_ctx, mods, row1(mix_post_g[l]), rows=rows, n_lat=n_lat, seq=seq)

        wgu = jnp.concatenate([
            jnp.concatenate([exp_w_gate[l], exp_w_up[l]], axis=2),
            jnp.concatenate([sh_w_gate[l], sh_w_up[l]], axis=1)[None]], axis=0).astype(BF16)
        wd = jnp.concatenate([exp_w_down[l], sh_w_down[l][None]], axis=0).astype(BF16)
        x_all = _moe(x_all, mods, row1(ffn_pre_g[l]), row1(ffn_post_g[l]), router_wt, router_b, before_m, wgu, wd,
                     rows=rows, n_lat=n_lat, seq=seq)
        x_lat = x_ctx = x_all

    return x_all.reshape(n_batch, seq, d)
```

```python
import functools
import math

import jax
import jax.numpy as jnp
from jax import lax
from jax.experimental import pallas as pl
from jax.experimental.pallas import tpu as pltpu

F32 = jnp.float32
BF16 = jnp.bfloat16
HIGHEST = lax.Precision.HIGHEST

D_MODEL = 1024
N_MOD = 6
NORM_EPS = 1e-6
GRID_W = 64
ROPE_BASE = 10000.0

HEADS = 8
MLA_NOPE = 64
MLA_ROPE = 32
MLA_V = 64
Q_RANK = 256
KV_RANK = 128
MLA_COLS = Q_RANK + KV_RANK + MLA_ROPE

RWKV_HEAD = 64
RWKV_WIDTH = HEADS * RWKV_HEAD
DECAY_LORA = 64
ICLR_LORA = 64
GATE_LORA = 160
RWKV_COLS = 3 * RWKV_WIDTH + 2 * DECAY_LORA + 2 * ICLR_LORA + GATE_LORA
RWKV_PAD = 2048
GATE_PAD = 256
LNX_EPS = 64e-5

N_EXPERTS = 16
N_GROUPS = 4
EXPERTS_PER_GROUP = 4
EXPERT_HIDDEN = 256

LANES = 128
SUBLANES = 8
HEAD_SLOT = 128
IN_PACKED = Q_RANK + KV_RANK + 2 * LANES + RWKV_PAD
TOKEN_TILE = 256
ATTN_KV_BLOCK = 1024
ATTN_Q_TILE = 512
MOE_TILE = 512
MOE_BLOCK = 128
CHUNK = 64
INV_DIAG = 16
ADA_TILE = 1536
VMEM_LIMIT = 48 * 1024 * 1024


def _dot(a, b):
    return jnp.dot(a, b, preferred_element_type=F32)


def _dot_nt(a, b):
    return lax.dot_general(a, b, (((1,), (1,)), ((), ())), preferred_element_type=F32)


def _rms(x, g):
    return x * lax.rsqrt(jnp.mean(x * x, axis=-1, keepdims=True) + NORM_EPS) * g


def _seg_sum(x, ones_bd):
    hi = x.astype(BF16)
    lo = (x - hi.astype(F32)).astype(BF16)
    return _dot(hi, ones_bd) + _dot(lo, ones_bd)


def _params(sem):
    return pltpu.CompilerParams(dimension_semantics=sem, vmem_limit_bytes=VMEM_LIMIT)


def _ada_kernel(c_ref, w_ref, b_ref, o_ref):
    c = c_ref[...]
    cond = c * jax.nn.sigmoid(c)
    o_ref[0] = jnp.dot(cond, w_ref[0], precision=HIGHEST, preferred_element_type=F32) + b_ref[0]


def _ada_modulation(cond_rows, ada_w, ada_b):
    depth, d, n = ada_w.shape
    tn = ADA_TILE
    rows = cond_rows.shape[0]
    return pl.pallas_call(
        _ada_kernel,
        out_shape=jax.ShapeDtypeStruct((depth, rows, n), F32),
        grid=(depth, n // tn),
        in_specs=[
            pl.BlockSpec((rows, d), lambda l, j: (0, 0)),
            pl.BlockSpec((1, d, tn), lambda l, j: (l, 0, j)),
            pl.BlockSpec((1, 1, tn), lambda l, j: (l, 0, j)),
        ],
        out_specs=pl.BlockSpec((1, rows, tn), lambda l, j: (l, 0, j)),
        compiler_params=_params(("parallel", "parallel")),
        name="ada_modulation",
    )(cond_rows, ada_w, ada_b.reshape(depth, 1, n))


def _stream_specs(tm, lat_tiles, merged):
    off = lat_tiles if merged else 0
    return (pl.BlockSpec((tm, D_MODEL), lambda i: (jnp.minimum(i, lat_tiles - 1), 0)),
            pl.BlockSpec((tm, D_MODEL), lambda i: (jnp.maximum(i - lat_tiles, 0) + off, 0)))


def _in_proj_kernel(xl_ref, xc_ref, mod_ref, g_ref, win_ref, qng_ref, kvng_ref, wq1_ref, wq2_ref, wuk_ref, wuv_ref,
                    cq_ref, sq_ref, ck_ref, q_out, k_out, v_out, p_out, *, q_scale, lat_tiles):
    x = jnp.where(pl.program_id(0) < lat_tiles, xl_ref[...], xc_ref[...])
    shift = mod_ref[0, 0:1, :]
    scale = mod_ref[0, 1:2, :]
    h = _rms(x, g_ref[...]) * (1.0 + scale) + shift
    p = _dot(h.astype(BF16), win_ref[...])
    c_q = p[:, 0:Q_RANK]
    c_kv = p[:, Q_RANK:Q_RANK + KV_RANK]
    kr_a = p[:, Q_RANK + KV_RANK:Q_RANK + KV_RANK + LANES]
    kr_b = p[:, Q_RANK + KV_RANK + LANES:Q_RANK + KV_RANK + 2 * LANES]
    p_out[...] = p[:, Q_RANK + KV_RANK + 2 * LANES:]

    tile8 = lambda t: jnp.concatenate([t] * HEADS, axis=1)
    cqn = _rms(c_q, qng_ref[...]).astype(BF16)
    q = _dot(cqn, wq1_ref[...]) * tile8(cq_ref[...]) + _dot(cqn, wq2_ref[...]) * tile8(sq_ref[...])
    q_out[...] = (q * q_scale).astype(BF16)

    ckvn = _rms(c_kv, kvng_ref[...]).astype(BF16)
    k_rot = kr_a * ck_ref[...] + kr_b * sq_ref[...]
    k_out[...] = (_dot(ckvn, wuk_ref[...]) + tile8(k_rot)).astype(BF16)
    v_out[...] = _dot_nt(wuv_ref[...], ckvn).astype(BF16)


def _in_proj(x_lat, x_ctx, mods, g, w_in_p, qng, kvng, wq1, wq2, wuk, wuv, cq_t, sq_t, ck_t, *, n_lat, n_ctx, seq,
             ctx_len):
    nt = n_lat + n_ctx
    tm = TOKEN_TILE
    lat_tiles = n_lat // tm
    n_batch = n_lat // seq

    def mod_idx(i):
        return (jnp.where(i < lat_tiles, (i * tm) // seq, n_batch), 0, 0)

    def tab_idx(i):
        return (jnp.where(i < lat_tiles, i % (seq // tm), seq // tm + (i - lat_tiles) % (ctx_len // tm)), 0)

    const = lambda i: (0, 0)
    row = lambda i: (i, 0)
    qw = HEADS * HEAD_SLOT
    return pl.pallas_call(
        functools.partial(_in_proj_kernel, q_scale=float((MLA_NOPE + MLA_ROPE) ** -0.5 * math.log2(math.e)),
                          lat_tiles=lat_tiles),
        out_shape=(
            jax.ShapeDtypeStruct((nt, qw), BF16),
            jax.ShapeDtypeStruct((nt, qw), BF16),
            jax.ShapeDtypeStruct((HEADS * MLA_V, nt), BF16),
            jax.ShapeDtypeStruct((nt, RWKV_PAD), F32),
        ),
        grid=(nt // tm,),
        in_specs=[
            *_stream_specs(tm, lat_tiles, x_ctx is x_lat),
            pl.BlockSpec((1, N_MOD, D_MODEL), mod_idx),
            pl.BlockSpec((1, D_MODEL), const),
            pl.BlockSpec((D_MODEL, IN_PACKED), const),
            pl.BlockSpec((1, Q_RANK), const),
            pl.BlockSpec((1, KV_RANK), const),
            pl.BlockSpec((Q_RANK, qw), const),
            pl.BlockSpec((Q_RANK, qw), const),
            pl.BlockSpec((KV_RANK, qw), const),
            pl.BlockSpec((HEADS * MLA_V, KV_RANK), const),
            pl.BlockSpec((tm, HEAD_SLOT), tab_idx),
            pl.BlockSpec((tm, HEAD_SLOT), tab_idx),
            pl.BlockSpec((tm, HEAD_SLOT), tab_idx),
        ],
        out_specs=(
            pl.BlockSpec((tm, qw), row),
            pl.BlockSpec((tm, qw), row),
            pl.BlockSpec((HEADS * MLA_V, tm), lambda i: (0, i)),
            pl.BlockSpec((tm, RWKV_PAD), row),
        ),
        compiler_params=_params(("parallel",)),
        name="in_proj",
    )(x_lat, x_ctx, mods, g, w_in_p, qng, kvng, wq1, wq2, wuk, wuv, cq_t, sq_t, ck_t)


def _attn_kernel(q_ref, kc_ref, vc_ref, *rest):
    o_ref = rest[-1]
    blocks = [(kc_ref, vc_ref, 0, kc_ref.shape[0])]
    if len(rest) == 3:
        kl_ref, vl_ref = rest[0], rest[1]
        kb = min(ATTN_KV_BLOCK, kl_ref.shape[0])
        blocks += [(kl_ref, vl_ref, s0, kb) for s0 in range(0, kl_ref.shape[0], kb)]
    heads = range(2)
    hs = [slice(h * HEAD_SLOT, (h + 1) * HEAD_SLOT) for h in heads]
    vs = [slice(h * MLA_V, (h + 1) * MLA_V) for h in heads]
    q = [q_ref[:, hs[h]] for h in heads]

    def scores(j):
        k_ref, _, s0, size = blocks[j]
        return [_dot_nt(k_ref[s0:s0 + size, hs[h]], q[h]) for h in heads]

    m, den, acc = [None] * 2, [None] * 2, [None] * 2
    s_cur = scores(0)
    for j, (_, vt_ref, s0, size) in enumerate(blocks):
        s_next = scores(j + 1) if j + 1 < len(blocks) else None
        for h in heads:
            s = s_cur[h]
            m_blk = jnp.max(s, axis=0, keepdims=True)
            if j == 0:
                m[h] = m_blk
                p = jnp.exp2(s - m_blk)
                den[h] = jnp.sum(p, axis=0, keepdims=True)
                acc[h] = _dot(vt_ref[vs[h], s0:s0 + size], p.astype(BF16))
            else:
                m_new = jnp.maximum(m[h], m_blk)
                alpha = jnp.exp2(m[h] - m_new)
                p = jnp.exp2(s - m_new)
                den[h] = alpha * den[h] + jnp.sum(p, axis=0, keepdims=True)
                acc[h] = alpha * acc[h] + _dot(vt_ref[vs[h], s0:s0 + size], p.astype(BF16))
                m[h] = m_new
        s_cur = s_next
    out_t = jnp.concatenate([acc[h] / den[h] for h in heads], axis=0)
    o_ref[...] = out_t.T.astype(o_ref.dtype)


def _attention(q, k, v_t, *, n_batch, seq, ctx_len, latent):
    tq = ATTN_Q_TILE if latent else TOKEN_TILE
    seg = seq if latent else ctx_len
    assert seg % tq == 0
    q_tiles = seg // tq
    q_blk0 = 0 if latent else (n_batch * seq) // tq
    ctx_blk0 = (n_batch * seq) // ctx_len
    kv_ctx = lambda b, hp, j: (ctx_blk0 + b, hp)
    kv_lat = lambda b, hp, j: (b, hp)
    in_specs = [
        pl.BlockSpec((tq, 2 * HEAD_SLOT), lambda b, hp, j: (q_blk0 + b * q_tiles + j, hp)),
        pl.BlockSpec((ctx_len, 2 * HEAD_SLOT), kv_ctx),
        pl.BlockSpec((2 * MLA_V, ctx_len), lambda b, hp, j: (hp, ctx_blk0 + b)),
    ]
    args = [q, k, v_t]
    if latent:
        in_specs += [pl.BlockSpec((seq, 2 * HEAD_SLOT), kv_lat),
                     pl.BlockSpec((2 * MLA_V, seq), lambda b, hp, j: (hp, b))]
        args += [k, v_t]
    return pl.pallas_call(
        _attn_kernel,
        out_shape=jax.ShapeDtypeStruct((n_batch * seg, HEADS * MLA_V), BF16),
        grid=(n_batch, HEADS // 2, q_tiles),
        in_specs=in_specs,
        out_specs=pl.BlockSpec((tq, 2 * MLA_V), lambda b, hp, j: (b * q_tiles + j, hp)),
        compiler_params=_params(("parallel", "parallel", "arbitrary")),
        name="attention_lat" if latent else "attention_ctx",
    )(*args)


def _split2(x):
    hi = x.astype(BF16)
    return hi, (x - hi.astype(F32)).astype(BF16)


def _chunk_transpose(x):
    n = CHUNK
    xt = x.T
    rows = []
    for c in range(x.shape[0] // n):
        rows.append(jnp.concatenate(
            [xt[h * n:(h + 1) * n, c * n:(c + 1) * n] for h in range(x.shape[1] // n)], axis=1))
    return jnp.concatenate(rows, axis=0)


def _rwkv_prep_kernel(p_ref, hp_ref, hn_ref, mu_ref, w2_ref, w0_ref, a2_ref, a0_ref, g2_ref, kk_ref, ka_ref, rk_ref,
                      ones_ref, trif_ref, trir_ref, onesc_ref, v_out, rt0_out, kt0_out, bt0_out, kb0_out, pc0_out,
                      rt1_out, kt1_out, bt1_out, kb1_out, pc1_out, g_out, bon_out, *, lat_tiles, seq_tiles,
                      ctx_tiles):
    i = pl.program_id(0)
    tm = p_ref.shape[0]
    is_lat = i < lat_tiles
    local = jnp.where(is_lat, i % seq_tiles, (i - lat_tiles) % ctx_tiles)
    seg = jnp.where(is_lat, seq_tiles, ctx_tiles)
    keep_prev = jnp.where(local == 0, 0.0, 1.0)
    keep_next = jnp.where(local == seg - 1, 0.0, 1.0)

    p = p_ref[...]
    row = lax.broadcasted_iota(jnp.int32, (tm, 1), 0)
    prev = jnp.where(row == 0, hp_ref[7:8, :] * keep_prev, pltpu.roll(p, 1, 0))
    nxt = jnp.where(row == tm - 1, hn_ref[0:1, :] * keep_next, pltpu.roll(p, tm - 1, 0))
    ps = p + mu_ref[0:1, :] * (prev - p) + mu_ref[1:2, :] * (nxt - p)

    w = RWKV_WIDTH
    r = ps[:, 0:w]
    k = ps[:, w:2 * w]
    v = ps[:, 2 * w:3 * w]
    wl = ps[:, 3 * w:3 * w + 2 * DECAY_LORA]
    al = ps[:, 3 * w + 2 * DECAY_LORA:3 * w + 2 * DECAY_LORA + 2 * ICLR_LORA]
    gl = ps[:, 3 * w + 2 * DECAY_LORA + 2 * ICLR_LORA:3 * w + 2 * DECAY_LORA + 2 * ICLR_LORA + GATE_PAD]

    g_out[...] = _dot(jax.nn.sigmoid(gl).astype(BF16), g2_ref[...])
    z = w0_ref[...] + _dot(jnp.tanh(wl).astype(BF16), w2_ref[...])
    logw = (-math.exp(-0.5)) * jax.nn.sigmoid(z)
    a = jax.nn.sigmoid(a0_ref[...] + _dot(al.astype(BF16), a2_ref[...]))

    ones_bd = ones_ref[...]
    kk = k * kk_ref[...]
    kk = kk * lax.rsqrt(_seg_sum(kk * kk, ones_bd) + 1e-12)
    ka = ka_ref[...]
    v_out[...] = v.astype(BF16)

    ke_sum = None
    for d, tri_ref, (rt_out, kt_out, bt_out, kb_out, pc_out) in (
            (0, trif_ref, (rt0_out, kt0_out, bt0_out, kb0_out, pc0_out)),
            (1, trir_ref, (rt1_out, kt1_out, bt1_out, kb1_out, pc1_out))):
        lw = logw[:, d * w:(d + 1) * w]
        a_d = a[:, d * w:(d + 1) * w]
        lw_hi, lw_lo = _split2(lw)
        cum = _dot(tri_ref[...], lw_hi) + _dot(tri_ref[...], lw_lo)
        tot = _dot(onesc_ref[...], lw_hi) + _dot(onesc_ref[...], lw_lo)
        e_neg = jnp.exp(-cum)
        ke = k * (1.0 + (a_d - 1.0) * ka)
        ke_sum = ke if ke_sum is None else ke_sum + ke
        rt_out[...] = (r * jnp.exp(cum)).astype(BF16)
        kt_out[...] = (kk * jnp.exp(cum - lw)).astype(BF16)
        bt_out[...] = _chunk_transpose(a_d * kk * e_neg).astype(BF16)
        kb_out[...] = _chunk_transpose(ke * e_neg).astype(BF16)
        pc_out[...] = _chunk_transpose(jnp.exp(tot))
    bon_out[...] = _seg_sum(r * ke_sum * rk_ref[...], ones_bd) * v


def _rwkv_prep(p_rw, mu, w2bd, w0, a2bd, a0, g2p, k_k, k_a, r_k, ones_bd, tri_f, tri_r, ones_c, *, n_lat, seq,
               ctx_len):
    nt = p_rw.shape[0]
    tm = TOKEN_TILE
    n_tiles = nt // tm
    halo = SUBLANES
    blocks8 = nt // halo
    const = lambda i: (0, 0)
    row = lambda i: (i, 0)
    w = RWKV_WIDTH
    o16 = jax.ShapeDtypeStruct((nt, w), BF16)
    o32 = jax.ShapeDtypeStruct((nt, w), F32)
    return pl.pallas_call(
        functools.partial(_rwkv_prep_kernel, lat_tiles=n_lat // tm, seq_tiles=seq // tm, ctx_tiles=ctx_len // tm),
        out_shape=(o16, o16, o16, o16, o16, o32, o16, o16, o16, o16, o32, o32, o32),
        grid=(n_tiles,),
        in_specs=[
            pl.BlockSpec((tm, RWKV_PAD), row),
            pl.BlockSpec((halo, RWKV_PAD), lambda i: (jnp.maximum(i * (tm // halo) - 1, 0), 0)),
            pl.BlockSpec((halo, RWKV_PAD), lambda i: (jnp.minimum((i + 1) * (tm // halo), blocks8 - 1), 0)),
            pl.BlockSpec((2, RWKV_PAD), const),
            pl.BlockSpec((2 * DECAY_LORA, 2 * w), const),
            pl.BlockSpec((1, 2 * w), const),
            pl.BlockSpec((2 * ICLR_LORA, 2 * w), const),
            pl.BlockSpec((1, 2 * w), const),
            pl.BlockSpec((GATE_PAD, w), const),
            pl.BlockSpec((1, w), const),
            pl.BlockSpec((1, w), const),
            pl.BlockSpec((1, w), const),
            pl.BlockSpec((w, w), const),
            pl.BlockSpec((tm, tm), const),
            pl.BlockSpec((tm, tm), const),
            pl.BlockSpec((tm, tm), const),
        ],
        out_specs=(pl.BlockSpec((tm, w), row),) * 13,
        compiler_params=_params(("parallel",)),
        name="rwkv_prep",
    )(p_rw, p_rw, p_rw, mu, w2bd, w0, a2bd, a0, g2p, k_k, k_a, r_k, ones_bd, tri_f, tri_r, ones_c)


HGROUP = 2
HGROUP_W = HGROUP * RWKV_HEAD
SCAN_CHUNKS = 4


def _scan_kernel(v0_ref, rt0_ref, kt0_ref, bt0_ref, kb0_ref, pc0_ref, v1_ref, rt1_ref, kt1_ref, bt1_ref, kb1_ref,
                 pc1_ref, y0_ref, y1_ref, h_ref):
    @pl.when(pl.program_id(1) == 0)
    def _():
        h_ref[...] = jnp.zeros_like(h_ref)

    n, qw = CHUNK, HGROUP_W
    t_i = lax.broadcasted_iota(jnp.int32, (n, qw), 0)
    s_i = lax.broadcasted_iota(jnp.int32, (n, qw), 1) % n
    bd_mask = (lax.broadcasted_iota(jnp.int32, (qw, qw), 0) // n) == (lax.broadcasted_iota(jnp.int32, (qw, qw), 1) // n)
    eye = jnp.where(t_i == s_i, 1.0, 0.0)
    same16 = (t_i // INV_DIAG) == (s_i // INV_DIAG)
    same32 = (t_i // (2 * INV_DIAG)) == (s_i // (2 * INV_DIAG))
    sib32 = jnp.logical_and(same32, jnp.logical_not(same16))
    sib64 = jnp.logical_not(same32)
    incl = (s_i <= t_i, s_i >= t_i)
    strict = (s_i < t_i, s_i > t_i)

    groups = [(d, q, j) for j in range(SCAN_CHUNKS) for d in (0, 1) for q in range(HEADS // HGROUP)]
    dirs = [d for d, _, _ in groups]
    ng = range(len(groups))
    row0 = lambda d, j: (SCAN_CHUNKS - 1 - j) * n if d else j * n
    refs = ((v0_ref, rt0_ref, kt0_ref, bt0_ref, kb0_ref, pc0_ref), (v1_ref, rt1_ref, kt1_ref, bt1_ref, kb1_ref, pc1_ref))
    ld = lambda k: [refs[d][k][row0(d, j):row0(d, j) + n, q * qw:(q + 1) * qw] for d, q, j in groups]
    v, rt, kt, bt_t, kb_t, pc = (ld(k) for k in range(6))

    def bd(x):
        return jnp.where(bd_mask, jnp.concatenate([x.astype(BF16)] * HGROUP, axis=0), jnp.zeros((), BF16))

    mm = lambda x, w: _dot(x.astype(BF16), w)
    rows2 = lambda x, y: jnp.concatenate([x.astype(BF16), y.astype(BF16)], axis=0)

    bd2 = lambda x, y: jnp.concatenate([bd(x), bd(y)], axis=1)
    gram = [_dot(rows2(kt[g], rt[g]), bd2(bt_t[g], kb_t[g])) for g in ng]
    a_m = [jnp.where(strict[dirs[g]], gram[g][0:n, 0:qw], 0.0) for g in ng]
    c_m = [jnp.where(incl[dirs[g]], gram[g][n:2 * n, 0:qw], 0.0) for g in ng]
    b_m = [jnp.where(strict[dirs[g]], gram[g][0:n, qw:2 * qw], 0.0) for g in ng]
    e_m = [jnp.where(incl[dirs[g]], gram[g][n:2 * n, qw:2 * qw], 0.0) for g in ng]

    dd = [jnp.where(same16, a_m[g], 0.0) for g in ng]
    pw = [mm(dd[g], bd(dd[g])) for g in ng]
    t = [eye - dd[g] for g in ng]
    for _ in range(2):
        both = [_dot(rows2(t[g], pw[g]), bd(pw[g])) for g in ng]
        t = [t[g] + both[g][0:n] for g in ng]
        pw = [both[g][n:2 * n] for g in ng]
    t = [t[g] + mm(t[g], bd(pw[g])) for g in ng]
    for sib in (sib32, sib64):
        ta = [mm(t[g], bd(jnp.where(sib, a_m[g], 0.0))) for g in ng]
        t = [t[g] - mm(ta[g], bd(t[g])) for g in ng]

    bev = [_dot(jnp.concatenate([rows2(b_m[g], e_m[g]), kb_t[g]], axis=0), bd(v[g])) for g in ng]
    kv = [bev[g][2 * n:3 * n] for g in ng]
    mw = [mm(t[g], bd2(kt[g], bev[g][0:n])) for g in ng]
    m1 = [mw[g][:, 0:qw] for g in ng]
    w2 = [mw[g][:, qw:2 * qw] for g in ng]
    cq = [mm(c_m[g], bd2(m1[g], w2[g])) for g in ng]
    qh = [rt[g].astype(F32) - cq[g][:, 0:qw] for g in ng]
    y0 = [bev[g][n:2 * n] - cq[g][:, qw:2 * qw] for g in ng]

    chains = len(groups) // SCAN_CHUNKS
    h = [h_ref[s] for s in range(chains)]
    for g, (d, q, j) in enumerate(groups):
        s = g % chains
        uy = _dot(rows2(m1[g], qh[g]), bd(h[s]))
        (y0_ref, y1_ref)[d][row0(d, j):row0(d, j) + n, q * qw:(q + 1) * qw] = uy[n:2 * n] + y0[g]
        h[s] = pc[g] * (h[s] - _dot(bt_t[g], bd(uy[0:n] + w2[g])) + kv[g])
    for s in range(chains):
        h_ref[s] = h[s]


def _rwkv_scan(v, ops0, ops1, *, n_batch, seq, ctx_len):
    nt = v.shape[0]
    n = CHUNK * SCAN_CHUNKS
    assert seq % n == 0 and ctx_len % n == 0
    lat_c = seq // n
    ctx_c = ctx_len // n
    ctx0 = n_batch * lat_c

    def idx_f(bi, c):
        return (jnp.where(c < ctx_c, ctx0 + bi * ctx_c + c, bi * lat_c + (c - ctx_c)), 0)

    def idx_r(bi, c):
        return (jnp.where(c < ctx_c, ctx0 + bi * ctx_c + (ctx_c - 1 - c), bi * lat_c + (lat_c - 1 - (c - ctx_c))), 0)

    spec_f = pl.BlockSpec((n, RWKV_WIDTH), idx_f)
    spec_r = pl.BlockSpec((n, RWKV_WIDTH), idx_r)
    out = jax.ShapeDtypeStruct((nt, RWKV_WIDTH), F32)
    return pl.pallas_call(
        _scan_kernel,
        out_shape=(out, out),
        grid=(n_batch, ctx_c + lat_c),
        in_specs=[spec_f] * 6 + [spec_r] * 6,
        out_specs=(spec_f, spec_r),
        scratch_shapes=[pltpu.VMEM((2 * HEADS // HGROUP, CHUNK, HGROUP_W), F32)],
        compiler_params=_params(("parallel", "arbitrary")),
        name="rwkv_scan",
    )(v, *ops0, v, *ops1)


def _mix_out_kernel(*refs, lat_tiles):
    (yf_ref, yb_ref, g_ref, bon_ref, lng_ref, lnb_ref, ones_ref, woa_ref, wor_ref, xl_ref, xc_ref, mod_ref, pg_ref,
     att_ref) = refs[:14]
    o_ref = refs[-1]
    is_lat = pl.program_id(0) < lat_tiles
    ones_bd = ones_ref[...]
    y = yf_ref[...] + yb_ref[...]
    inv_n = 1.0 / RWKV_HEAD
    mu = _seg_sum(y, ones_bd) * inv_n
    yc = y - mu
    var = _seg_sum(yc * yc, ones_bd) * inv_n
    yn = yc * lax.rsqrt(var + LNX_EPS) * lng_ref[...] + lnb_ref[...]
    rw = (yn + bon_ref[...]) * g_ref[...]
    att = att_ref[...]
    if len(refs) == 16:
        att = jnp.where(is_lat, att, refs[14][...])
    mix = _dot(att, woa_ref[...]) + _dot(rw.astype(BF16), wor_ref[...])
    x = jnp.where(is_lat, xl_ref[...], xc_ref[...])
    o_ref[...] = x + mod_ref[0, 2:3, :] * _rms(mix, pg_ref[...])


def _mix_out(yf, yb, att_lat, att_ctx, g, bon, lnx_g, lnx_b, ones_bd, wo_att, wo_rw, x_lat, x_ctx, mods, post_g, *,
             rows, n_lat, seq):
    tm = TOKEN_TILE
    lat_tiles = n_lat // tm
    n_batch = n_lat // seq
    w = RWKV_WIDTH

    def mod_idx(i):
        return (jnp.where(i < lat_tiles, (i * tm) // seq, n_batch), 0, 0)

    const = lambda i: (0, 0)
    row = lambda i: (i, 0)
    in_specs = [
        pl.BlockSpec((tm, w), row),
        pl.BlockSpec((tm, w), row),
        pl.BlockSpec((tm, w), row),
        pl.BlockSpec((tm, w), row),
        pl.BlockSpec((1, w), const),
        pl.BlockSpec((1, w), const),
        pl.BlockSpec((w, w), const),
        pl.BlockSpec((w, D_MODEL), const),
        pl.BlockSpec((w, D_MODEL), const),
        *_stream_specs(tm, lat_tiles, x_ctx is x_lat),
        pl.BlockSpec((1, N_MOD, D_MODEL), mod_idx),
        pl.BlockSpec((1, D_MODEL), const),
        pl.BlockSpec((tm, w), lambda i: (jnp.minimum(i, lat_tiles - 1), 0)),
    ]
    args = [yf, yb, g, bon, lnx_g, lnx_b, ones_bd, wo_att, wo_rw, x_lat, x_ctx, mods, post_g, att_lat]
    if att_ctx is not None:
        in_specs.append(pl.BlockSpec((tm, w), lambda i: (jnp.maximum(i - lat_tiles, 0), 0)))
        args.append(att_ctx)
    return pl.pallas_call(
        functools.partial(_mix_out_kernel, lat_tiles=lat_tiles),
        out_shape=jax.ShapeDtypeStruct((rows, D_MODEL), F32),
        grid=(rows // tm,),
        in_specs=in_specs,
        out_specs=pl.BlockSpec((tm, D_MODEL), row),
        compiler_params=_params(("parallel",)),
        name="mix_out",
    )(*args)


def _route(logits_t, bias_col):
    scores = jax.nn.sigmoid(logits_t[0:N_EXPERTS, :])
    biased = scores + bias_col
    s_rows = [scores[e:e + 1, :] for e in range(N_EXPERTS)]
    b_rows = [biased[e:e + 1, :] for e in range(N_EXPERTS)]
    npg = EXPERTS_PER_GROUP
    group_scores = []
    for gi in range(N_GROUPS):
        bg = b_rows[gi * npg:(gi + 1) * npg]
        best_pair = None
        for i in range(npg):
            for j in range(i + 1, npg):
                pair = bg[i] + bg[j]
                best_pair = pair if best_pair is None else jnp.maximum(best_pair, pair)
        group_scores.append(best_pair)
    best = group_scores[0]
    best_idx = jnp.zeros(best.shape, jnp.int32)
    for gi in range(1, N_GROUPS):
        upd = group_scores[gi] > best
        best = jnp.where(upd, group_scores[gi], best)
        best_idx = jnp.where(upd, gi, best_idx)
    pick = lambda rows, j: functools.reduce(
        lambda acc, gi: jnp.where(best_idx == gi, rows[gi * npg + j], acc), range(1, N_GROUPS), rows[j])
    bb = [pick(b_rows, j) for j in range(npg)]
    ss = [pick(s_rows, j) for j in range(npg)]
    weights = []
    for j in range(npg):
        rank = jnp.zeros(best.shape, jnp.int32)
        for i in range(npg):
            if i == j:
                continue
            beats = (bb[i] > bb[j]) | ((bb[i] == bb[j]) & (i < j)) if i < j else (bb[i] > bb[j])
            rank = rank + beats.astype(jnp.int32)
        weights.append(jnp.where(rank < 2, ss[j], 0.0))
    den = weights[0] + weights[1] + weights[2] + weights[3]
    return [wj / den for wj in weights], best_idx


def _swiglu(h_b, wgu, wd, gate=None):
    gu = _dot(h_b, wgu)
    g_part = gu[:, 0:EXPERT_HIDDEN]
    act = g_part * jax.nn.sigmoid(g_part) * gu[:, EXPERT_HIDDEN:]
    if gate is not None:
        act = act * gate
    return _dot(act.astype(BF16), wd)


def _moe_kernel(x_ref, mod_ref, pre_ref, post_ref, rw_ref, rb_ref, before_ref, wgu_ref, wd_ref, o_ref, sorted_scr):
    tm = x_ref.shape[0]
    blk = MOE_BLOCK
    n_blocks = sorted_scr.shape[0] // blk
    x = x_ref[...]
    h = _rms(x, pre_ref[...]) * (1.0 + mod_ref[0, 4:5, :]) + mod_ref[0, 3:4, :]
    h_hi = h.astype(BF16)
    h_lo = (h - h_hi.astype(F32)).astype(BF16)
    rw = rw_ref[...]
    rw_hi = rw.astype(BF16)
    rw_lo = (rw - rw_hi.astype(F32)).astype(BF16)
    logits_t = _dot_nt(rw_hi, h_hi) + _dot_nt(rw_lo, h_hi) + _dot_nt(rw_hi, h_lo)
    gates, group = _route(logits_t, rb_ref[...])

    sub8 = lax.broadcasted_iota(jnp.int32, (SUBLANES, tm), 0)
    onehot = jnp.where(sub8 == group, 1.0, 0.0)
    before = _dot(onehot.astype(BF16), before_ref[...])
    counts = [jnp.sum(onehot[g:g + 1, :]).astype(jnp.int32) for g in range(N_GROUPS)]
    ends, acc = [], jnp.int32(0)
    for g in range(N_GROUPS):
        acc = acc + ((counts[g] + (blk - 1)) // blk) * blk
        ends.append(acc)
    starts = [jnp.int32(0)] + ends[:-1]
    pos = sum(onehot[g:g + 1, :] * (before[g:g + 1, :] + starts[g].astype(F32)) for g in range(N_GROUPS))

    sub = lax.broadcasted_iota(jnp.int32, (LANES, tm), 0)
    stack = jnp.where(sub == EXPERTS_PER_GROUP, pos, 0.0)
    for j in range(EXPERTS_PER_GROUP):
        stack = jnp.where(sub == j, gates[j], stack)
    cols = stack.T
    g_hi = cols.astype(BF16)
    g_lo = (cols - g_hi.astype(F32)).astype(BF16)
    pos_row = pos.astype(jnp.int32)
    pos_col = cols[:, EXPERTS_PER_GROUP:EXPERTS_PER_GROUP + 1].astype(jnp.int32)

    f_shared = _swiglu(h_hi, wgu_ref[N_EXPERTS], wd_ref[N_EXPERTS])

    for s in range(n_blocks):
        r0 = s * blk
        rows = pl.ds(r0, blk)
        grp = sum((r0 >= ends[g]).astype(jnp.int32) for g in range(N_GROUPS - 1))

        @pl.when(r0 < ends[-1])
        def _():
            r_i = lax.broadcasted_iota(jnp.int32, (blk, tm), 0) + r0
            take = jnp.where(r_i == pos_row, 1.0, 0.0).astype(BF16)
            h_s = _dot(take, h_hi).astype(BF16)
            g_s = _dot(take, g_hi) + _dot(take, g_lo)
            out = None
            for j in range(EXPERTS_PER_GROUP):
                e = grp * EXPERTS_PER_GROUP + j
                y = _swiglu(h_s, wgu_ref[e], wd_ref[e], g_s[:, j:j + 1])
                out = y if out is None else out + y
            sorted_scr[rows, :] = out.astype(BF16)

        @pl.when(r0 >= ends[-1])
        def _():
            sorted_scr[rows, :] = jnp.zeros((blk, D_MODEL), BF16)

    c_i = lax.broadcasted_iota(jnp.int32, (tm, n_blocks * blk), 1)
    put = jnp.where(c_i == pos_col, 1.0, 0.0).astype(BF16)
    f = f_shared + _dot(put, sorted_scr[...])
    o_ref[...] = x + mod_ref[0, 5:6, :] * _rms(f, post_ref[...])


def _moe(x_all, mods, pre_g, post_g, router_wt, router_b, before_m, wgu, wd, *, rows, n_lat, seq):
    tm = MOE_TILE
    lat_tiles = n_lat // tm
    n_batch = n_lat // seq
    n_e = wgu.shape[0]

    def mod_idx(i):
        return (jnp.where(i < lat_tiles, (i * tm) // seq, n_batch), 0, 0)

    const = lambda i: (0, 0)
    const3 = lambda i: (0, 0, 0)
    row = lambda i: (i, 0)
    resident = pl.Buffered(1)
    return pl.pallas_call(
        _moe_kernel,
        out_shape=jax.ShapeDtypeStruct((rows, D_MODEL), F32),
        grid=(rows // tm,),
        in_specs=[
            pl.BlockSpec((tm, D_MODEL), row),
            pl.BlockSpec((1, N_MOD, D_MODEL), mod_idx),
            pl.BlockSpec((1, D_MODEL), const),
            pl.BlockSpec((1, D_MODEL), const),
            pl.BlockSpec((LANES, D_MODEL), const),
            pl.BlockSpec((N_EXPERTS, 1), const),
            pl.BlockSpec((tm, tm), const),
            pl.BlockSpec((n_e, D_MODEL, 2 * EXPERT_HIDDEN), const3, pipeline_mode=resident),
            pl.BlockSpec((n_e, EXPERT_HIDDEN, D_MODEL), const3, pipeline_mode=resident),
        ],
        out_specs=pl.BlockSpec((tm, D_MODEL), row),
        scratch_shapes=[pltpu.VMEM((tm + N_GROUPS * MOE_BLOCK, D_MODEL), BF16)],
        compiler_params=_params(("parallel",)),
        name="moe",
    )(x_all, mods, pre_g, post_g, router_wt, router_b, before_m, wgu, wd)


def _pack_in_proj(w_in, shift_mu):
    d = w_in.shape[0]
    z = lambda n: jnp.zeros((d, n), w_in.dtype)
    kr = w_in[:, Q_RANK + KV_RANK:MLA_COLS]
    half = MLA_ROPE // 2
    kr_sw = jnp.concatenate([kr[:, half:], kr[:, :half]], axis=1)
    pad = LANES - MLA_NOPE - MLA_ROPE
    packed = jnp.concatenate([
        w_in[:, :Q_RANK + KV_RANK],
        z(MLA_NOPE), kr, z(pad),
        z(MLA_NOPE), kr_sw, z(pad),
        w_in[:, MLA_COLS:], z(RWKV_PAD - RWKV_COLS),
    ], axis=1)
    mu = jnp.pad(shift_mu, ((0, 0), (0, RWKV_PAD - RWKV_COLS)))
    return packed.astype(BF16), mu


def _pack_mla(w_uq, w_ukv):
    half = MLA_ROPE // 2
    dq = MLA_NOPE + MLA_ROPE
    q3 = w_uq.reshape(Q_RANK, HEADS, dq)
    zq = lambda n: jnp.zeros((Q_RANK, HEADS, n), w_uq.dtype)
    pad = HEAD_SLOT - dq
    wq1 = jnp.concatenate([q3, zq(pad)], axis=2).reshape(Q_RANK, HEADS * HEAD_SLOT)
    wq2 = jnp.concatenate([zq(MLA_NOPE), q3[:, :, MLA_NOPE + half:], q3[:, :, MLA_NOPE:MLA_NOPE + half], zq(pad)],
                          axis=2).reshape(Q_RANK, HEADS * HEAD_SLOT)
    kv3 = w_ukv.reshape(KV_RANK, HEADS, MLA_NOPE + MLA_V)
    wuk = jnp.concatenate([kv3[:, :, :MLA_NOPE], jnp.zeros((KV_RANK, HEADS, HEAD_SLOT - MLA_NOPE), w_ukv.dtype)],
                          axis=2).reshape(KV_RANK, HEADS * HEAD_SLOT)
    wuv = kv3[:, :, MLA_NOPE:].reshape(KV_RANK, HEADS * MLA_V).T
    return wq1.astype(BF16), wq2.astype(BF16), wuk.astype(BF16), wuv.astype(BF16)


def _block_diag2(m):
    r, c = m.shape[1], m.shape[2]
    z = jnp.zeros((r, c), m.dtype)
    return jnp.concatenate([jnp.concatenate([m[0], z], axis=1), jnp.concatenate([z, m[1]], axis=1)], axis=0)


def _rope_tables(seq, ctx_len):
    axis_dim = MLA_ROPE // 2
    t = jnp.arange(seq, dtype=jnp.int32)
    row = (t // GRID_W).astype(F32)
    col = (t % GRID_W).astype(F32)
    inv_freq = ROPE_BASE ** (-jnp.arange(0, axis_dim, 2, dtype=F32) / axis_dim)
    ang = jnp.concatenate([row[:, None] * inv_freq, col[:, None] * inv_freq], axis=-1)
    cos = jnp.concatenate([jnp.cos(ang), jnp.ones((ctx_len, axis_dim), F32)], axis=0)
    sin = jnp.concatenate([jnp.sin(ang), jnp.zeros((ctx_len, axis_dim), F32)], axis=0)
    n = seq + ctx_len
    ones = jnp.ones((n, MLA_NOPE), F32)
    z_nope = jnp.zeros((n, MLA_NOPE), F32)
    z_pad = jnp.zeros((n, HEAD_SLOT - MLA_NOPE - MLA_ROPE), F32)
    cq = jnp.concatenate([ones, cos, cos, z_pad], axis=1)
    sq = jnp.concatenate([z_nope, -sin, sin, z_pad], axis=1)
    ck = jnp.concatenate([z_nope, cos, cos, z_pad], axis=1)
    return cq, sq, ck


def kernel(x, c, ctx, c_ctx, ada_w, ada_b, mix_pre_g, mix_post_g, ffn_pre_g, ffn_post_g, w_in, q_norm_g, kv_norm_g,
           w_uq, w_ukv, shift_mu, decay_w0, decay_w2, iclr_a0, iclr_a2, gate_g2, k_k, k_a, r_k, lnx_g, lnx_b, w_out,
           router_w, router_bias, exp_w_gate, exp_w_up, exp_w_down, sh_w_gate, sh_w_up, sh_w_down):
    n_batch, seq, d = x.shape
    ctx_len = ctx.shape[1]
    depth = ada_w.shape[0]
    assert d == D_MODEL and seq % TOKEN_TILE == 0 and ctx_len % TOKEN_TILE == 0 and seq % ctx_len == 0
    assert seq % GRID_W == 0 and n_batch + 1 <= SUBLANES
    n_lat = n_batch * seq
    n_ctx = n_batch * ctx_len
    assert seq % MOE_TILE == 0 and n_ctx % MOE_TILE == 0

    x_lat, x_ctx = x.reshape(n_lat, d), ctx.reshape(n_ctx, d)
    cond_rows = jnp.concatenate([c, c_ctx[None], jnp.zeros((SUBLANES - n_batch - 1, d), F32)], axis=0)
    mods_all = _ada_modulation(cond_rows, ada_w, ada_b).reshape(depth, SUBLANES, N_MOD, d)

    cq_t, sq_t, ck_t = _rope_tables(seq, ctx_len)
    w = RWKV_WIDTH
    ones_bd = (jnp.arange(w)[:, None] // RWKV_HEAD == jnp.arange(w)[None, :] // RWKV_HEAD).astype(BF16)
    ti = jnp.arange(TOKEN_TILE)
    same_chunk = (ti[:, None] // CHUNK) == (ti[None, :] // CHUNK)
    ones_c = same_chunk.astype(BF16)
    tri_f = jnp.logical_and(same_chunk, ti[None, :] <= ti[:, None]).astype(BF16)
    tri_r = jnp.logical_and(same_chunk, ti[None, :] >= ti[:, None]).astype(BF16)
    router_wt = jnp.pad(router_w.T, ((0, LANES - N_EXPERTS), (0, 0)))
    router_b = router_bias.reshape(N_EXPERTS, 1)
    tj = jnp.arange(MOE_TILE)
    before_m = (tj[:, None] < tj[None, :]).astype(BF16)
    row1 = lambda a: a.reshape(1, -1)

    for l in range(depth):
        ctx_out = l < depth - 1
        mods = mods_all[l]
        rows = n_lat + n_ctx if ctx_out else n_lat

        w_in_p, mu = _pack_in_proj(w_in[l], shift_mu[l])
        wq1, wq2, wuk, wuv = _pack_mla(w_uq[l], w_ukv[l])
        q, k, v, p_rw = _in_proj(x_lat, x_ctx, mods, row1(mix_pre_g[l]), w_in_p, row1(q_norm_g[l]),
                                 row1(kv_norm_g[l]), wq1, wq2, wuk, wuv, cq_t, sq_t, ck_t, n_lat=n_lat, n_ctx=n_ctx,
                                 seq=seq, ctx_len=ctx_len)
        att = _attention(q, k, v, n_batch=n_batch, seq=seq, ctx_len=ctx_len, latent=True)
        att_c = _attention(q, k, v, n_batch=n_batch, seq=seq, ctx_len=ctx_len, latent=False) if ctx_out else None

        g2p = jnp.pad(gate_g2[l], ((0, GATE_PAD - GATE_LORA), (0, 0))).astype(BF16)
        prep = _rwkv_prep(
            p_rw, mu, _block_diag2(decay_w2[l]).astype(BF16), decay_w0[l].reshape(1, 2 * w),
            _block_diag2(iclr_a2[l]).astype(BF16), iclr_a0[l].reshape(1, 2 * w), g2p,
            row1(k_k[l]), row1(k_a[l]), row1(r_k[l]), ones_bd, tri_f, tri_r, ones_c,
            n_lat=n_lat, seq=seq, ctx_len=ctx_len)
        vv, g, bon = prep[0], prep[11], prep[12]
        yf, yb = _rwkv_scan(vv, prep[1:6], prep[6:11], n_batch=n_batch, seq=seq, ctx_len=ctx_len)

        wo = w_out[l].astype(BF16)
        x_all = _mix_out(yf, yb, att, att_c, g, bon, row1(lnx_g[l]), row1(lnx_b[l]), ones_bd, wo[:w], wo[w:], x_lat,
                         x_ctx, mods, row1(mix_post_g[l]), rows=rows, n_lat=n_lat, seq=seq)

        wgu = jnp.concatenate([
            jnp.concatenate([exp_w_gate[l], exp_w_up[l]], axis=2),
            jnp.concatenate([sh_w_gate[l], sh_w_up[l]], axis=1)[None]], axis=0).astype(BF16)
        wd = jnp.concatenate([exp_w_down[l], sh_w_down[l][None]], axis=0).astype(BF16)
        x_all = _moe(x_all, mods, row1(ffn_pre_g[l]), row1(ffn_post_g[l]), router_wt, router_b, before_m, wgu, wd,
                     rows=rows, n_lat=n_lat, seq=seq)
        x_lat = x_ctx = x_all

    return x_all.reshape(n_batch, seq, d)
```

```python
import functools
import math

import jax
import jax.numpy as jnp
from jax import lax
from jax.experimental import pallas as pl
from jax.experimental.pallas import tpu as pltpu

F32 = jnp.float32
BF16 = jnp.bfloat16
HIGHEST = lax.Precision.HIGHEST

D_MODEL = 1024
N_MOD = 6
NORM_EPS = 1e-6
GRID_W = 64
ROPE_BASE = 10000.0

HEADS = 8
MLA_NOPE = 64
MLA_ROPE = 32
MLA_V = 64
Q_RANK = 256
KV_RANK = 128
MLA_COLS = Q_RANK + KV_RANK + MLA_ROPE

RWKV_HEAD = 64
RWKV_WIDTH = HEADS * RWKV_HEAD
DECAY_LORA = 64
ICLR_LORA = 64
GATE_LORA = 160
RWKV_COLS = 3 * RWKV_WIDTH + 2 * DECAY_LORA + 2 * ICLR_LORA + GATE_LORA
RWKV_PAD = 2048
GATE_PAD = 256
LNX_EPS = 64e-5

N_EXPERTS = 16
N_GROUPS = 4
EXPERTS_PER_GROUP = 4
EXPERT_HIDDEN = 256

LANES = 128
SUBLANES = 8
HEAD_SLOT = 128
IN_PACKED = Q_RANK + KV_RANK + 2 * LANES + RWKV_PAD
TOKEN_TILE = 256
ATTN_KV_BLOCK = 2048
ATTN_Q_TILE = 512
MOE_TILE = 512
MOE_BLOCK = 128
MOE_MAIN_ROWS = MOE_TILE + 2 * MOE_BLOCK
CHUNK = 64
INV_DIAG = 16
ADA_TILE = 1536
VMEM_LIMIT = 48 * 1024 * 1024


def _dot(a, b):
    return jnp.dot(a, b, preferred_element_type=F32)


def _dot_nt(a, b):
    return lax.dot_general(a, b, (((1,), (1,)), ((), ())), preferred_element_type=F32)


def _rms(x, g):
    return x * lax.rsqrt(jnp.mean(x * x, axis=-1, keepdims=True) + NORM_EPS) * g


def _seg_sum(x, ones_bd):
    hi = x.astype(BF16)
    lo = (x - hi.astype(F32)).astype(BF16)
    return _dot(hi, ones_bd) + _dot(lo, ones_bd)


def _params(sem):
    return pltpu.CompilerParams(dimension_semantics=sem, vmem_limit_bytes=VMEM_LIMIT)


def _ada_kernel(c_ref, w_ref, b_ref, o_ref):
    c = c_ref[...]
    cond = c * jax.nn.sigmoid(c)
    o_ref[0] = jnp.dot(cond, w_ref[0], precision=HIGHEST, preferred_element_type=F32) + b_ref[0]


def _ada_modulation(cond_rows, ada_w, ada_b):
    depth, d, n = ada_w.shape
    tn = ADA_TILE
    rows = cond_rows.shape[0]
    return pl.pallas_call(
        _ada_kernel,
        out_shape=jax.ShapeDtypeStruct((depth, rows, n), F32),
        grid=(depth, n // tn),
        in_specs=[
            pl.BlockSpec((rows, d), lambda l, j: (0, 0)),
            pl.BlockSpec((1, d, tn), lambda l, j: (l, 0, j)),
            pl.BlockSpec((1, 1, tn), lambda l, j: (l, 0, j)),
        ],
        out_specs=pl.BlockSpec((1, rows, tn), lambda l, j: (l, 0, j)),
        compiler_params=_params(("parallel", "parallel")),
        name="ada_modulation",
    )(cond_rows, ada_w, ada_b.reshape(depth, 1, n))


def _stream_specs(tm, lat_tiles, merged):
    off = lat_tiles if merged else 0
    return (pl.BlockSpec((tm, D_MODEL), lambda i: (jnp.minimum(i, lat_tiles - 1), 0)),
            pl.BlockSpec((tm, D_MODEL), lambda i: (jnp.maximum(i - lat_tiles, 0) + off, 0)))


def _in_proj_kernel(xl_ref, xc_ref, mod_ref, g_ref, win_ref, qng_ref, kvng_ref, wq1_ref, wq2_ref, wuk_ref, wuv_ref,
                    cq_ref, sq_ref, ck_ref, q_out, k_out, v_out, p_out, *, q_scale, lat_tiles):
    x = jnp.where(pl.program_id(0) < lat_tiles, xl_ref[...], xc_ref[...])
    shift = mod_ref[0, 0:1, :]
    scale = mod_ref[0, 1:2, :]
    h = _rms(x, g_ref[...]) * (1.0 + scale) + shift
    p = _dot(h.astype(BF16), win_ref[...])
    c_q = p[:, 0:Q_RANK]
    c_kv = p[:, Q_RANK:Q_RANK + KV_RANK]
    kr_a = p[:, Q_RANK + KV_RANK:Q_RANK + KV_RANK + LANES]
    kr_b = p[:, Q_RANK + KV_RANK + LANES:Q_RANK + KV_RANK + 2 * LANES]
    p_out[...] = p[:, Q_RANK + KV_RANK + 2 * LANES:]

    tile8 = lambda t: jnp.concatenate([t] * HEADS, axis=1)
    cqn = _rms(c_q, qng_ref[...]).astype(BF16)
    q = _dot(cqn, wq1_ref[...]) * tile8(cq_ref[...]) + _dot(cqn, wq2_ref[...]) * tile8(sq_ref[...])
    q_out[...] = (q * q_scale).astype(BF16)

    ckvn = _rms(c_kv, kvng_ref[...]).astype(BF16)
    k_rot = kr_a * ck_ref[...] + kr_b * sq_ref[...]
    k_out[...] = (_dot(ckvn, wuk_ref[...]) + tile8(k_rot)).astype(BF16)
    v_out[...] = _dot_nt(wuv_ref[...], ckvn).astype(BF16)


def _in_proj(x_lat, x_ctx, mods, g, w_in_p, qng, kvng, wq1, wq2, wuk, wuv, cq_t, sq_t, ck_t, *, n_lat, n_ctx, seq,
             ctx_len):
    nt = n_lat + n_ctx
    tm = TOKEN_TILE
    lat_tiles = n_lat // tm
    n_batch = n_lat // seq

    def mod_idx(i):
        return (jnp.where(i < lat_tiles, (i * tm) // seq, n_batch), 0, 0)

    def tab_idx(i):
        return (jnp.where(i < lat_tiles, i % (seq // tm), seq // tm + (i - lat_tiles) % (ctx_len // tm)), 0)

    const = lambda i: (0, 0)
    row = lambda i: (i, 0)
    qw = HEADS * HEAD_SLOT
    return pl.pallas_call(
        functools.partial(_in_proj_kernel, q_scale=float((MLA_NOPE + MLA_ROPE) ** -0.5 * math.log2(math.e)),
                          lat_tiles=lat_tiles),
        out_shape=(
            jax.ShapeDtypeStruct((nt, qw), BF16),
            jax.ShapeDtypeStruct((nt, qw), BF16),
            jax.ShapeDtypeStruct((HEADS * MLA_V, nt), BF16),
            jax.ShapeDtypeStruct((nt, RWKV_PAD), F32),
        ),
        grid=(nt // tm,),
        in_specs=[
            *_stream_specs(tm, lat_tiles, x_ctx is x_lat),
            pl.BlockSpec((1, N_MOD, D_MODEL), mod_idx),
            pl.BlockSpec((1, D_MODEL), const),
            pl.BlockSpec((D_MODEL, IN_PACKED), const),
            pl.BlockSpec((1, Q_RANK), const),
            pl.BlockSpec((1, KV_RANK), const),
            pl.BlockSpec((Q_RANK, qw), const),
            pl.BlockSpec((Q_RANK, qw), const),
            pl.BlockSpec((KV_RANK, qw), const),
            pl.BlockSpec((HEADS * MLA_V, KV_RANK), const),
            pl.BlockSpec((tm, HEAD_SLOT), tab_idx),
            pl.BlockSpec((tm, HEAD_SLOT), tab_idx),
            pl.BlockSpec((tm, HEAD_SLOT), tab_idx),
        ],
        out_specs=(
            pl.BlockSpec((tm, qw), row),
            pl.BlockSpec((tm, qw), row),
            pl.BlockSpec((HEADS * MLA_V, tm), lambda i: (0, i)),
            pl.BlockSpec((tm, RWKV_PAD), row),
        ),
        compiler_params=_params(("parallel",)),
        name="in_proj",
    )(x_lat, x_ctx, mods, g, w_in_p, qng, kvng, wq1, wq2, wuk, wuv, cq_t, sq_t, ck_t)


def _attn_kernel(q_ref, kc_ref, vc_ref, *rest):
    o_ref = rest[-1]
    blocks = [(kc_ref, vc_ref, 0, kc_ref.shape[0])]
    if len(rest) == 3:
        kl_ref, vl_ref = rest[0], rest[1]
        kb = min(ATTN_KV_BLOCK, kl_ref.shape[0])
        blocks += [(kl_ref, vl_ref, s0, kb) for s0 in range(0, kl_ref.shape[0], kb)]
    heads = range(2)
    hs = [slice(h * HEAD_SLOT, (h + 1) * HEAD_SLOT) for h in heads]
    vs = [slice(h * MLA_V, (h + 1) * MLA_V) for h in heads]
    q = [q_ref[:, hs[h]] for h in heads]

    def scores(j):
        k_ref, _, s0, size = blocks[j]
        return [_dot_nt(k_ref[s0:s0 + size, hs[h]], q[h]) for h in heads]

    m, den, acc = [None] * 2, [None] * 2, [None] * 2
    s_cur = scores(0)
    for j, (_, vt_ref, s0, size) in enumerate(blocks):
        s_next = scores(j + 1) if j + 1 < len(blocks) else None
        for h in heads:
            s = s_cur[h]
            m_blk = jnp.max(s, axis=0, keepdims=True)
            if j == 0:
                m[h] = m_blk
                p = jnp.exp2(s - m_blk)
                den[h] = jnp.sum(p, axis=0, keepdims=True)
                acc[h] = _dot(vt_ref[vs[h], s0:s0 + size], p.astype(BF16))
            else:
                m_new = jnp.maximum(m[h], m_blk)
                alpha = jnp.exp2(m[h] - m_new)
                p = jnp.exp2(s - m_new)
                den[h] = alpha * den[h] + jnp.sum(p, axis=0, keepdims=True)
                acc[h] = alpha * acc[h] + _dot(vt_ref[vs[h], s0:s0 + size], p.astype(BF16))
                m[h] = m_new
        s_cur = s_next
    out_t = jnp.concatenate([acc[h] / den[h] for h in heads], axis=0)
    o_ref[...] = out_t.T.astype(o_ref.dtype)


def _attention(q, k, v_t, *, n_batch, seq, ctx_len, latent):
    tq = ATTN_Q_TILE if latent else TOKEN_TILE
    seg = seq if latent else ctx_len
    assert seg % tq == 0
    q_tiles = seg // tq
    q_blk0 = 0 if latent else (n_batch * seq) // tq
    ctx_blk0 = (n_batch * seq) // ctx_len
    kv_ctx = lambda b, hp, j: (ctx_blk0 + b, hp)
    kv_lat = lambda b, hp, j: (b, hp)
    in_specs = [
        pl.BlockSpec((tq, 2 * HEAD_SLOT), lambda b, hp, j: (q_blk0 + b * q_tiles + j, hp)),
        pl.BlockSpec((ctx_len, 2 * HEAD_SLOT), kv_ctx),
        pl.BlockSpec((2 * MLA_V, ctx_len), lambda b, hp, j: (hp, ctx_blk0 + b)),
    ]
    args = [q, k, v_t]
    if latent:
        in_specs += [pl.BlockSpec((seq, 2 * HEAD_SLOT), kv_lat),
                     pl.BlockSpec((2 * MLA_V, seq), lambda b, hp, j: (hp, b))]
        args += [k, v_t]
    return pl.pallas_call(
        _attn_kernel,
        out_shape=jax.ShapeDtypeStruct((n_batch * seg, HEADS * MLA_V), BF16),
        grid=(n_batch, HEADS // 2, q_tiles),
        in_specs=in_specs,
        out_specs=pl.BlockSpec((tq, 2 * MLA_V), lambda b, hp, j: (b * q_tiles + j, hp)),
        compiler_params=_params(("parallel", "parallel", "arbitrary")),
        name="attention_lat" if latent else "attention_ctx",
    )(*args)


def _split2(x):
    hi = x.astype(BF16)
    return hi, (x - hi.astype(F32)).astype(BF16)


def _chunk_transpose(x):
    n = CHUNK
    xt = x.T
    rows = []
    for c in range(x.shape[0] // n):
        rows.append(jnp.concatenate(
            [xt[h * n:(h + 1) * n, c * n:(c + 1) * n] for h in range(x.shape[1] // n)], axis=1))
    return jnp.concatenate(rows, axis=0)


def _rwkv_prep_kernel(p_ref, hp_ref, hn_ref, mu_ref, w2_ref, w0_ref, a2_ref, a0_ref, g2_ref, kk_ref, ka_ref, rk_ref,
                      ones_ref, trif_ref, trir_ref, onesc_ref, v_out, rt0_out, kt0_out, bt0_out, kb0_out, pc0_out,
                      rt1_out, kt1_out, bt1_out, kb1_out, pc1_out, g_out, bon_out, *, lat_tiles, seq_tiles,
                      ctx_tiles):
    i = pl.program_id(0)
    tm = p_ref.shape[0]
    is_lat = i < lat_tiles
    local = jnp.where(is_lat, i % seq_tiles, (i - lat_tiles) % ctx_tiles)
    seg = jnp.where(is_lat, seq_tiles, ctx_tiles)
    keep_prev = jnp.where(local == 0, 0.0, 1.0)
    keep_next = jnp.where(local == seg - 1, 0.0, 1.0)

    p = p_ref[...]
    row = lax.broadcasted_iota(jnp.int32, (tm, 1), 0)
    prev = jnp.where(row == 0, hp_ref[7:8, :] * keep_prev, pltpu.roll(p, 1, 0))
    nxt = jnp.where(row == tm - 1, hn_ref[0:1, :] * keep_next, pltpu.roll(p, tm - 1, 0))
    ps = p + mu_ref[0:1, :] * (prev - p) + mu_ref[1:2, :] * (nxt - p)

    w = RWKV_WIDTH
    r = ps[:, 0:w]
    k = ps[:, w:2 * w]
    v = ps[:, 2 * w:3 * w]
    wl = ps[:, 3 * w:3 * w + 2 * DECAY_LORA]
    al = ps[:, 3 * w + 2 * DECAY_LORA:3 * w + 2 * DECAY_LORA + 2 * ICLR_LORA]
    gl = ps[:, 3 * w + 2 * DECAY_LORA + 2 * ICLR_LORA:3 * w + 2 * DECAY_LORA + 2 * ICLR_LORA + GATE_PAD]

    g_out[...] = _dot(jax.nn.sigmoid(gl).astype(BF16), g2_ref[...])
    z = w0_ref[...] + _dot(jnp.tanh(wl).astype(BF16), w2_ref[...])
    logw = (-math.exp(-0.5) * math.log2(math.e)) * jax.nn.sigmoid(z)
    a = jax.nn.sigmoid(a0_ref[...] + _dot(al.astype(BF16), a2_ref[...]))

    ones_bd = ones_ref[...]
    kk = k * kk_ref[...]
    kk = kk * lax.rsqrt(_seg_sum(kk * kk, ones_bd) + 1e-12)
    ka = ka_ref[...]
    v_out[...] = v.astype(BF16)

    ke_sum = None
    for d, tri_ref, (rt_out, kt_out, bt_out, kb_out, pc_out) in (
            (0, trif_ref, (rt0_out, kt0_out, bt0_out, kb0_out, pc0_out)),
            (1, trir_ref, (rt1_out, kt1_out, bt1_out, kb1_out, pc1_out))):
        lw = logw[:, d * w:(d + 1) * w]
        a_d = a[:, d * w:(d + 1) * w]
        lw_hi, lw_lo = _split2(lw)
        cum = _dot(tri_ref[...], lw_hi) + _dot(tri_ref[...], lw_lo)
        tot = _dot(onesc_ref[...], lw_hi) + _dot(onesc_ref[...], lw_lo)
        e_neg = jnp.exp2(-cum)
        ke = k * (1.0 + (a_d - 1.0) * ka)
        ke_sum = ke if ke_sum is None else ke_sum + ke
        rt_out[...] = (r * jnp.exp2(cum)).astype(BF16)
        kt_out[...] = (kk * jnp.exp2(cum - lw)).astype(BF16)
        bt_out[...] = _chunk_transpose(a_d * kk * e_neg).astype(BF16)
        kb_out[...] = _chunk_transpose(ke * e_neg).astype(BF16)
        pc_out[...] = _chunk_transpose(jnp.exp2(tot))
    bon_out[...] = _seg_sum(r * ke_sum * rk_ref[...], ones_bd) * v


def _rwkv_prep(p_rw, mu, w2bd, w0, a2bd, a0, g2p, k_k, k_a, r_k, ones_bd, tri_f, tri_r, ones_c, *, n_lat, seq,
               ctx_len):
    nt = p_rw.shape[0]
    tm = TOKEN_TILE
    n_tiles = nt // tm
    halo = SUBLANES
    blocks8 = nt // halo
    const = lambda i: (0, 0)
    row = lambda i: (i, 0)
    w = RWKV_WIDTH
    o16 = jax.ShapeDtypeStruct((nt, w), BF16)
    o32 = jax.ShapeDtypeStruct((nt, w), F32)
    return pl.pallas_call(
        functools.partial(_rwkv_prep_kernel, lat_tiles=n_lat // tm, seq_tiles=seq // tm, ctx_tiles=ctx_len // tm),
        out_shape=(o16, o16, o16, o16, o16, o32, o16, o16, o16, o16, o32, o32, o32),
        grid=(n_tiles,),
        in_specs=[
            pl.BlockSpec((tm, RWKV_PAD), row),
            pl.BlockSpec((halo, RWKV_PAD), lambda i: (jnp.maximum(i * (tm // halo) - 1, 0), 0)),
            pl.BlockSpec((halo, RWKV_PAD), lambda i: (jnp.minimum((i + 1) * (tm // halo), blocks8 - 1), 0)),
            pl.BlockSpec((2, RWKV_PAD), const),
            pl.BlockSpec((2 * DECAY_LORA, 2 * w), const),
            pl.BlockSpec((1, 2 * w), const),
            pl.BlockSpec((2 * ICLR_LORA, 2 * w), const),
            pl.BlockSpec((1, 2 * w), const),
            pl.BlockSpec((GATE_PAD, w), const),
            pl.BlockSpec((1, w), const),
            pl.BlockSpec((1, w), const),
            pl.BlockSpec((1, w), const),
            pl.BlockSpec((w, w), const),
            pl.BlockSpec((tm, tm), const),
            pl.BlockSpec((tm, tm), const),
            pl.BlockSpec((tm, tm), const),
        ],
        out_specs=(pl.BlockSpec((tm, w), row),) * 13,
        compiler_params=_params(("parallel",)),
        name="rwkv_prep",
    )(p_rw, p_rw, p_rw, mu, w2bd, w0, a2bd, a0, g2p, k_k, k_a, r_k, ones_bd, tri_f, tri_r, ones_c)


HGROUP = 2
HGROUP_W = HGROUP * RWKV_HEAD
SCAN_CHUNKS = 4


def _scan_kernel(v0_ref, rt0_ref, kt0_ref, bt0_ref, kb0_ref, pc0_ref, v1_ref, rt1_ref, kt1_ref, bt1_ref, kb1_ref,
                 pc1_ref, y0_ref, y1_ref, h_ref):
    @pl.when(pl.program_id(1) == 0)
    def _():
        h_ref[...] = jnp.zeros_like(h_ref)

    n, qw = CHUNK, HGROUP_W
    t_i = lax.broadcasted_iota(jnp.int32, (n, qw), 0)
    s_i = lax.broadcasted_iota(jnp.int32, (n, qw), 1) % n
    bd_mask = (lax.broadcasted_iota(jnp.int32, (qw, qw), 0) // n) == (lax.broadcasted_iota(jnp.int32, (qw, qw), 1) // n)
    eye = jnp.where(t_i == s_i, 1.0, 0.0)
    same16 = (t_i // INV_DIAG) == (s_i // INV_DIAG)
    same32 = (t_i // (2 * INV_DIAG)) == (s_i // (2 * INV_DIAG))
    sib32 = jnp.logical_and(same32, jnp.logical_not(same16))
    sib64 = jnp.logical_not(same32)
    incl = (s_i <= t_i, s_i >= t_i)
    strict = (s_i < t_i, s_i > t_i)

    groups = [(d, q, j) for j in range(SCAN_CHUNKS) for d in (0, 1) for q in range(HEADS // HGROUP)]
    dirs = [d for d, _, _ in groups]
    ng = range(len(groups))
    row0 = lambda d, j: (SCAN_CHUNKS - 1 - j) * n if d else j * n
    refs = ((v0_ref, rt0_ref, kt0_ref, bt0_ref, kb0_ref, pc0_ref), (v1_ref, rt1_ref, kt1_ref, bt1_ref, kb1_ref, pc1_ref))
    ld = lambda k: [refs[d][k][row0(d, j):row0(d, j) + n, q * qw:(q + 1) * qw] for d, q, j in groups]
    v, rt, kt, bt_t, kb_t, pc = (ld(k) for k in range(6))

    def bd(x):
        return jnp.where(bd_mask, jnp.concatenate([x.astype(BF16)] * HGROUP, axis=0), jnp.zeros((), BF16))

    mm = lambda x, w: _dot(x.astype(BF16), w)
    rows2 = lambda x, y: jnp.concatenate([x.astype(BF16), y.astype(BF16)], axis=0)

    bd2 = lambda x, y: jnp.concatenate([bd(x), bd(y)], axis=1)
    gram = [_dot(rows2(kt[g], rt[g]), bd2(bt_t[g], kb_t[g])) for g in ng]
    a_m = [jnp.where(strict[dirs[g]], gram[g][0:n, 0:qw], 0.0) for g in ng]
    c_m = [jnp.where(incl[dirs[g]], gram[g][n:2 * n, 0:qw], 0.0) for g in ng]
    b_m = [jnp.where(strict[dirs[g]], gram[g][0:n, qw:2 * qw], 0.0) for g in ng]
    e_m = [jnp.where(incl[dirs[g]], gram[g][n:2 * n, qw:2 * qw], 0.0) for g in ng]

    dd = [jnp.where(same16, a_m[g], 0.0) for g in ng]
    pw = [mm(dd[g], bd(dd[g])) for g in ng]
    t = [eye - dd[g] for g in ng]
    for _ in range(2):
        both = [_dot(rows2(t[g], pw[g]), bd(pw[g])) for g in ng]
        t = [t[g] + both[g][0:n] for g in ng]
        pw = [both[g][n:2 * n] for g in ng]
    t = [t[g] + mm(t[g], bd(pw[g])) for g in ng]
    for sib in (sib32, sib64):
        ta = [mm(t[g], bd(jnp.where(sib, a_m[g], 0.0))) for g in ng]
        t = [t[g] - mm(ta[g], bd(t[g])) for g in ng]

    bev = [_dot(jnp.concatenate([rows2(b_m[g], e_m[g]), kb_t[g]], axis=0), bd(v[g])) for g in ng]
    kv = [bev[g][2 * n:3 * n] for g in ng]
    mw = [mm(t[g], bd2(kt[g], bev[g][0:n])) for g in ng]
    m1 = [mw[g][:, 0:qw] for g in ng]
    w2 = [mw[g][:, qw:2 * qw] for g in ng]
    cq = [mm(c_m[g], bd2(m1[g], w2[g])) for g in ng]
    qh = [rt[g].astype(F32) - cq[g][:, 0:qw] for g in ng]
    y0 = [bev[g][n:2 * n] - cq[g][:, qw:2 * qw] for g in ng]

    chains = len(groups) // SCAN_CHUNKS
    h = [h_ref[s] for s in range(chains)]
    for g, (d, q, j) in enumerate(groups):
        s = g % chains
        uy = _dot(rows2(m1[g], qh[g]), bd(h[s]))
        (y0_ref, y1_ref)[d][row0(d, j):row0(d, j) + n, q * qw:(q + 1) * qw] = uy[n:2 * n] + y0[g]
        h[s] = pc[g] * (h[s] - _dot(bt_t[g], bd(uy[0:n] + w2[g])) + kv[g])
    for s in range(chains):
        h_ref[s] = h[s]


def _rwkv_scan(v, ops0, ops1, *, n_batch, seq, ctx_len):
    nt = v.shape[0]
    n = CHUNK * SCAN_CHUNKS
    assert seq % n == 0 and ctx_len % n == 0
    lat_c = seq // n
    ctx_c = ctx_len // n
    ctx0 = n_batch * lat_c

    def idx_f(bi, c):
        return (jnp.where(c < ctx_c, ctx0 + bi * ctx_c + c, bi * lat_c + (c - ctx_c)), 0)

    def idx_r(bi, c):
        return (jnp.where(c < ctx_c, ctx0 + bi * ctx_c + (ctx_c - 1 - c), bi * lat_c + (lat_c - 1 - (c - ctx_c))), 0)

    spec_f = pl.BlockSpec((n, RWKV_WIDTH), idx_f)
    spec_r = pl.BlockSpec((n, RWKV_WIDTH), idx_r)
    out = jax.ShapeDtypeStruct((nt, RWKV_WIDTH), F32)
    return pl.pallas_call(
        _scan_kernel,
        out_shape=(out, out),
        grid=(n_batch, ctx_c + lat_c),
        in_specs=[spec_f] * 6 + [spec_r] * 6,
        out_specs=(spec_f, spec_r),
        scratch_shapes=[pltpu.VMEM((2 * HEADS // HGROUP, CHUNK, HGROUP_W), F32)],
        compiler_params=_params(("parallel", "arbitrary")),
        name="rwkv_scan",
    )(v, *ops0, v, *ops1)


def _mix_out_kernel(*refs, lat_tiles):
    (yf_ref, yb_ref, g_ref, bon_ref, lng_ref, lnb_ref, ones_ref, woa_ref, wor_ref, xl_ref, xc_ref, mod_ref, pg_ref,
     att_ref) = refs[:14]
    o_ref = refs[-1]
    is_lat = pl.program_id(0) < lat_tiles
    ones_bd = ones_ref[...]
    y = yf_ref[...] + yb_ref[...]
    inv_n = 1.0 / RWKV_HEAD
    mu = _seg_sum(y, ones_bd) * inv_n
    yc = y - mu
    var = _seg_sum(yc * yc, ones_bd) * inv_n
    yn = yc * lax.rsqrt(var + LNX_EPS) * lng_ref[...] + lnb_ref[...]
    rw = (yn + bon_ref[...]) * g_ref[...]
    att = att_ref[...]
    if len(refs) == 16:
        att = jnp.where(is_lat, att, refs[14][...])
    mix = _dot(att, woa_ref[...]) + _dot(rw.astype(BF16), wor_ref[...])
    x = jnp.where(is_lat, xl_ref[...], xc_ref[...])
    o_ref[...] = x + mod_ref[0, 2:3, :] * _rms(mix, pg_ref[...])


def _mix_out(yf, yb, att_lat, att_ctx, g, bon, lnx_g, lnx_b, ones_bd, wo_att, wo_rw, x_lat, x_ctx, mods, post_g, *,
             rows, n_lat, seq):
    tm = TOKEN_TILE
    lat_tiles = n_lat // tm
    n_batch = n_lat // seq
    w = RWKV_WIDTH

    def mod_idx(i):
        return (jnp.where(i < lat_tiles, (i * tm) // seq, n_batch), 0, 0)

    const = lambda i: (0, 0)
    row = lambda i: (i, 0)
    in_specs = [
        pl.BlockSpec((tm, w), row),
        pl.BlockSpec((tm, w), row),
        pl.BlockSpec((tm, w), row),
        pl.BlockSpec((tm, w), row),
        pl.BlockSpec((1, w), const),
        pl.BlockSpec((1, w), const),
        pl.BlockSpec((w, w), const),
        pl.BlockSpec((w, D_MODEL), const),
        pl.BlockSpec((w, D_MODEL), const),
        *_stream_specs(tm, lat_tiles, x_ctx is x_lat),
        pl.BlockSpec((1, N_MOD, D_MODEL), mod_idx),
        pl.BlockSpec((1, D_MODEL), const),
        pl.BlockSpec((tm, w), lambda i: (jnp.minimum(i, lat_tiles - 1), 0)),
    ]
    args = [yf, yb, g, bon, lnx_g, lnx_b, ones_bd, wo_att, wo_rw, x_lat, x_ctx, mods, post_g, att_lat]
    if att_ctx is not None:
        in_specs.append(pl.BlockSpec((tm, w), lambda i: (jnp.maximum(i - lat_tiles, 0), 0)))
        args.append(att_ctx)
    return pl.pallas_call(
        functools.partial(_mix_out_kernel, lat_tiles=lat_tiles),
        out_shape=jax.ShapeDtypeStruct((rows, D_MODEL), F32),
        grid=(rows // tm,),
        in_specs=in_specs,
        out_specs=pl.BlockSpec((tm, D_MODEL), row),
        compiler_params=_params(("parallel",)),
        name="mix_out",
    )(*args)


def _route(logits_t, bias_col):
    scores = jax.nn.sigmoid(logits_t[0:N_EXPERTS, :])
    biased = scores + bias_col
    s_rows = [scores[e:e + 1, :] for e in range(N_EXPERTS)]
    b_rows = [biased[e:e + 1, :] for e in range(N_EXPERTS)]
    npg = EXPERTS_PER_GROUP
    group_scores = []
    for gi in range(N_GROUPS):
        bg = b_rows[gi * npg:(gi + 1) * npg]
        best_pair = None
        for i in range(npg):
            for j in range(i + 1, npg):
                pair = bg[i] + bg[j]
                best_pair = pair if best_pair is None else jnp.maximum(best_pair, pair)
        group_scores.append(best_pair)
    best = group_scores[0]
    best_idx = jnp.zeros(best.shape, jnp.int32)
    for gi in range(1, N_GROUPS):
        upd = group_scores[gi] > best
        best = jnp.where(upd, group_scores[gi], best)
        best_idx = jnp.where(upd, gi, best_idx)
    pick = lambda rows, j: functools.reduce(
        lambda acc, gi: jnp.where(best_idx == gi, rows[gi * npg + j], acc), range(1, N_GROUPS), rows[j])
    bb = [pick(b_rows, j) for j in range(npg)]
    ss = [pick(s_rows, j) for j in range(npg)]
    weights = []
    for j in range(npg):
        rank = jnp.zeros(best.shape, jnp.int32)
        for i in range(npg):
            if i == j:
                continue
            beats = (bb[i] > bb[j]) | ((bb[i] == bb[j]) & (i < j)) if i < j else (bb[i] > bb[j])
            rank = rank + beats.astype(jnp.int32)
        weights.append(jnp.where(rank < 2, ss[j], 0.0))
    den = weights[0] + weights[1] + weights[2] + weights[3]
    return [wj / den for wj in weights], best_idx


def _swiglu(h_b, wgu, wd, gate=None):
    gu = _dot(h_b, wgu)
    g_part = gu[:, 0:EXPERT_HIDDEN]
    act = g_part * jax.nn.sigmoid(g_part) * gu[:, EXPERT_HIDDEN:]
    if gate is not None:
        act = act * gate
    return _dot(act.astype(BF16), wd)


def _moe_kernel(x_ref, mod_ref, pre_ref, post_ref, rw_ref, rb_ref, before_ref, wgu_ref, wd_ref, o_ref, sorted_scr,
                f_scr):
    tm = x_ref.shape[0]
    blk = MOE_BLOCK
    n_blocks = sorted_scr.shape[0] // blk
    x = x_ref[...]
    h = _rms(x, pre_ref[...]) * (1.0 + mod_ref[0, 4:5, :]) + mod_ref[0, 3:4, :]
    h_hi = h.astype(BF16)
    h_lo = (h - h_hi.astype(F32)).astype(BF16)
    rw = rw_ref[...]
    rw_hi = rw.astype(BF16)
    rw_lo = (rw - rw_hi.astype(F32)).astype(BF16)
    logits_t = _dot_nt(rw_hi, h_hi) + _dot_nt(rw_lo, h_hi) + _dot_nt(rw_hi, h_lo)
    gates, group = _route(logits_t, rb_ref[...])

    sub8 = lax.broadcasted_iota(jnp.int32, (SUBLANES, tm), 0)
    onehot = jnp.where(sub8 == group, 1.0, 0.0)
    before = _dot(onehot.astype(BF16), before_ref[...])
    counts = [jnp.sum(onehot[g:g + 1, :]).astype(jnp.int32) for g in range(N_GROUPS)]
    ends, acc = [], jnp.int32(0)
    for g in range(N_GROUPS):
        acc = acc + ((counts[g] + (blk - 1)) // blk) * blk
        ends.append(acc)
    starts = [jnp.int32(0)] + ends[:-1]
    pos = sum(onehot[g:g + 1, :] * (before[g:g + 1, :] + starts[g].astype(F32)) for g in range(N_GROUPS))

    sub = lax.broadcasted_iota(jnp.int32, (LANES, tm), 0)
    stack = jnp.where(sub == EXPERTS_PER_GROUP, pos, 0.0)
    for j in range(EXPERTS_PER_GROUP):
        stack = jnp.where(sub == j, gates[j], stack)
    cols = stack.T
    g_hi = cols.astype(BF16)
    g_lo = (cols - g_hi.astype(F32)).astype(BF16)
    pos_row = pos.astype(jnp.int32)
    pos_col = cols[:, EXPERTS_PER_GROUP:EXPERTS_PER_GROUP + 1].astype(jnp.int32)

    f_shared = _swiglu(h_hi, wgu_ref[N_EXPERTS], wd_ref[N_EXPERTS])

    for s in range(n_blocks):
        r0 = s * blk
        rows = pl.ds(r0, blk)
        grp = sum((r0 >= ends[g]).astype(jnp.int32) for g in range(N_GROUPS - 1))

        @pl.when(r0 < ends[-1])
        def _():
            r_i = lax.broadcasted_iota(jnp.int32, (blk, tm), 0) + r0
            take = jnp.where(r_i == pos_row, 1.0, 0.0).astype(BF16)
            h_s = _dot(take, h_hi).astype(BF16)
            g_s = _dot(take, g_hi) + _dot(take, g_lo)
            out = None
            for j in range(EXPERTS_PER_GROUP):
                e = grp * EXPERTS_PER_GROUP + j
                y = _swiglu(h_s, wgu_ref[e], wd_ref[e], g_s[:, j:j + 1])
                out = y if out is None else out + y
            sorted_scr[rows, :] = out.astype(BF16)

        @pl.when(r0 >= ends[-1])
        def _():
            sorted_scr[rows, :] = jnp.zeros((blk, D_MODEL), BF16)

    main, total = MOE_MAIN_ROWS, n_blocks * blk

    def put_back(c0, c1):
        c_i = lax.broadcasted_iota(jnp.int32, (tm, c1 - c0), 1) + c0
        put = jnp.where(c_i == pos_col, 1.0, 0.0).astype(BF16)
        return _dot(put, sorted_scr[c0:c1, :])

    f_scr[...] = f_shared + put_back(0, main)

    @pl.when(ends[-1] > main)
    def _():
        f_scr[...] += put_back(main, total)

    o_ref[...] = x + mod_ref[0, 5:6, :] * _rms(f_scr[...], post_ref[...])


def _moe(x_all, mods, pre_g, post_g, router_wt, router_b, before_m, wgu, wd, *, rows, n_lat, seq):
    tm = MOE_TILE
    lat_tiles = n_lat // tm
    n_batch = n_lat // seq
    n_e = wgu.shape[0]

    def mod_idx(i):
        return (jnp.where(i < lat_tiles, (i * tm) // seq, n_batch), 0, 0)

    const = lambda i: (0, 0)
    const3 = lambda i: (0, 0, 0)
    row = lambda i: (i, 0)
    resident = pl.Buffered(1)
    return pl.pallas_call(
        _moe_kernel,
        out_shape=jax.ShapeDtypeStruct((rows, D_MODEL), F32),
        grid=(rows // tm,),
        in_specs=[
            pl.BlockSpec((tm, D_MODEL), row),
            pl.BlockSpec((1, N_MOD, D_MODEL), mod_idx),
            pl.BlockSpec((1, D_MODEL), const),
            pl.BlockSpec((1, D_MODEL), const),
            pl.BlockSpec((LANES, D_MODEL), const),
            pl.BlockSpec((N_EXPERTS, 1), const),
            pl.BlockSpec((tm, tm), const),
            pl.BlockSpec((n_e, D_MODEL, 2 * EXPERT_HIDDEN), const3, pipeline_mode=resident),
            pl.BlockSpec((n_e, EXPERT_HIDDEN, D_MODEL), const3, pipeline_mode=resident),
        ],
        out_specs=pl.BlockSpec((tm, D_MODEL), row),
        scratch_shapes=[pltpu.VMEM((tm + N_GROUPS * MOE_BLOCK, D_MODEL), BF16), pltpu.VMEM((tm, D_MODEL), F32)],
        compiler_params=_params(("parallel",)),
        name="moe",
    )(x_all, mods, pre_g, post_g, router_wt, router_b, before_m, wgu, wd)


def _pack_in_proj(w_in, shift_mu):
    d = w_in.shape[0]
    z = lambda n: jnp.zeros((d, n), w_in.dtype)
    kr = w_in[:, Q_RANK + KV_RANK:MLA_COLS]
    half = MLA_ROPE // 2
    kr_sw = jnp.concatenate([kr[:, half:], kr[:, :half]], axis=1)
    pad = LANES - MLA_NOPE - MLA_ROPE
    packed = jnp.concatenate([
        w_in[:, :Q_RANK + KV_RANK],
        z(MLA_NOPE), kr, z(pad),
        z(MLA_NOPE), kr_sw, z(pad),
        w_in[:, MLA_COLS:], z(RWKV_PAD - RWKV_COLS),
    ], axis=1)
    mu = jnp.pad(shift_mu, ((0, 0), (0, RWKV_PAD - RWKV_COLS)))
    return packed.astype(BF16), mu


def _pack_mla(w_uq, w_ukv):
    half = MLA_ROPE // 2
    dq = MLA_NOPE + MLA_ROPE
    q3 = w_uq.reshape(Q_RANK, HEADS, dq)
    zq = lambda n: jnp.zeros((Q_RANK, HEADS, n), w_uq.dtype)
    pad = HEAD_SLOT - dq
    wq1 = jnp.concatenate([q3, zq(pad)], axis=2).reshape(Q_RANK, HEADS * HEAD_SLOT)
    wq2 = jnp.concatenate([zq(MLA_NOPE), q3[:, :, MLA_NOPE + half:], q3[:, :, MLA_NOPE:MLA_NOPE + half], zq(pad)],
                          axis=2).reshape(Q_RANK, HEADS * HEAD_SLOT)
    kv3 = w_ukv.reshape(KV_RANK, HEADS, MLA_NOPE + MLA_V)
    wuk = jnp.concatenate([kv3[:, :, :MLA_NOPE], jnp.zeros((KV_RANK, HEADS, HEAD_SLOT - MLA_NOPE), w_ukv.dtype)],
                          axis=2).reshape(KV_RANK, HEADS * HEAD_SLOT)
    wuv = kv3[:, :, MLA_NOPE:].reshape(KV_RANK, HEADS * MLA_V).T
    return wq1.astype(BF16), wq2.astype(BF16), wuk.astype(BF16), wuv.astype(BF16)


def _block_diag2(m):
    r, c = m.shape[1], m.shape[2]
    z = jnp.zeros((r, c), m.dtype)
    return jnp.concatenate([jnp.concatenate([m[0], z], axis=1), jnp.concatenate([z, m[1]], axis=1)], axis=0)


def _rope_tables(seq, ctx_len):
    axis_dim = MLA_ROPE // 2
    t = jnp.arange(seq, dtype=jnp.int32)
    row = (t // GRID_W).astype(F32)
    col = (t % GRID_W).astype(F32)
    inv_freq = ROPE_BASE ** (-jnp.arange(0, axis_dim, 2, dtype=F32) / axis_dim)
    ang = jnp.concatenate([row[:, None] * inv_freq, col[:, None] * inv_freq], axis=-1)
    cos = jnp.concatenate([jnp.cos(ang), jnp.ones((ctx_len, axis_dim), F32)], axis=0)
    sin = jnp.concatenate([jnp.sin(ang), jnp.zeros((ctx_len, axis_dim), F32)], axis=0)
    n = seq + ctx_len
    ones = jnp.ones((n, MLA_NOPE), F32)
    z_nope = jnp.zeros((n, MLA_NOPE), F32)
    z_pad = jnp.zeros((n, HEAD_SLOT - MLA_NOPE - MLA_ROPE), F32)
    cq = jnp.concatenate([ones, cos, cos, z_pad], axis=1)
    sq = jnp.concatenate([z_nope, -sin, sin, z_pad], axis=1)
    ck = jnp.concatenate([z_nope, cos, cos, z_pad], axis=1)
    return cq, sq, ck


def kernel(x, c, ctx, c_ctx, ada_w, ada_b, mix_pre_g, mix_post_g, ffn_pre_g, ffn_post_g, w_in, q_norm_g, kv_norm_g,
           w_uq, w_ukv, shift_mu, decay_w0, decay_w2, iclr_a0, iclr_a2, gate_g2, k_k, k_a, r_k, lnx_g, lnx_b, w_out,
           router_w, router_bias, exp_w_gate, exp_w_up, exp_w_down, sh_w_gate, sh_w_up, sh_w_down):
    n_batch, seq, d = x.shape
    ctx_len = ctx.shape[1]
    depth = ada_w.shape[0]
    assert d == D_MODEL and seq % TOKEN_TILE == 0 and ctx_len % TOKEN_TILE == 0 and seq % ctx_len == 0
    assert seq % GRID_W == 0 and n_batch + 1 <= SUBLANES
    n_lat = n_batch * seq
    n_ctx = n_batch * ctx_len
    assert seq % MOE_TILE == 0 and n_ctx % MOE_TILE == 0

    x_lat, x_ctx = x.reshape(n_lat, d), ctx.reshape(n_ctx, d)
    cond_rows = jnp.concatenate([c, c_ctx[None], jnp.zeros((SUBLANES - n_batch - 1, d), F32)], axis=0)
    mods_all = _ada_modulation(cond_rows, ada_w, ada_b).reshape(depth, SUBLANES, N_MOD, d)

    cq_t, sq_t, ck_t = _rope_tables(seq, ctx_len)
    w = RWKV_WIDTH
    ones_bd = (jnp.arange(w)[:, None] // RWKV_HEAD == jnp.arange(w)[None, :] // RWKV_HEAD).astype(BF16)
    ti = jnp.arange(TOKEN_TILE)
    same_chunk = (ti[:, None] // CHUNK) == (ti[None, :] // CHUNK)
    ones_c = same_chunk.astype(BF16)
    tri_f = jnp.logical_and(same_chunk, ti[None, :] <= ti[:, None]).astype(BF16)
    tri_r = jnp.logical_and(same_chunk, ti[None, :] >= ti[:, None]).astype(BF16)
    router_wt = jnp.pad(router_w.T, ((0, LANES - N_EXPERTS), (0, 0)))
    router_b = router_bias.reshape(N_EXPERTS, 1)
    tj = jnp.arange(MOE_TILE)
    before_m = (tj[:, None] < tj[None, :]).astype(BF16)
    row1 = lambda a: a.reshape(1, -1)

    for l in range(depth):
        ctx_out = l < depth - 1
        mods = mods_all[l]
        rows = n_lat + n_ctx if ctx_out else n_lat

        w_in_p, mu = _pack_in_proj(w_in[l], shift_mu[l])
        wq1, wq2, wuk, wuv = _pack_mla(w_uq[l], w_ukv[l])
        q, k, v, p_rw = _in_proj(x_lat, x_ctx, mods, row1(mix_pre_g[l]), w_in_p, row1(q_norm_g[l]),
                                 row1(kv_norm_g[l]), wq1, wq2, wuk, wuv, cq_t, sq_t, ck_t, n_lat=n_lat, n_ctx=n_ctx,
                                 seq=seq, ctx_len=ctx_len)
        att = _attention(q, k, v, n_batch=n_batch, seq=seq, ctx_len=ctx_len, latent=True)
        att_c = _attention(q, k, v, n_batch=n_batch, seq=seq, ctx_len=ctx_len, latent=False) if ctx_out else None

        g2p = jnp.pad(gate_g2[l], ((0, GATE_PAD - GATE_LORA), (0, 0))).astype(BF16)
        prep = _rwkv_prep(
            p_rw, mu, _block_diag2(decay_w2[l]).astype(BF16), decay_w0[l].reshape(1, 2 * w),
            _block_diag2(iclr_a2[l]).astype(BF16), iclr_a0[l].reshape(1, 2 * w), g2p,
            row1(k_k[l]), row1(k_a[l]), row1(r_k[l]), ones_bd, tri_f, tri_r, ones_c,
            n_lat=n_lat, seq=seq, ctx_len=ctx_len)
        vv, g, bon = prep[0], prep[11], prep[12]
        yf, yb = _rwkv_scan(vv, prep[1:6], prep[6:11], n_batch=n_batch, seq=seq, ctx_len=ctx_len)

        wo = w_out[l].astype(BF16)
        x_all = _mix_out(yf, yb, att, att_c, g, bon, row1(lnx_g[l]), row1(lnx_b[l]), ones_bd, wo[:w], wo[w:], x_lat,
                         x_ctx, mods, row1(mix_post_g[l]), rows=rows, n_lat=n_lat, seq=seq)

        wgu = jnp.concatenate([
            jnp.concatenate([exp_w_gate[l], exp_w_up[l]], axis=2),
            jnp.concatenate([sh_w_gate[l], sh_w_up[l]], axis=1)[None]], axis=0).astype(BF16)
        wd = jnp.concatenate([exp_w_down[l], sh_w_down[l][None]], axis=0).astype(BF16)
        x_all = _moe(x_all, mods, row1(ffn_pre_g[l]), row1(ffn_post_g[l]), router_wt, router_b, before_m, wgu, wd,
                     rows=rows, n_lat=n_lat, seq=seq)
        x_lat = x_ctx = x_all

    return x_all.reshape(n_batch, seq, d)
```

```python
import functools
import math

import jax
import jax.numpy as jnp
from jax import lax
from jax.experimental import pallas as pl
from jax.experimental.pallas import tpu as pltpu

F32 = jnp.float32
BF16 = jnp.bfloat16
HIGHEST = lax.Precision.HIGHEST

D_MODEL = 1024
N_MOD = 6
NORM_EPS = 1e-6
GRID_W = 64
ROPE_BASE = 10000.0

HEADS = 8
MLA_NOPE = 64
MLA_ROPE = 32
MLA_V = 64
Q_RANK = 256
KV_RANK = 128
MLA_COLS = Q_RANK + KV_RANK + MLA_ROPE

RWKV_HEAD = 64
RWKV_WIDTH = HEADS * RWKV_HEAD
DECAY_LORA = 64
ICLR_LORA = 64
GATE_LORA = 160
RWKV_COLS = 3 * RWKV_WIDTH + 2 * DECAY_LORA + 2 * ICLR_LORA + GATE_LORA
RWKV_PAD = 2048
GATE_PAD = 256
LNX_EPS = 64e-5

N_EXPERTS = 16
N_GROUPS = 4
EXPERTS_PER_GROUP = 4
EXPERT_HIDDEN = 256

LANES = 128
SUBLANES = 8
HEAD_SLOT = 128
IN_PACKED = Q_RANK + KV_RANK + 2 * LANES + RWKV_PAD
TOKEN_TILE = 256
PROJ_TILE = 512
ATTN_KV_BLOCK = 2048
ATTN_Q_TILE = 512
MOE_TILE = 512
MOE_BLOCK = 128
MOE_MAIN_ROWS = MOE_TILE + 2 * MOE_BLOCK
CHUNK = 64
INV_DIAG = 16
ADA_TILE = 1536
VMEM_LIMIT = 48 * 1024 * 1024


def _dot(a, b):
    return jnp.dot(a, b, preferred_element_type=F32)


def _dot_nt(a, b):
    return lax.dot_general(a, b, (((1,), (1,)), ((), ())), preferred_element_type=F32)


def _rms(x, g):
    return x * lax.rsqrt(jnp.mean(x * x, axis=-1, keepdims=True) + NORM_EPS) * g


def _seg_sum(x, ones_bd):
    hi = x.astype(BF16)
    lo = (x - hi.astype(F32)).astype(BF16)
    return _dot(hi, ones_bd) + _dot(lo, ones_bd)


def _params(sem):
    return pltpu.CompilerParams(dimension_semantics=sem, vmem_limit_bytes=VMEM_LIMIT)


def _ada_kernel(c_ref, w_ref, b_ref, o_ref):
    c = c_ref[...]
    cond = c * jax.nn.sigmoid(c)
    o_ref[0] = jnp.dot(cond, w_ref[0], precision=HIGHEST, preferred_element_type=F32) + b_ref[0]


def _ada_modulation(cond_rows, ada_w, ada_b):
    depth, d, n = ada_w.shape
    tn = ADA_TILE
    rows = cond_rows.shape[0]
    return pl.pallas_call(
        _ada_kernel,
        out_shape=jax.ShapeDtypeStruct((depth, rows, n), F32),
        grid=(depth, n // tn),
        in_specs=[
            pl.BlockSpec((rows, d), lambda l, j: (0, 0)),
            pl.BlockSpec((1, d, tn), lambda l, j: (l, 0, j)),
            pl.BlockSpec((1, 1, tn), lambda l, j: (l, 0, j)),
        ],
        out_specs=pl.BlockSpec((1, rows, tn), lambda l, j: (l, 0, j)),
        compiler_params=_params(("parallel", "parallel")),
        name="ada_modulation",
    )(cond_rows, ada_w, ada_b.reshape(depth, 1, n))


def _stream_specs(tm, lat_tiles, merged):
    off = lat_tiles if merged else 0
    return (pl.BlockSpec((tm, D_MODEL), lambda i: (jnp.minimum(i, lat_tiles - 1), 0)),
            pl.BlockSpec((tm, D_MODEL), lambda i: (jnp.maximum(i - lat_tiles, 0) + off, 0)))


def _in_proj_kernel(xl_ref, xc_ref, mod_ref, g_ref, win_ref, qng_ref, kvng_ref, wq1_ref, wq2_ref, wuk_ref, wuv_ref,
                    cq_ref, sq_ref, ck_ref, q_out, k_out, v_out, p_out, *, q_scale, lat_tiles):
    x = jnp.where(pl.program_id(0) < lat_tiles, xl_ref[...], xc_ref[...])
    shift = mod_ref[0, 0:1, :]
    scale = mod_ref[0, 1:2, :]
    h = _rms(x, g_ref[...]) * (1.0 + scale) + shift
    p = _dot(h.astype(BF16), win_ref[...])
    c_q = p[:, 0:Q_RANK]
    c_kv = p[:, Q_RANK:Q_RANK + KV_RANK]
    kr_a = p[:, Q_RANK + KV_RANK:Q_RANK + KV_RANK + LANES]
    kr_b = p[:, Q_RANK + KV_RANK + LANES:Q_RANK + KV_RANK + 2 * LANES]
    p_out[...] = p[:, Q_RANK + KV_RANK + 2 * LANES:]

    tile8 = lambda t: jnp.concatenate([t] * HEADS, axis=1)
    cqn = _rms(c_q, qng_ref[...]).astype(BF16)
    q = _dot(cqn, wq1_ref[...]) * tile8(cq_ref[...]) + _dot(cqn, wq2_ref[...]) * tile8(sq_ref[...])
    q_out[...] = (q * q_scale).astype(BF16)

    ckvn = _rms(c_kv, kvng_ref[...]).astype(BF16)
    k_rot = kr_a * ck_ref[...] + kr_b * sq_ref[...]
    k_out[...] = (_dot(ckvn, wuk_ref[...]) + tile8(k_rot)).astype(BF16)
    v_out[...] = _dot_nt(wuv_ref[...], ckvn).astype(BF16)


def _in_proj(x_lat, x_ctx, mods, g, w_in_p, qng, kvng, wq1, wq2, wuk, wuv, cq_t, sq_t, ck_t, *, n_lat, n_ctx, seq,
             ctx_len):
    nt = n_lat + n_ctx
    tm = PROJ_TILE
    lat_tiles = n_lat // tm
    n_batch = n_lat // seq

    def mod_idx(i):
        return (jnp.where(i < lat_tiles, (i * tm) // seq, n_batch), 0, 0)

    def tab_idx(i):
        return (jnp.where(i < lat_tiles, i % (seq // tm), seq // tm), 0)

    const = lambda i: (0, 0)
    row = lambda i: (i, 0)
    qw = HEADS * HEAD_SLOT
    return pl.pallas_call(
        functools.partial(_in_proj_kernel, q_scale=float((MLA_NOPE + MLA_ROPE) ** -0.5 * math.log2(math.e)),
                          lat_tiles=lat_tiles),
        out_shape=(
            jax.ShapeDtypeStruct((nt, qw), BF16),
            jax.ShapeDtypeStruct((nt, qw), BF16),
            jax.ShapeDtypeStruct((HEADS * MLA_V, nt), BF16),
            jax.ShapeDtypeStruct((nt, RWKV_PAD), F32),
        ),
        grid=(nt // tm,),
        in_specs=[
            *_stream_specs(tm, lat_tiles, x_ctx is x_lat),
            pl.BlockSpec((1, N_MOD, D_MODEL), mod_idx),
            pl.BlockSpec((1, D_MODEL), const),
            pl.BlockSpec((D_MODEL, IN_PACKED), const),
            pl.BlockSpec((1, Q_RANK), const),
            pl.BlockSpec((1, KV_RANK), const),
            pl.BlockSpec((Q_RANK, qw), const),
            pl.BlockSpec((Q_RANK, qw), const),
            pl.BlockSpec((KV_RANK, qw), const),
            pl.BlockSpec((HEADS * MLA_V, KV_RANK), const),
            pl.BlockSpec((tm, HEAD_SLOT), tab_idx),
            pl.BlockSpec((tm, HEAD_SLOT), tab_idx),
            pl.BlockSpec((tm, HEAD_SLOT), tab_idx),
        ],
        out_specs=(
            pl.BlockSpec((tm, qw), row),
            pl.BlockSpec((tm, qw), row),
            pl.BlockSpec((HEADS * MLA_V, tm), lambda i: (0, i)),
            pl.BlockSpec((tm, RWKV_PAD), row),
        ),
        compiler_params=_params(("parallel",)),
        name="in_proj",
    )(x_lat, x_ctx, mods, g, w_in_p, qng, kvng, wq1, wq2, wuk, wuv, cq_t, sq_t, ck_t)


def _attn_kernel(q_ref, kc_ref, vc_ref, *rest):
    o_ref = rest[-1]
    blocks = [(kc_ref, vc_ref, 0, kc_ref.shape[0])]
    if len(rest) == 3:
        kl_ref, vl_ref = rest[0], rest[1]
        kb = min(ATTN_KV_BLOCK, kl_ref.shape[0])
        blocks += [(kl_ref, vl_ref, s0, kb) for s0 in range(0, kl_ref.shape[0], kb)]
    heads = range(2)
    hs = [slice(h * HEAD_SLOT, (h + 1) * HEAD_SLOT) for h in heads]
    vs = [slice(h * MLA_V, (h + 1) * MLA_V) for h in heads]
    q = [q_ref[:, hs[h]] for h in heads]

    def scores(j):
        k_ref, _, s0, size = blocks[j]
        return [_dot_nt(k_ref[s0:s0 + size, hs[h]], q[h]) for h in heads]

    m, den, acc = [None] * 2, [None] * 2, [None] * 2
    s_cur = scores(0)
    for j, (_, vt_ref, s0, size) in enumerate(blocks):
        s_next = scores(j + 1) if j + 1 < len(blocks) else None
        for h in heads:
            s = s_cur[h]
            m_blk = jnp.max(s, axis=0, keepdims=True)
            if j == 0:
                m[h] = m_blk
                p = jnp.exp2(s - m_blk)
                den[h] = jnp.sum(p, axis=0, keepdims=True)
                acc[h] = _dot(vt_ref[vs[h], s0:s0 + size], p.astype(BF16))
            else:
                m_new = jnp.maximum(m[h], m_blk)
                alpha = jnp.exp2(m[h] - m_new)
                p = jnp.exp2(s - m_new)
                den[h] = alpha * den[h] + jnp.sum(p, axis=0, keepdims=True)
                acc[h] = alpha * acc[h] + _dot(vt_ref[vs[h], s0:s0 + size], p.astype(BF16))
                m[h] = m_new
        s_cur = s_next
    out_t = jnp.concatenate([acc[h] / den[h] for h in heads], axis=0)
    o_ref[...] = out_t.T.astype(o_ref.dtype)


def _attention(q, k, v_t, *, n_batch, seq, ctx_len, latent):
    tq = ATTN_Q_TILE if latent else TOKEN_TILE
    seg = seq if latent else ctx_len
    assert seg % tq == 0
    q_tiles = seg // tq
    q_blk0 = 0 if latent else (n_batch * seq) // tq
    ctx_blk0 = (n_batch * seq) // ctx_len
    kv_ctx = lambda b, hp, j: (ctx_blk0 + b, hp)
    kv_lat = lambda b, hp, j: (b, hp)
    in_specs = [
        pl.BlockSpec((tq, 2 * HEAD_SLOT), lambda b, hp, j: (q_blk0 + b * q_tiles + j, hp)),
        pl.BlockSpec((ctx_len, 2 * HEAD_SLOT), kv_ctx),
        pl.BlockSpec((2 * MLA_V, ctx_len), lambda b, hp, j: (hp, ctx_blk0 + b)),
    ]
    args = [q, k, v_t]
    if latent:
        in_specs += [pl.BlockSpec((seq, 2 * HEAD_SLOT), kv_lat),
                     pl.BlockSpec((2 * MLA_V, seq), lambda b, hp, j: (hp, b))]
        args += [k, v_t]
    return pl.pallas_call(
        _attn_kernel,
        out_shape=jax.ShapeDtypeStruct((n_batch * seg, HEADS * MLA_V), BF16),
        grid=(n_batch, HEADS // 2, q_tiles),
        in_specs=in_specs,
        out_specs=pl.BlockSpec((tq, 2 * MLA_V), lambda b, hp, j: (b * q_tiles + j, hp)),
        compiler_params=_params(("parallel", "parallel", "arbitrary")),
        name="attention_lat" if latent else "attention_ctx",
    )(*args)


def _split2(x):
    hi = x.astype(BF16)
    return hi, (x - hi.astype(F32)).astype(BF16)


def _chunk_transpose(x):
    n = CHUNK
    xt = x.T
    rows = []
    for c in range(x.shape[0] // n):
        rows.append(jnp.concatenate(
            [xt[h * n:(h + 1) * n, c * n:(c + 1) * n] for h in range(x.shape[1] // n)], axis=1))
    return jnp.concatenate(rows, axis=0)


def _rwkv_prep_kernel(p_ref, hp_ref, hn_ref, mu_ref, w2_ref, w0_ref, a2_ref, a0_ref, g2_ref, kk_ref, ka_ref, rk_ref,
                      ones_ref, trif_ref, trir_ref, onesc_ref, v_out, rt0_out, kt0_out, bt0_out, kb0_out, pc0_out,
                      rt1_out, kt1_out, bt1_out, kb1_out, pc1_out, g_out, bon_out, *, lat_tiles, seq_tiles,
                      ctx_tiles):
    i = pl.program_id(0)
    tm = p_ref.shape[0]
    is_lat = i < lat_tiles
    local = jnp.where(is_lat, i % seq_tiles, (i - lat_tiles) % ctx_tiles)
    seg = jnp.where(is_lat, seq_tiles, ctx_tiles)
    keep_prev = jnp.where(local == 0, 0.0, 1.0)
    keep_next = jnp.where(local == seg - 1, 0.0, 1.0)

    p = p_ref[...]
    row = lax.broadcasted_iota(jnp.int32, (tm, 1), 0)
    prev = jnp.where(row == 0, hp_ref[7:8, :] * keep_prev, pltpu.roll(p, 1, 0))
    nxt = jnp.where(row == tm - 1, hn_ref[0:1, :] * keep_next, pltpu.roll(p, tm - 1, 0))
    ps = p + mu_ref[0:1, :] * (prev - p) + mu_ref[1:2, :] * (nxt - p)

    w = RWKV_WIDTH
    r = ps[:, 0:w]
    k = ps[:, w:2 * w]
    v = ps[:, 2 * w:3 * w]
    wl = ps[:, 3 * w:3 * w + 2 * DECAY_LORA]
    al = ps[:, 3 * w + 2 * DECAY_LORA:3 * w + 2 * DECAY_LORA + 2 * ICLR_LORA]
    gl = ps[:, 3 * w + 2 * DECAY_LORA + 2 * ICLR_LORA:3 * w + 2 * DECAY_LORA + 2 * ICLR_LORA + GATE_PAD]

    g_out[...] = _dot(jax.nn.sigmoid(gl).astype(BF16), g2_ref[...])
    z = w0_ref[...] + _dot(jnp.tanh(wl).astype(BF16), w2_ref[...])
    logw = (-math.exp(-0.5) * math.log2(math.e)) * jax.nn.sigmoid(z)
    a = jax.nn.sigmoid(a0_ref[...] + _dot(al.astype(BF16), a2_ref[...]))

    ones_bd = ones_ref[...]
    kk = k * kk_ref[...]
    kk = kk * lax.rsqrt(_seg_sum(kk * kk, ones_bd) + 1e-12)
    ka = ka_ref[...]
    v_out[...] = v.astype(BF16)

    ke_sum = None
    for d, tri_ref, (rt_out, kt_out, bt_out, kb_out, pc_out) in (
            (0, trif_ref, (rt0_out, kt0_out, bt0_out, kb0_out, pc0_out)),
            (1, trir_ref, (rt1_out, kt1_out, bt1_out, kb1_out, pc1_out))):
        lw = logw[:, d * w:(d + 1) * w]
        a_d = a[:, d * w:(d + 1) * w]
        lw_hi, lw_lo = _split2(lw)
        cum = _dot(tri_ref[...], lw_hi) + _dot(tri_ref[...], lw_lo)
        tot = _dot(onesc_ref[...], lw_hi) + _dot(onesc_ref[...], lw_lo)
        e_neg = jnp.exp2(-cum)
        ke = k * (1.0 + (a_d - 1.0) * ka)
        ke_sum = ke if ke_sum is None else ke_sum + ke
        rt_out[...] = (r * jnp.exp2(cum)).astype(BF16)
        kt_out[...] = (kk * jnp.exp2(cum - lw)).astype(BF16)
        bt_out[...] = _chunk_transpose(a_d * kk * e_neg).astype(BF16)
        kb_out[...] = _chunk_transpose(ke * e_neg).astype(BF16)
        pc_out[...] = _chunk_transpose(jnp.exp2(tot))
    bon_out[...] = _seg_sum(r * ke_sum * rk_ref[...], ones_bd) * v


def _rwkv_prep(p_rw, mu, w2bd, w0, a2bd, a0, g2p, k_k, k_a, r_k, ones_bd, tri_f, tri_r, ones_c, *, n_lat, seq,
               ctx_len):
    nt = p_rw.shape[0]
    tm = TOKEN_TILE
    n_tiles = nt // tm
    halo = SUBLANES
    blocks8 = nt // halo
    const = lambda i: (0, 0)
    row = lambda i: (i, 0)
    w = RWKV_WIDTH
    o16 = jax.ShapeDtypeStruct((nt, w), BF16)
    o32 = jax.ShapeDtypeStruct((nt, w), F32)
    return pl.pallas_call(
        functools.partial(_rwkv_prep_kernel, lat_tiles=n_lat // tm, seq_tiles=seq // tm, ctx_tiles=ctx_len // tm),
        out_shape=(o16, o16, o16, o16, o16, o32, o16, o16, o16, o16, o32, o32, o32),
        grid=(n_tiles,),
        in_specs=[
            pl.BlockSpec((tm, RWKV_PAD), row),
            pl.BlockSpec((halo, RWKV_PAD), lambda i: (jnp.maximum(i * (tm // halo) - 1, 0), 0)),
            pl.BlockSpec((halo, RWKV_PAD), lambda i: (jnp.minimum((i + 1) * (tm // halo), blocks8 - 1), 0)),
            pl.BlockSpec((2, RWKV_PAD), const),
            pl.BlockSpec((2 * DECAY_LORA, 2 * w), const),
            pl.BlockSpec((1, 2 * w), const),
            pl.BlockSpec((2 * ICLR_LORA, 2 * w), const),
            pl.BlockSpec((1, 2 * w), const),
            pl.BlockSpec((GATE_PAD, w), const),
            pl.BlockSpec((1, w), const),
            pl.BlockSpec((1, w), const),
            pl.BlockSpec((1, w), const),
            pl.BlockSpec((w, w), const),
            pl.BlockSpec((tm, tm), const),
            pl.BlockSpec((tm, tm), const),
            pl.BlockSpec((tm, tm), const),
        ],
        out_specs=(pl.BlockSpec((tm, w), row),) * 13,
        compiler_params=_params(("parallel",)),
        name="rwkv_prep",
    )(p_rw, p_rw, p_rw, mu, w2bd, w0, a2bd, a0, g2p, k_k, k_a, r_k, ones_bd, tri_f, tri_r, ones_c)


HGROUP = 2
HGROUP_W = HGROUP * RWKV_HEAD
SCAN_CHUNKS = 4


def _scan_kernel(v0_ref, rt0_ref, kt0_ref, bt0_ref, kb0_ref, pc0_ref, v1_ref, rt1_ref, kt1_ref, bt1_ref, kb1_ref,
                 pc1_ref, y0_ref, y1_ref, h_ref):
    @pl.when(pl.program_id(1) == 0)
    def _():
        h_ref[...] = jnp.zeros_like(h_ref)

    n, qw = CHUNK, HGROUP_W
    t_i = lax.broadcasted_iota(jnp.int32, (n, qw), 0)
    s_i = lax.broadcasted_iota(jnp.int32, (n, qw), 1) % n
    bd_mask = (lax.broadcasted_iota(jnp.int32, (qw, qw), 0) // n) == (lax.broadcasted_iota(jnp.int32, (qw, qw), 1) // n)
    eye = jnp.where(t_i == s_i, 1.0, 0.0)
    same16 = (t_i // INV_DIAG) == (s_i // INV_DIAG)
    same32 = (t_i // (2 * INV_DIAG)) == (s_i // (2 * INV_DIAG))
    sib32 = jnp.logical_and(same32, jnp.logical_not(same16))
    sib64 = jnp.logical_not(same32)
    incl = (s_i <= t_i, s_i >= t_i)
    strict = (s_i < t_i, s_i > t_i)

    groups = [(d, q, j) for j in range(SCAN_CHUNKS) for d in (0, 1) for q in range(HEADS // HGROUP)]
    dirs = [d for d, _, _ in groups]
    ng = range(len(groups))
    row0 = lambda d, j: (SCAN_CHUNKS - 1 - j) * n if d else j * n
    refs = ((v0_ref, rt0_ref, kt0_ref, bt0_ref, kb0_ref, pc0_ref), (v1_ref, rt1_ref, kt1_ref, bt1_ref, kb1_ref, pc1_ref))
    ld = lambda k: [refs[d][k][row0(d, j):row0(d, j) + n, q * qw:(q + 1) * qw] for d, q, j in groups]
    v, rt, kt, bt_t, kb_t, pc = (ld(k) for k in range(6))

    def bd(x):
        return jnp.where(bd_mask, jnp.concatenate([x.astype(BF16)] * HGROUP, axis=0), jnp.zeros((), BF16))

    mm = lambda x, w: _dot(x.astype(BF16), w)
    rows2 = lambda x, y: jnp.concatenate([x.astype(BF16), y.astype(BF16)], axis=0)

    bd2 = lambda x, y: jnp.concatenate([bd(x), bd(y)], axis=1)
    gram = [_dot(rows2(kt[g], rt[g]), bd2(bt_t[g], kb_t[g])) for g in ng]
    a_m = [jnp.where(strict[dirs[g]], gram[g][0:n, 0:qw], 0.0) for g in ng]
    c_m = [jnp.where(incl[dirs[g]], gram[g][n:2 * n, 0:qw], 0.0) for g in ng]
    b_m = [jnp.where(strict[dirs[g]], gram[g][0:n, qw:2 * qw], 0.0) for g in ng]
    e_m = [jnp.where(incl[dirs[g]], gram[g][n:2 * n, qw:2 * qw], 0.0) for g in ng]

    dd = [jnp.where(same16, a_m[g], 0.0) for g in ng]
    pw = [mm(dd[g], bd(dd[g])) for g in ng]
    t = [eye - dd[g] for g in ng]
    for _ in range(2):
        both = [_dot(rows2(t[g], pw[g]), bd(pw[g])) for g in ng]
        t = [t[g] + both[g][0:n] for g in ng]
        pw = [both[g][n:2 * n] for g in ng]
    t = [t[g] + mm(t[g], bd(pw[g])) for g in ng]
    for sib in (sib32, sib64):
        ta = [mm(t[g], bd(jnp.where(sib, a_m[g], 0.0))) for g in ng]
        t = [t[g] - mm(ta[g], bd(t[g])) for g in ng]

    bev = [_dot(jnp.concatenate([rows2(b_m[g], e_m[g]), kb_t[g]], axis=0), bd(v[g])) for g in ng]
    kv = [bev[g][2 * n:3 * n] for g in ng]
    mw = [mm(t[g], bd2(kt[g], bev[g][0:n])) for g in ng]
    m1 = [mw[g][:, 0:qw] for g in ng]
    w2 = [mw[g][:, qw:2 * qw] for g in ng]
    cq = [mm(c_m[g], bd2(m1[g], w2[g])) for g in ng]
    qh = [rt[g].astype(F32) - cq[g][:, 0:qw] for g in ng]
    y0 = [bev[g][n:2 * n] - cq[g][:, qw:2 * qw] for g in ng]

    chains = len(groups) // SCAN_CHUNKS
    h = [h_ref[s] for s in range(chains)]
    for g, (d, q, j) in enumerate(groups):
        s = g % chains
        uy = _dot(rows2(m1[g], qh[g]), bd(h[s]))
        (y0_ref, y1_ref)[d][row0(d, j):row0(d, j) + n, q * qw:(q + 1) * qw] = uy[n:2 * n] + y0[g]
        h[s] = pc[g] * (h[s] - _dot(bt_t[g], bd(uy[0:n] + w2[g])) + kv[g])
    for s in range(chains):
        h_ref[s] = h[s]


def _rwkv_scan(v, ops0, ops1, *, n_batch, seq, ctx_len):
    nt = v.shape[0]
    n = CHUNK * SCAN_CHUNKS
    assert seq % n == 0 and ctx_len % n == 0
    lat_c = seq // n
    ctx_c = ctx_len // n
    ctx0 = n_batch * lat_c

    def idx_f(bi, c):
        return (jnp.where(c < ctx_c, ctx0 + bi * ctx_c + c, bi * lat_c + (c - ctx_c)), 0)

    def idx_r(bi, c):
        return (jnp.where(c < ctx_c, ctx0 + bi * ctx_c + (ctx_c - 1 - c), bi * lat_c + (lat_c - 1 - (c - ctx_c))), 0)

    spec_f = pl.BlockSpec((n, RWKV_WIDTH), idx_f)
    spec_r = pl.BlockSpec((n, RWKV_WIDTH), idx_r)
    out = jax.ShapeDtypeStruct((nt, RWKV_WIDTH), F32)
    return pl.pallas_call(
        _scan_kernel,
        out_shape=(out, out),
        grid=(n_batch, ctx_c + lat_c),
        in_specs=[spec_f] * 6 + [spec_r] * 6,
        out_specs=(spec_f, spec_r),
        scratch_shapes=[pltpu.VMEM((2 * HEADS // HGROUP, CHUNK, HGROUP_W), F32)],
        compiler_params=_params(("parallel", "arbitrary")),
        name="rwkv_scan",
    )(v, *ops0, v, *ops1)


def _mix_out_kernel(*refs, lat_tiles):
    (yf_ref, yb_ref, g_ref, bon_ref, lng_ref, lnb_ref, ones_ref, woa_ref, wor_ref, xl_ref, xc_ref, mod_ref, pg_ref,
     att_ref) = refs[:14]
    o_ref = refs[-1]
    is_lat = pl.program_id(0) < lat_tiles
    ones_bd = ones_ref[...]
    y = yf_ref[...] + yb_ref[...]
    inv_n = 1.0 / RWKV_HEAD
    mu = _seg_sum(y, ones_bd) * inv_n
    yc = y - mu
    var = _seg_sum(yc * yc, ones_bd) * inv_n
    yn = yc * lax.rsqrt(var + LNX_EPS) * lng_ref[...] + lnb_ref[...]
    rw = (yn + bon_ref[...]) * g_ref[...]
    att = att_ref[...]
    if len(refs) == 16:
        att = jnp.where(is_lat, att, refs[14][...])
    mix = _dot(att, woa_ref[...]) + _dot(rw.astype(BF16), wor_ref[...])
    x = jnp.where(is_lat, xl_ref[...], xc_ref[...])
    o_ref[...] = x + mod_ref[0, 2:3, :] * _rms(mix, pg_ref[...])


def _mix_out(yf, yb, att_lat, att_ctx, g, bon, lnx_g, lnx_b, ones_bd, wo_att, wo_rw, x_lat, x_ctx, mods, post_g, *,
             rows, n_lat, seq):
    tm = PROJ_TILE
    lat_tiles = n_lat // tm
    n_batch = n_lat // seq
    w = RWKV_WIDTH

    def mod_idx(i):
        return (jnp.where(i < lat_tiles, (i * tm) // seq, n_batch), 0, 0)

    const = lambda i: (0, 0)
    row = lambda i: (i, 0)
    in_specs = [
        pl.BlockSpec((tm, w), row),
        pl.BlockSpec((tm, w), row),
        pl.BlockSpec((tm, w), row),
        pl.BlockSpec((tm, w), row),
        pl.BlockSpec((1, w), const),
        pl.BlockSpec((1, w), const),
        pl.BlockSpec((w, w), const),
        pl.BlockSpec((w, D_MODEL), const),
        pl.BlockSpec((w, D_MODEL), const),
        *_stream_specs(tm, lat_tiles, x_ctx is x_lat),
        pl.BlockSpec((1, N_MOD, D_MODEL), mod_idx),
        pl.BlockSpec((1, D_MODEL), const),
        pl.BlockSpec((tm, w), lambda i: (jnp.minimum(i, lat_tiles - 1), 0)),
    ]
    args = [yf, yb, g, bon, lnx_g, lnx_b, ones_bd, wo_att, wo_rw, x_lat, x_ctx, mods, post_g, att_lat]
    if att_ctx is not None:
        in_specs.append(pl.BlockSpec((tm, w), lambda i: (jnp.maximum(i - lat_tiles, 0), 0)))
        args.append(att_ctx)
    return pl.pallas_call(
        functools.partial(_mix_out_kernel, lat_tiles=lat_tiles),
        out_shape=jax.ShapeDtypeStruct((rows, D_MODEL), F32),
        grid=(rows // tm,),
        in_specs=in_specs,
        out_specs=pl.BlockSpec((tm, D_MODEL), row),
        compiler_params=_params(("parallel",)),
        name="mix_out",
    )(*args)


def _route(logits_t, bias_col):
    scores = jax.nn.sigmoid(logits_t[0:N_EXPERTS, :])
    biased = scores + bias_col
    s_rows = [scores[e:e + 1, :] for e in range(N_EXPERTS)]
    b_rows = [biased[e:e + 1, :] for e in range(N_EXPERTS)]
    npg = EXPERTS_PER_GROUP
    group_scores = []
    for gi in range(N_GROUPS):
        bg = b_rows[gi * npg:(gi + 1) * npg]
        best_pair = None
        for i in range(npg):
            for j in range(i + 1, npg):
                pair = bg[i] + bg[j]
                best_pair = pair if best_pair is None else jnp.maximum(best_pair, pair)
        group_scores.append(best_pair)
    best = group_scores[0]
    best_idx = jnp.zeros(best.shape, jnp.int32)
    for gi in range(1, N_GROUPS):
        upd = group_scores[gi] > best
        best = jnp.where(upd, group_scores[gi], best)
        best_idx = jnp.where(upd, gi, best_idx)
    pick = lambda rows, j: functools.reduce(
        lambda acc, gi: jnp.where(best_idx == gi, rows[gi * npg + j], acc), range(1, N_GROUPS), rows[j])
    bb = [pick(b_rows, j) for j in range(npg)]
    ss = [pick(s_rows, j) for j in range(npg)]
    weights = []
    for j in range(npg):
        rank = jnp.zeros(best.shape, jnp.int32)
        for i in range(npg):
            if i == j:
                continue
            beats = (bb[i] > bb[j]) | ((bb[i] == bb[j]) & (i < j)) if i < j else (bb[i] > bb[j])
            rank = rank + beats.astype(jnp.int32)
        weights.append(jnp.where(rank < 2, ss[j], 0.0))
    den = weights[0] + weights[1] + weights[2] + weights[3]
    return [wj / den for wj in weights], best_idx


def _swiglu(h_b, wgu, wd, gate=None):
    gu = _dot(h_b, wgu)
    g_part = gu[:, 0:EXPERT_HIDDEN]
    act = g_part * jax.nn.sigmoid(g_part) * gu[:, EXPERT_HIDDEN:]
    if gate is not None:
        act = act * gate
    return _dot(act.astype(BF16), wd)


def _moe_kernel(x_ref, mod_ref, pre_ref, post_ref, rw_ref, rb_ref, before_ref, wgu_ref, wd_ref, o_ref, sorted_scr,
                f_scr):
    tm = x_ref.shape[0]
    blk = MOE_BLOCK
    n_blocks = sorted_scr.shape[0] // blk
    x = x_ref[...]
    h = _rms(x, pre_ref[...]) * (1.0 + mod_ref[0, 4:5, :]) + mod_ref[0, 3:4, :]
    h_hi = h.astype(BF16)
    h_lo = (h - h_hi.astype(F32)).astype(BF16)
    rw = rw_ref[...]
    rw_hi = rw.astype(BF16)
    rw_lo = (rw - rw_hi.astype(F32)).astype(BF16)
    logits_t = _dot_nt(rw_hi, h_hi) + _dot_nt(rw_lo, h_hi) + _dot_nt(rw_hi, h_lo)
    gates, group = _route(logits_t, rb_ref[...])

    sub8 = lax.broadcasted_iota(jnp.int32, (SUBLANES, tm), 0)
    onehot = jnp.where(sub8 == group, 1.0, 0.0)
    before = _dot(onehot.astype(BF16), before_ref[...])
    counts = [jnp.sum(onehot[g:g + 1, :]).astype(jnp.int32) for g in range(N_GROUPS)]
    ends, acc = [], jnp.int32(0)
    for g in range(N_GROUPS):
        acc = acc + ((counts[g] + (blk - 1)) // blk) * blk
        ends.append(acc)
    starts = [jnp.int32(0)] + ends[:-1]
    pos = sum(onehot[g:g + 1, :] * (before[g:g + 1, :] + starts[g].astype(F32)) for g in range(N_GROUPS))

    sub = lax.broadcasted_iota(jnp.int32, (LANES, tm), 0)
    stack = jnp.where(sub == EXPERTS_PER_GROUP, pos, 0.0)
    for j in range(EXPERTS_PER_GROUP):
        stack = jnp.where(sub == j, gates[j], stack)
    cols = stack.T
    g_hi = cols.astype(BF16)
    g_lo = (cols - g_hi.astype(F32)).astype(BF16)
    pos_row = pos.astype(jnp.int32)
    pos_col = cols[:, EXPERTS_PER_GROUP:EXPERTS_PER_GROUP + 1].astype(jnp.int32)

    f_shared = _swiglu(h_hi, wgu_ref[N_EXPERTS], wd_ref[N_EXPERTS])

    for s in range(n_blocks):
        r0 = s * blk
        rows = pl.ds(r0, blk)
        grp = sum((r0 >= ends[g]).astype(jnp.int32) for g in range(N_GROUPS - 1))

        @pl.when(r0 < ends[-1])
        def _():
            r_i = lax.broadcasted_iota(jnp.int32, (blk, tm), 0) + r0
            take = jnp.where(r_i == pos_row, 1.0, 0.0).astype(BF16)
            h_s = _dot(take, h_hi).astype(BF16)
            g_s = _dot(take, g_hi) + _dot(take, g_lo)
            out = None
            for j in range(EXPERTS_PER_GROUP):
                e = grp * EXPERTS_PER_GROUP + j
                y = _swiglu(h_s, wgu_ref[e], wd_ref[e], g_s[:, j:j + 1])
                out = y if out is None else out + y
            sorted_scr[rows, :] = out.astype(BF16)

        @pl.when(r0 >= ends[-1])
        def _():
            sorted_scr[rows, :] = jnp.zeros((blk, D_MODEL), BF16)

    main, total = MOE_MAIN_ROWS, n_blocks * blk

    def put_back(c0, c1):
        c_i = lax.broadcasted_iota(jnp.int32, (tm, c1 - c0), 1) + c0
        put = jnp.where(c_i == pos_col, 1.0, 0.0).astype(BF16)
        return _dot(put, sorted_scr[c0:c1, :])

    f_scr[...] = f_shared + put_back(0, main)

    @pl.when(ends[-1] > main)
    def _():
        f_scr[...] += put_back(main, total)

    o_ref[...] = x + mod_ref[0, 5:6, :] * _rms(f_scr[...], post_ref[...])


def _moe(x_all, mods, pre_g, post_g, router_wt, router_b, before_m, wgu, wd, *, rows, n_lat, seq):
    tm = MOE_TILE
    lat_tiles = n_lat // tm
    n_batch = n_lat // seq
    n_e = wgu.shape[0]

    def mod_idx(i):
        return (jnp.where(i < lat_tiles, (i * tm) // seq, n_batch), 0, 0)

    const = lambda i: (0, 0)
    const3 = lambda i: (0, 0, 0)
    row = lambda i: (i, 0)
    resident = pl.Buffered(1)
    return pl.pallas_call(
        _moe_kernel,
        out_shape=jax.ShapeDtypeStruct((rows, D_MODEL), F32),
        grid=(rows // tm,),
        in_specs=[
            pl.BlockSpec((tm, D_MODEL), row),
            pl.BlockSpec((1, N_MOD, D_MODEL), mod_idx),
            pl.BlockSpec((1, D_MODEL), const),
            pl.BlockSpec((1, D_MODEL), const),
            pl.BlockSpec((LANES, D_MODEL), const),
            pl.BlockSpec((N_EXPERTS, 1), const),
            pl.BlockSpec((tm, tm), const),
            pl.BlockSpec((n_e, D_MODEL, 2 * EXPERT_HIDDEN), const3, pipeline_mode=resident),
            pl.BlockSpec((n_e, EXPERT_HIDDEN, D_MODEL), const3, pipeline_mode=resident),
        ],
        out_specs=pl.BlockSpec((tm, D_MODEL), row),
        scratch_shapes=[pltpu.VMEM((tm + N_GROUPS * MOE_BLOCK, D_MODEL), BF16), pltpu.VMEM((tm, D_MODEL), F32)],
        compiler_params=_params(("parallel",)),
        name="moe",
    )(x_all, mods, pre_g, post_g, router_wt, router_b, before_m, wgu, wd)


def _pack_in_proj(w_in, shift_mu):
    d = w_in.shape[0]
    z = lambda n: jnp.zeros((d, n), w_in.dtype)
    kr = w_in[:, Q_RANK + KV_RANK:MLA_COLS]
    half = MLA_ROPE // 2
    kr_sw = jnp.concatenate([kr[:, half:], kr[:, :half]], axis=1)
    pad = LANES - MLA_NOPE - MLA_ROPE
    packed = jnp.concatenate([
        w_in[:, :Q_RANK + KV_RANK],
        z(MLA_NOPE), kr, z(pad),
        z(MLA_NOPE), kr_sw, z(pad),
        w_in[:, MLA_COLS:], z(RWKV_PAD - RWKV_COLS),
    ], axis=1)
    mu = jnp.pad(shift_mu, ((0, 0), (0, RWKV_PAD - RWKV_COLS)))
    return packed.astype(BF16), mu


def _pack_mla(w_uq, w_ukv):
    half = MLA_ROPE // 2
    dq = MLA_NOPE + MLA_ROPE
    q3 = w_uq.reshape(Q_RANK, HEADS, dq)
    zq = lambda n: jnp.zeros((Q_RANK, HEADS, n), w_uq.dtype)
    pad = HEAD_SLOT - dq
    wq1 = jnp.concatenate([q3, zq(pad)], axis=2).reshape(Q_RANK, HEADS * HEAD_SLOT)
    wq2 = jnp.concatenate([zq(MLA_NOPE), q3[:, :, MLA_NOPE + half:], q3[:, :, MLA_NOPE:MLA_NOPE + half], zq(pad)],
                          axis=2).reshape(Q_RANK, HEADS * HEAD_SLOT)
    kv3 = w_ukv.reshape(KV_RANK, HEADS, MLA_NOPE + MLA_V)
    wuk = jnp.concatenate([kv3[:, :, :MLA_NOPE], jnp.zeros((KV_RANK, HEADS, HEAD_SLOT - MLA_NOPE), w_ukv.dtype)],
                          axis=2).reshape(KV_RANK, HEADS * HEAD_SLOT)
    wuv = kv3[:, :, MLA_NOPE:].reshape(KV_RANK, HEADS * MLA_V).T
    return wq1.astype(BF16), wq2.astype(BF16), wuk.astype(BF16), wuv.astype(BF16)


def _block_diag2(m):
    r, c = m.shape[1], m.shape[2]
    z = jnp.zeros((r, c), m.dtype)
    return jnp.concatenate([jnp.concatenate([m[0], z], axis=1), jnp.concatenate([z, m[1]], axis=1)], axis=0)


def _rope_tables(seq, ctx_len):
    axis_dim = MLA_ROPE // 2
    t = jnp.arange(seq, dtype=jnp.int32)
    row = (t // GRID_W).astype(F32)
    col = (t % GRID_W).astype(F32)
    inv_freq = ROPE_BASE ** (-jnp.arange(0, axis_dim, 2, dtype=F32) / axis_dim)
    ang = jnp.concatenate([row[:, None] * inv_freq, col[:, None] * inv_freq], axis=-1)
    cos = jnp.concatenate([jnp.cos(ang), jnp.ones((ctx_len, axis_dim), F32)], axis=0)
    sin = jnp.concatenate([jnp.sin(ang), jnp.zeros((ctx_len, axis_dim), F32)], axis=0)
    n = seq + ctx_len
    ones = jnp.ones((n, MLA_NOPE), F32)
    z_nope = jnp.zeros((n, MLA_NOPE), F32)
    z_pad = jnp.zeros((n, HEAD_SLOT - MLA_NOPE - MLA_ROPE), F32)
    cq = jnp.concatenate([ones, cos, cos, z_pad], axis=1)
    sq = jnp.concatenate([z_nope, -sin, sin, z_pad], axis=1)
    ck = jnp.concatenate([z_nope, cos, cos, z_pad], axis=1)
    return cq, sq, ck


def kernel(x, c, ctx, c_ctx, ada_w, ada_b, mix_pre_g, mix_post_g, ffn_pre_g, ffn_post_g, w_in, q_norm_g, kv_norm_g,
           w_uq, w_ukv, shift_mu, decay_w0, decay_w2, iclr_a0, iclr_a2, gate_g2, k_k, k_a, r_k, lnx_g, lnx_b, w_out,
           router_w, router_bias, exp_w_gate, exp_w_up, exp_w_down, sh_w_gate, sh_w_up, sh_w_down):
    n_batch, seq, d = x.shape
    ctx_len = ctx.shape[1]
    depth = ada_w.shape[0]
    assert d == D_MODEL and seq % TOKEN_TILE == 0 and ctx_len % TOKEN_TILE == 0 and seq % ctx_len == 0
    assert seq % GRID_W == 0 and n_batch + 1 <= SUBLANES
    n_lat = n_batch * seq
    n_ctx = n_batch * ctx_len
    assert seq % MOE_TILE == 0 and n_ctx % MOE_TILE == 0 and seq % PROJ_TILE == 0 and n_ctx % PROJ_TILE == 0

    x_lat, x_ctx = x.reshape(n_lat, d), ctx.reshape(n_ctx, d)
    cond_rows = jnp.concatenate([c, c_ctx[None], jnp.zeros((SUBLANES - n_batch - 1, d), F32)], axis=0)
    mods_all = _ada_modulation(cond_rows, ada_w, ada_b).reshape(depth, SUBLANES, N_MOD, d)

    cq_t, sq_t, ck_t = _rope_tables(seq, max(ctx_len, PROJ_TILE))
    w = RWKV_WIDTH
    ones_bd = (jnp.arange(w)[:, None] // RWKV_HEAD == jnp.arange(w)[None, :] // RWKV_HEAD).astype(BF16)
    ti = jnp.arange(TOKEN_TILE)
    same_chunk = (ti[:, None] // CHUNK) == (ti[None, :] // CHUNK)
    ones_c = same_chunk.astype(BF16)
    tri_f = jnp.logical_and(same_chunk, ti[None, :] <= ti[:, None]).astype(BF16)
    tri_r = jnp.logical_and(same_chunk, ti[None, :] >= ti[:, None]).astype(BF16)
    router_wt = jnp.pad(router_w.T, ((0, LANES - N_EXPERTS), (0, 0)))
    router_b = router_bias.reshape(N_EXPERTS, 1)
    tj = jnp.arange(MOE_TILE)
    before_m = (tj[:, None] < tj[None, :]).astype(BF16)
    row1 = lambda a: a.reshape(1, -1)

    for l in range(depth):
        ctx_out = l < depth - 1
        mods = mods_all[l]
        rows = n_lat + n_ctx if ctx_out else n_lat

        w_in_p, mu = _pack_in_proj(w_in[l], shift_mu[l])
        wq1, wq2, wuk, wuv = _pack_mla(w_uq[l], w_ukv[l])
        q, k, v, p_rw = _in_proj(x_lat, x_ctx, mods, row1(mix_pre_g[l]), w_in_p, row1(q_norm_g[l]),
                                 row1(kv_norm_g[l]), wq1, wq2, wuk, wuv, cq_t, sq_t, ck_t, n_lat=n_lat, n_ctx=n_ctx,
                                 seq=seq, ctx_len=ctx_len)
        att = _attention(q, k, v, n_batch=n_batch, seq=seq, ctx_len=ctx_len, latent=True)
        att_c = _attention(q, k, v, n_batch=n_batch, seq=seq, ctx_len=ctx_len, latent=False) if ctx_out else None

        g2p = jnp.pad(gate_g2[l], ((0, GATE_PAD - GATE_LORA), (0, 0))).astype(BF16)
        prep = _rwkv_prep(
            p_rw, mu, _block_diag2(decay_w2[l]).astype(BF16), decay_w0[l].reshape(1, 2 * w),
            _block_diag2(iclr_a2[l]).astype(BF16), iclr_a0[l].reshape(1, 2 * w), g2p,
            row1(k_k[l]), row1(k_a[l]), row1(r_k[l]), ones_bd, tri_f, tri_r, ones_c,
            n_lat=n_lat, seq=seq, ctx_len=ctx_len)
        vv, g, bon = prep[0], prep[11], prep[12]
        yf, yb = _rwkv_scan(vv, prep[1:6], prep[6:11], n_batch=n_batch, seq=seq, ctx_len=ctx_len)

        wo = w_out[l].astype(BF16)
        x_all = _mix_out(yf, yb, att, att_c, g, bon, row1(lnx_g[l]), row1(lnx_b[l]), ones_bd, wo[:w], wo[w:], x_lat,
                         x_ctx, mods, row1(mix_post_g[l]), rows=rows, n_lat=n_lat, seq=seq)

        wgu = jnp.concatenate([
            jnp.concatenate([exp_w_gate[l], exp_w_up[l]], axis=2),
            jnp.concatenate([sh_w_gate[l], sh_w_up[l]], axis=1)[None]], axis=0).astype(BF16)
        wd = jnp.concatenate([exp_w_down[l], sh_w_down[l][None]], axis=0).astype(BF16)
        x_all = _moe(x_all, mods, row1(ffn_pre_g[l]), row1(ffn_post_g[l]), router_wt, router_b, before_m, wgu, wd,
                     rows=rows, n_lat=n_lat, seq=seq)
        x_lat = x_ctx = x_all

    return x_all.reshape(n_batch, seq, d)
```

```python
import functools
import math

import jax
import jax.numpy as jnp
from jax import lax
from jax.experimental import pallas as pl
from jax.experimental.pallas import tpu as pltpu

F32 = jnp.float32
BF16 = jnp.bfloat16
HIGHEST = lax.Precision.HIGHEST

D_MODEL = 1024
N_MOD = 6
NORM_EPS = 1e-6
GRID_W = 64
ROPE_BASE = 10000.0

HEADS = 8
MLA_NOPE = 64
MLA_ROPE = 32
MLA_V = 64
Q_RANK = 256
KV_RANK = 128
MLA_COLS = Q_RANK + KV_RANK + MLA_ROPE

RWKV_HEAD = 64
RWKV_WIDTH = HEADS * RWKV_HEAD
DECAY_LORA = 64
ICLR_LORA = 64
GATE_LORA = 160
RWKV_COLS = 3 * RWKV_WIDTH + 2 * DECAY_LORA + 2 * ICLR_LORA + GATE_LORA
RWKV_PAD = 2048
GATE_PAD = 256
LNX_EPS = 64e-5

N_EXPERTS = 16
N_GROUPS = 4
EXPERTS_PER_GROUP = 4
EXPERT_HIDDEN = 256

LANES = 128
SUBLANES = 8
HEAD_SLOT = 128
IN_PACKED = Q_RANK + KV_RANK + 2 * LANES + RWKV_PAD
TOKEN_TILE = 256
PROJ_TILE = 512
ATTN_KV_BLOCK = 2048
ATTN_Q_TILE = 512
MOE_TILE = 512
MOE_BLOCK = 128
MOE_MAIN_ROWS = MOE_TILE + 2 * MOE_BLOCK
CHUNK = 64
INV_DIAG = 16
ADA_TILE = 1536
VMEM_LIMIT = 48 * 1024 * 1024


def _dot(a, b):
    return jnp.dot(a, b, preferred_element_type=F32)


def _dot_nt(a, b):
    return lax.dot_general(a, b, (((1,), (1,)), ((), ())), preferred_element_type=F32)


def _rms(x, g):
    return x * lax.rsqrt(jnp.mean(x * x, axis=-1, keepdims=True) + NORM_EPS) * g


def _seg_sum(x, ones_bd):
    hi = x.astype(BF16)
    lo = (x - hi.astype(F32)).astype(BF16)
    return _dot(hi, ones_bd) + _dot(lo, ones_bd)


def _params(sem):
    return pltpu.CompilerParams(dimension_semantics=sem, vmem_limit_bytes=VMEM_LIMIT)


def _ada_kernel(c_ref, w_ref, b_ref, o_ref):
    c = c_ref[...]
    cond = c * jax.nn.sigmoid(c)
    o_ref[0] = jnp.dot(cond, w_ref[0], precision=HIGHEST, preferred_element_type=F32) + b_ref[0]


def _ada_modulation(cond_rows, ada_w, ada_b):
    depth, d, n = ada_w.shape
    tn = ADA_TILE
    rows = cond_rows.shape[0]
    return pl.pallas_call(
        _ada_kernel,
        out_shape=jax.ShapeDtypeStruct((depth, rows, n), F32),
        grid=(depth, n // tn),
        in_specs=[
            pl.BlockSpec((rows, d), lambda l, j: (0, 0)),
            pl.BlockSpec((1, d, tn), lambda l, j: (l, 0, j)),
            pl.BlockSpec((1, 1, tn), lambda l, j: (l, 0, j)),
        ],
        out_specs=pl.BlockSpec((1, rows, tn), lambda l, j: (l, 0, j)),
        compiler_params=_params(("parallel", "parallel")),
        name="ada_modulation",
    )(cond_rows, ada_w, ada_b.reshape(depth, 1, n))


def _stream_specs(tm, lat_tiles, merged):
    off = lat_tiles if merged else 0
    return (pl.BlockSpec((tm, D_MODEL), lambda i: (jnp.minimum(i, lat_tiles - 1), 0)),
            pl.BlockSpec((tm, D_MODEL), lambda i: (jnp.maximum(i - lat_tiles, 0) + off, 0)))


def _in_proj_kernel(xl_ref, xc_ref, mod_ref, g_ref, win_ref, qng_ref, kvng_ref, wq1_ref, wq2_ref, wuk_ref, wuv_ref,
                    cq_ref, sq_ref, ck_ref, q_out, k_out, v_out, p_out, *, q_scale, lat_tiles):
    x = jnp.where(pl.program_id(0) < lat_tiles, xl_ref[...], xc_ref[...])
    shift = mod_ref[0, 0:1, :]
    scale = mod_ref[0, 1:2, :]
    h = _rms(x, g_ref[...]) * (1.0 + scale) + shift
    p = _dot(h.astype(BF16), win_ref[...])
    c_q = p[:, 0:Q_RANK]
    c_kv = p[:, Q_RANK:Q_RANK + KV_RANK]
    kr_a = p[:, Q_RANK + KV_RANK:Q_RANK + KV_RANK + LANES]
    kr_b = p[:, Q_RANK + KV_RANK + LANES:Q_RANK + KV_RANK + 2 * LANES]
    p_out[...] = p[:, Q_RANK + KV_RANK + 2 * LANES:]

    tile8 = lambda t: jnp.concatenate([t] * HEADS, axis=1)
    cqn = _rms(c_q, qng_ref[...]).astype(BF16)
    q = _dot(cqn, wq1_ref[...]) * tile8(cq_ref[...]) + _dot(cqn, wq2_ref[...]) * tile8(sq_ref[...])
    q_out[...] = (q * q_scale).astype(BF16)

    ckvn = _rms(c_kv, kvng_ref[...]).astype(BF16)
    k_rot = kr_a * ck_ref[...] + kr_b * sq_ref[...]
    k_out[...] = (_dot(ckvn, wuk_ref[...]) + tile8(k_rot)).astype(BF16)
    v_out[...] = _dot_nt(wuv_ref[...], ckvn).astype(BF16)


def _in_proj(x_lat, x_ctx, mods, g, w_in_p, qng, kvng, wq1, wq2, wuk, wuv, cq_t, sq_t, ck_t, *, n_lat, n_ctx, seq,
             ctx_len):
    nt = n_lat + n_ctx
    tm = PROJ_TILE
    lat_tiles = n_lat // tm
    n_batch = n_lat // seq

    def mod_idx(i):
        return (jnp.where(i < lat_tiles, (i * tm) // seq, n_batch), 0, 0)

    def tab_idx(i):
        return (jnp.where(i < lat_tiles, i % (seq // tm), seq // tm), 0)

    const = lambda i: (0, 0)
    row = lambda i: (i, 0)
    qw = HEADS * HEAD_SLOT
    return pl.pallas_call(
        functools.partial(_in_proj_kernel, q_scale=float((MLA_NOPE + MLA_ROPE) ** -0.5 * math.log2(math.e)),
                          lat_tiles=lat_tiles),
        out_shape=(
            jax.ShapeDtypeStruct((nt, qw), BF16),
            jax.ShapeDtypeStruct((nt, qw), BF16),
            jax.ShapeDtypeStruct((HEADS * MLA_V, nt), BF16),
            jax.ShapeDtypeStruct((nt, RWKV_PAD), F32),
        ),
        grid=(nt // tm,),
        in_specs=[
            *_stream_specs(tm, lat_tiles, x_ctx is x_lat),
            pl.BlockSpec((1, N_MOD, D_MODEL), mod_idx),
            pl.BlockSpec((1, D_MODEL), const),
            pl.BlockSpec((D_MODEL, IN_PACKED), const),
            pl.BlockSpec((1, Q_RANK), const),
            pl.BlockSpec((1, KV_RANK), const),
            pl.BlockSpec((Q_RANK, qw), const),
            pl.BlockSpec((Q_RANK, qw), const),
            pl.BlockSpec((KV_RANK, qw), const),
            pl.BlockSpec((HEADS * MLA_V, KV_RANK), const),
            pl.BlockSpec((tm, HEAD_SLOT), tab_idx),
            pl.BlockSpec((tm, HEAD_SLOT), tab_idx),
            pl.BlockSpec((tm, HEAD_SLOT), tab_idx),
        ],
        out_specs=(
            pl.BlockSpec((tm, qw), row),
            pl.BlockSpec((tm, qw), row),
            pl.BlockSpec((HEADS * MLA_V, tm), lambda i: (0, i)),
            pl.BlockSpec((tm, RWKV_PAD), row),
        ),
        compiler_params=_params(("parallel",)),
        name="in_proj",
    )(x_lat, x_ctx, mods, g, w_in_p, qng, kvng, wq1, wq2, wuk, wuv, cq_t, sq_t, ck_t)


def _attn_kernel(q_ref, kc_ref, vc_ref, *rest):
    o_ref = rest[-1]
    blocks = [(kc_ref, vc_ref, 0, kc_ref.shape[0])]
    if len(rest) == 3:
        kl_ref, vl_ref = rest[0], rest[1]
        kb = min(ATTN_KV_BLOCK, kl_ref.shape[0])
        blocks += [(kl_ref, vl_ref, s0, kb) for s0 in range(0, kl_ref.shape[0], kb)]
    heads = range(2)
    hs = [slice(h * HEAD_SLOT, (h + 1) * HEAD_SLOT) for h in heads]
    vs = [slice(h * MLA_V, (h + 1) * MLA_V) for h in heads]
    q = [q_ref[:, hs[h]] for h in heads]

    def scores(j):
        k_ref, _, s0, size = blocks[j]
        return [_dot_nt(k_ref[s0:s0 + size, hs[h]], q[h]) for h in heads]

    m, den, acc = [None] * 2, [None] * 2, [None] * 2
    s_cur = scores(0)
    for j, (_, vt_ref, s0, size) in enumerate(blocks):
        s_next = scores(j + 1) if j + 1 < len(blocks) else None
        for h in heads:
            s = s_cur[h]
            m_blk = jnp.max(s, axis=0, keepdims=True)
            if j == 0:
                m[h] = m_blk
                p = jnp.exp2(s - m_blk)
                den[h] = jnp.sum(p, axis=0, keepdims=True)
                acc[h] = _dot(vt_ref[vs[h], s0:s0 + size], p.astype(BF16))
            else:
                m_new = jnp.maximum(m[h], m_blk)
                alpha = jnp.exp2(m[h] - m_new)
                p = jnp.exp2(s - m_new)
                den[h] = alpha * den[h] + jnp.sum(p, axis=0, keepdims=True)
                acc[h] = alpha * acc[h] + _dot(vt_ref[vs[h], s0:s0 + size], p.astype(BF16))
                m[h] = m_new
        s_cur = s_next
    out_t = jnp.concatenate([acc[h] / den[h] for h in heads], axis=0)
    o_ref[...] = out_t.T.astype(o_ref.dtype)


def _attention(q, k, v_t, *, n_batch, seq, ctx_len, latent):
    tq = ATTN_Q_TILE if latent else TOKEN_TILE
    seg = seq if latent else ctx_len
    assert seg % tq == 0
    q_tiles = seg // tq
    q_blk0 = 0 if latent else (n_batch * seq) // tq
    ctx_blk0 = (n_batch * seq) // ctx_len
    kv_ctx = lambda b, hp, j: (ctx_blk0 + b, hp)
    kv_lat = lambda b, hp, j: (b, hp)
    in_specs = [
        pl.BlockSpec((tq, 2 * HEAD_SLOT), lambda b, hp, j: (q_blk0 + b * q_tiles + j, hp)),
        pl.BlockSpec((ctx_len, 2 * HEAD_SLOT), kv_ctx),
        pl.BlockSpec((2 * MLA_V, ctx_len), lambda b, hp, j: (hp, ctx_blk0 + b)),
    ]
    args = [q, k, v_t]
    if latent:
        in_specs += [pl.BlockSpec((seq, 2 * HEAD_SLOT), kv_lat),
                     pl.BlockSpec((2 * MLA_V, seq), lambda b, hp, j: (hp, b))]
        args += [k, v_t]
    return pl.pallas_call(
        _attn_kernel,
        out_shape=jax.ShapeDtypeStruct((n_batch * seg, HEADS * MLA_V), BF16),
        grid=(n_batch, HEADS // 2, q_tiles),
        in_specs=in_specs,
        out_specs=pl.BlockSpec((tq, 2 * MLA_V), lambda b, hp, j: (b * q_tiles + j, hp)),
        compiler_params=_params(("parallel", "parallel", "arbitrary")),
        name="attention_lat" if latent else "attention_ctx",
    )(*args)


def _split2(x):
    hi = x.astype(BF16)
    return hi, (x - hi.astype(F32)).astype(BF16)


def _chunk_transpose(x):
    n = CHUNK
    xt = x.T
    rows = []
    for c in range(x.shape[0] // n):
        rows.append(jnp.concatenate(
            [xt[h * n:(h + 1) * n, c * n:(c + 1) * n] for h in range(x.shape[1] // n)], axis=1))
    return jnp.concatenate(rows, axis=0)


def _rwkv_prep_kernel(p_ref, hp_ref, hn_ref, mu_ref, w2_ref, w0_ref, a2_ref, a0_ref, g2_ref, kk_ref, ka_ref, rk_ref,
                      ones_ref, trif_ref, trir_ref, onesc_ref, v_out, rt0_out, kt0_out, bt0_out, kb0_out, pc0_out,
                      rt1_out, kt1_out, bt1_out, kb1_out, pc1_out, g_out, bon_out, *, lat_tiles, seq_tiles,
                      ctx_tiles):
    i = pl.program_id(0)
    tm = p_ref.shape[0]
    is_lat = i < lat_tiles
    local = jnp.where(is_lat, i % seq_tiles, (i - lat_tiles) % ctx_tiles)
    seg = jnp.where(is_lat, seq_tiles, ctx_tiles)
    keep_prev = jnp.where(local == 0, 0.0, 1.0)
    keep_next = jnp.where(local == seg - 1, 0.0, 1.0)

    p = p_ref[...]
    row = lax.broadcasted_iota(jnp.int32, (tm, 1), 0)
    prev = jnp.where(row == 0, hp_ref[7:8, :] * keep_prev, pltpu.roll(p, 1, 0))
    nxt = jnp.where(row == tm - 1, hn_ref[0:1, :] * keep_next, pltpu.roll(p, tm - 1, 0))
    ps = p + mu_ref[0:1, :] * (prev - p) + mu_ref[1:2, :] * (nxt - p)

    w = RWKV_WIDTH
    r = ps[:, 0:w]
    k = ps[:, w:2 * w]
    v = ps[:, 2 * w:3 * w]
    wl = ps[:, 3 * w:3 * w + 2 * DECAY_LORA]
    al = ps[:, 3 * w + 2 * DECAY_LORA:3 * w + 2 * DECAY_LORA + 2 * ICLR_LORA]
    gl = ps[:, 3 * w + 2 * DECAY_LORA + 2 * ICLR_LORA:3 * w + 2 * DECAY_LORA + 2 * ICLR_LORA + GATE_PAD]

    g_out[...] = _dot(jax.nn.sigmoid(gl).astype(BF16), g2_ref[...])
    z = w0_ref[...] + _dot(jnp.tanh(wl).astype(BF16), w2_ref[...])
    logw = (-math.exp(-0.5) * math.log2(math.e)) * jax.nn.sigmoid(z)
    a = jax.nn.sigmoid(a0_ref[...] + _dot(al.astype(BF16), a2_ref[...]))

    ones_bd = ones_ref[...]
    kk = k * kk_ref[...]
    kk = kk * lax.rsqrt(_seg_sum(kk * kk, ones_bd) + 1e-12)
    ka = ka_ref[...]
    v_out[...] = v.astype(BF16)

    ke_sum = None
    for d, tri_ref, (rt_out, kt_out, bt_out, kb_out, pc_out) in (
            (0, trif_ref, (rt0_out, kt0_out, bt0_out, kb0_out, pc0_out)),
            (1, trir_ref, (rt1_out, kt1_out, bt1_out, kb1_out, pc1_out))):
        lw = logw[:, d * w:(d + 1) * w]
        a_d = a[:, d * w:(d + 1) * w]
        lw_hi, lw_lo = _split2(lw)
        cum = _dot(tri_ref[...], lw_hi) + _dot(tri_ref[...], lw_lo)
        tot = _dot(onesc_ref[...], lw_hi) + _dot(onesc_ref[...], lw_lo)
        e_neg = jnp.exp2(-cum)
        ke = k * (1.0 + (a_d - 1.0) * ka)
        ke_sum = ke if ke_sum is None else ke_sum + ke
        rt_out[...] = (r * jnp.exp2(cum)).astype(BF16)
        kt_out[...] = (kk * jnp.exp2(cum - lw)).astype(BF16)
        bt_out[...] = _chunk_transpose(a_d * kk * e_neg).astype(BF16)
        kb_out[...] = _chunk_transpose(ke * e_neg).astype(BF16)
        pc_out[...] = _chunk_transpose(jnp.exp2(tot))
    bon_out[...] = _seg_sum(r * ke_sum * rk_ref[...], ones_bd) * v


def _rwkv_prep(p_rw, mu, w2bd, w0, a2bd, a0, g2p, k_k, k_a, r_k, ones_bd, tri_f, tri_r, ones_c, *, n_lat, seq,
               ctx_len):
    nt = p_rw.shape[0]
    tm = TOKEN_TILE
    n_tiles = nt // tm
    halo = SUBLANES
    blocks8 = nt // halo
    const = lambda i: (0, 0)
    row = lambda i: (i, 0)
    w = RWKV_WIDTH
    o16 = jax.ShapeDtypeStruct((nt, w), BF16)
    o32 = jax.ShapeDtypeStruct((nt, w), F32)
    return pl.pallas_call(
        functools.partial(_rwkv_prep_kernel, lat_tiles=n_lat // tm, seq_tiles=seq // tm, ctx_tiles=ctx_len // tm),
        out_shape=(o16, o16, o16, o16, o16, o32, o16, o16, o16, o16, o32, o32, o32),
        grid=(n_tiles,),
        in_specs=[
            pl.BlockSpec((tm, RWKV_PAD), row),
            pl.BlockSpec((halo, RWKV_PAD), lambda i: (jnp.maximum(i * (tm // halo) - 1, 0), 0)),
            pl.BlockSpec((halo, RWKV_PAD), lambda i: (jnp.minimum((i + 1) * (tm // halo), blocks8 - 1), 0)),
            pl.BlockSpec((2, RWKV_PAD), const),
            pl.BlockSpec((2 * DECAY_LORA, 2 * w), const),
            pl.BlockSpec((1, 2 * w), const),
            pl.BlockSpec((2 * ICLR_LORA, 2 * w), const),
            pl.BlockSpec((1, 2 * w), const),
            pl.BlockSpec((GATE_PAD, w), const),
            pl.BlockSpec((1, w), const),
            pl.BlockSpec((1, w), const),
            pl.BlockSpec((1, w), const),
            pl.BlockSpec((w, w), const),
            pl.BlockSpec((tm, tm), const),
            pl.BlockSpec((tm, tm), const),
            pl.BlockSpec((tm, tm), const),
        ],
        out_specs=(pl.BlockSpec((tm, w), row),) * 13,
        compiler_params=_params(("parallel",)),
        name="rwkv_prep",
    )(p_rw, p_rw, p_rw, mu, w2bd, w0, a2bd, a0, g2p, k_k, k_a, r_k, ones_bd, tri_f, tri_r, ones_c)


HGROUP = 2
HGROUP_W = HGROUP * RWKV_HEAD
SCAN_CHUNKS = 4


def _scan_kernel(v0_ref, rt0_ref, kt0_ref, bt0_ref, kb0_ref, pc0_ref, v1_ref, rt1_ref, kt1_ref, bt1_ref, kb1_ref,
                 pc1_ref, y0_ref, y1_ref, h_ref):
    @pl.when(pl.program_id(1) == 0)
    def _():
        h_ref[...] = jnp.zeros_like(h_ref)

    n, qw = CHUNK, HGROUP_W
    t_i = lax.broadcasted_iota(jnp.int32, (n, qw), 0)
    s_i = lax.broadcasted_iota(jnp.int32, (n, qw), 1) % n
    bd_mask = (lax.broadcasted_iota(jnp.int32, (qw, qw), 0) // n) == (lax.broadcasted_iota(jnp.int32, (qw, qw), 1) // n)
    eye = jnp.where(t_i == s_i, 1.0, 0.0)
    same16 = (t_i // INV_DIAG) == (s_i // INV_DIAG)
    same32 = (t_i // (2 * INV_DIAG)) == (s_i // (2 * INV_DIAG))
    sib32 = jnp.logical_and(same32, jnp.logical_not(same16))
    sib64 = jnp.logical_not(same32)
    incl = (s_i <= t_i, s_i >= t_i)
    strict = (s_i < t_i, s_i > t_i)

    groups = [(d, q, j) for j in range(SCAN_CHUNKS) for d in (0, 1) for q in range(HEADS // HGROUP)]
    dirs = [d for d, _, _ in groups]
    ng = range(len(groups))
    row0 = lambda d, j: (SCAN_CHUNKS - 1 - j) * n if d else j * n
    refs = ((v0_ref, rt0_ref, kt0_ref, bt0_ref, kb0_ref, pc0_ref), (v1_ref, rt1_ref, kt1_ref, bt1_ref, kb1_ref, pc1_ref))
    ld = lambda k: [refs[d][k][row0(d, j):row0(d, j) + n, q * qw:(q + 1) * qw] for d, q, j in groups]
    v, rt, kt, bt_t, kb_t, pc = (ld(k) for k in range(6))

    def bd(x):
        return jnp.where(bd_mask, jnp.concatenate([x.astype(BF16)] * HGROUP, axis=0), jnp.zeros((), BF16))

    mm = lambda x, w: _dot(x.astype(BF16), w)
    rows2 = lambda x, y: jnp.concatenate([x.astype(BF16), y.astype(BF16)], axis=0)

    bd2 = lambda x, y: jnp.concatenate([bd(x), bd(y)], axis=1)
    gram = [_dot(rows2(kt[g], rt[g]), bd2(bt_t[g], kb_t[g])) for g in ng]
    a_m = [jnp.where(strict[dirs[g]], gram[g][0:n, 0:qw], 0.0) for g in ng]
    c_m = [jnp.where(incl[dirs[g]], gram[g][n:2 * n, 0:qw], 0.0) for g in ng]
    b_m = [jnp.where(strict[dirs[g]], gram[g][0:n, qw:2 * qw], 0.0) for g in ng]
    e_m = [jnp.where(incl[dirs[g]], gram[g][n:2 * n, qw:2 * qw], 0.0) for g in ng]

    dd = [jnp.where(same16, a_m[g], 0.0) for g in ng]
    pw = [mm(dd[g], bd(dd[g])) for g in ng]
    t = [eye - dd[g] for g in ng]
    for _ in range(2):
        both = [_dot(rows2(t[g], pw[g]), bd(pw[g])) for g in ng]
        t = [t[g] + both[g][0:n] for g in ng]
        pw = [both[g][n:2 * n] for g in ng]
    t = [t[g] + mm(t[g], bd(pw[g])) for g in ng]
    for sib in (sib32, sib64):
        ta = [mm(t[g], bd(jnp.where(sib, a_m[g], 0.0))) for g in ng]
        t = [t[g] - mm(ta[g], bd(t[g])) for g in ng]

    bev = [_dot(jnp.concatenate([rows2(b_m[g], e_m[g]), kb_t[g]], axis=0), bd(v[g])) for g in ng]
    kv = [bev[g][2 * n:3 * n] for g in ng]
    mw = [mm(t[g], bd2(kt[g], bev[g][0:n])) for g in ng]
    m1 = [mw[g][:, 0:qw] for g in ng]
    w2 = [mw[g][:, qw:2 * qw] for g in ng]
    cq = [mm(c_m[g], bd2(m1[g], w2[g])) for g in ng]
    qh = [rt[g].astype(F32) - cq[g][:, 0:qw] for g in ng]
    y0 = [bev[g][n:2 * n] - cq[g][:, qw:2 * qw] for g in ng]

    chains = len(groups) // SCAN_CHUNKS
    h = [h_ref[s] for s in range(chains)]
    for g, (d, q, j) in enumerate(groups):
        s = g % chains
        uy = _dot(rows2(m1[g], qh[g]), bd(h[s]))
        (y0_ref, y1_ref)[d][row0(d, j):row0(d, j) + n, q * qw:(q + 1) * qw] = uy[n:2 * n] + y0[g]
        h[s] = pc[g] * (h[s] - _dot(bt_t[g], bd(uy[0:n] + w2[g])) + kv[g])
    for s in range(chains):
        h_ref[s] = h[s]


def _rwkv_scan(v, ops0, ops1, *, n_batch, seq, ctx_len):
    nt = v.shape[0]
    n = CHUNK * SCAN_CHUNKS
    assert seq % n == 0 and ctx_len % n == 0
    lat_c = seq // n
    ctx_c = ctx_len // n
    ctx0 = n_batch * lat_c

    def idx_f(bi, c):
        return (jnp.where(c < ctx_c, ctx0 + bi * ctx_c + c, bi * lat_c + (c - ctx_c)), 0)

    def idx_r(bi, c):
        return (jnp.where(c < ctx_c, ctx0 + bi * ctx_c + (ctx_c - 1 - c), bi * lat_c + (lat_c - 1 - (c - ctx_c))), 0)

    spec_f = pl.BlockSpec((n, RWKV_WIDTH), idx_f)
    spec_r = pl.BlockSpec((n, RWKV_WIDTH), idx_r)
    out = jax.ShapeDtypeStruct((nt, RWKV_WIDTH), F32)
    return pl.pallas_call(
        _scan_kernel,
        out_shape=(out, out),
        grid=(n_batch, ctx_c + lat_c),
        in_specs=[spec_f] * 6 + [spec_r] * 6,
        out_specs=(spec_f, spec_r),
        scratch_shapes=[pltpu.VMEM((2 * HEADS // HGROUP, CHUNK, HGROUP_W), F32)],
        compiler_params=_params(("parallel", "arbitrary")),
        name="rwkv_scan",
    )(v, *ops0, v, *ops1)


def _mix_out_kernel(*refs, lat_tiles):
    (yf_ref, yb_ref, g_ref, bon_ref, lng_ref, lnb_ref, ones_ref, woa_ref, wor_ref, xl_ref, xc_ref, mod_ref, pg_ref,
     att_ref) = refs[:14]
    o_ref = refs[-1]
    is_lat = pl.program_id(0) < lat_tiles
    ones_bd = ones_ref[...]
    y = yf_ref[...] + yb_ref[...]
    inv_n = 1.0 / RWKV_HEAD
    mu = _seg_sum(y, ones_bd) * inv_n
    yc = y - mu
    var = _seg_sum(yc * yc, ones_bd) * inv_n
    yn = yc * lax.rsqrt(var + LNX_EPS) * lng_ref[...] + lnb_ref[...]
    rw = (yn + bon_ref[...]) * g_ref[...]
    att = att_ref[...]
    if len(refs) == 16:
        att = jnp.where(is_lat, att, refs[14][...])
    mix = _dot(att, woa_ref[...]) + _dot(rw.astype(BF16), wor_ref[...])
    x = jnp.where(is_lat, xl_ref[...], xc_ref[...])
    o_ref[...] = x + mod_ref[0, 2:3, :] * _rms(mix, pg_ref[...])


def _mix_out(yf, yb, att_lat, att_ctx, g, bon, lnx_g, lnx_b, ones_bd, wo_att, wo_rw, x_lat, x_ctx, mods, post_g, *,
             rows, n_lat, seq):
    tm = PROJ_TILE
    lat_tiles = n_lat // tm
    n_batch = n_lat // seq
    w = RWKV_WIDTH

    def mod_idx(i):
        return (jnp.where(i < lat_tiles, (i * tm) // seq, n_batch), 0, 0)

    const = lambda i: (0, 0)
    row = lambda i: (i, 0)
    in_specs = [
        pl.BlockSpec((tm, w), row),
        pl.BlockSpec((tm, w), row),
        pl.BlockSpec((tm, w), row),
        pl.BlockSpec((tm, w), row),
        pl.BlockSpec((1, w), const),
        pl.BlockSpec((1, w), const),
        pl.BlockSpec((w, w), const),
        pl.BlockSpec((w, D_MODEL), const),
        pl.BlockSpec((w, D_MODEL), const),
        *_stream_specs(tm, lat_tiles, x_ctx is x_lat),
        pl.BlockSpec((1, N_MOD, D_MODEL), mod_idx),
        pl.BlockSpec((1, D_MODEL), const),
        pl.BlockSpec((tm, w), lambda i: (jnp.minimum(i, lat_tiles - 1), 0)),
    ]
    args = [yf, yb, g, bon, lnx_g, lnx_b, ones_bd, wo_att, wo_rw, x_lat, x_ctx, mods, post_g, att_lat]
    if att_ctx is not None:
        in_specs.append(pl.BlockSpec((tm, w), lambda i: (jnp.maximum(i - lat_tiles, 0), 0)))
        args.append(att_ctx)
    return pl.pallas_call(
        functools.partial(_mix_out_kernel, lat_tiles=lat_tiles),
        out_shape=jax.ShapeDtypeStruct((rows, D_MODEL), F32),
        grid=(rows // tm,),
        in_specs=in_specs,
        out_specs=pl.BlockSpec((tm, D_MODEL), row),
        compiler_params=_params(("parallel",)),
        name="mix_out",
    )(*args)


def _route(logits_t, bias_col):
    scores = jax.nn.sigmoid(logits_t[0:N_EXPERTS, :])
    biased = scores + bias_col
    s_rows = [scores[e:e + 1, :] for e in range(N_EXPERTS)]
    b_rows = [biased[e:e + 1, :] for e in range(N_EXPERTS)]
    npg = EXPERTS_PER_GROUP
    group_scores = []
    for gi in range(N_GROUPS):
        bg = b_rows[gi * npg:(gi + 1) * npg]
        best_pair = None
        for i in range(npg):
            for j in range(i + 1, npg):
                pair = bg[i] + bg[j]
                best_pair = pair if best_pair is None else jnp.maximum(best_pair, pair)
        group_scores.append(best_pair)
    best = group_scores[0]
    best_idx = jnp.zeros(best.shape, jnp.int32)
    for gi in range(1, N_GROUPS):
        upd = group_scores[gi] > best
        best = jnp.where(upd, group_scores[gi], best)
        best_idx = jnp.where(upd, gi, best_idx)
    pick = lambda rows, j: functools.reduce(
        lambda acc, gi: jnp.where(best_idx == gi, rows[gi * npg + j], acc), range(1, N_GROUPS), rows[j])
    bb = [pick(b_rows, j) for j in range(npg)]
    ss = [pick(s_rows, j) for j in range(npg)]
    weights = []
    for j in range(npg):
        rank = jnp.zeros(best.shape, jnp.int32)
        for i in range(npg):
            if i == j:
                continue
            beats = (bb[i] > bb[j]) | ((bb[i] == bb[j]) & (i < j)) if i < j else (bb[i] > bb[j])
            rank = rank + beats.astype(jnp.int32)
        weights.append(jnp.where(rank < 2, ss[j], 0.0))
    den = weights[0] + weights[1] + weights[2] + weights[3]
    return [wj / den for wj in weights], best_idx


def _swiglu(h_b, wg, wu, wd, gate=None):
    g_part = _dot(h_b, wg)
    act = g_part * jax.nn.sigmoid(g_part) * _dot(h_b, wu)
    if gate is not None:
        act = act * gate
    return _dot(act.astype(BF16), wd)


def _moe_kernel(x_ref, mod_ref, pre_ref, post_ref, rw_ref, rb_ref, before_ref, wg_ref, wu_ref, wd_ref, sg_ref,
                su_ref, sd_ref, o_ref, sorted_scr, f_scr):
    tm = x_ref.shape[0]
    blk = MOE_BLOCK
    n_blocks = sorted_scr.shape[0] // blk
    x = x_ref[...]
    h = _rms(x, pre_ref[...]) * (1.0 + mod_ref[0, 4:5, :]) + mod_ref[0, 3:4, :]
    h_hi = h.astype(BF16)
    h_lo = (h - h_hi.astype(F32)).astype(BF16)
    rw = rw_ref[...]
    rw_hi = rw.astype(BF16)
    rw_lo = (rw - rw_hi.astype(F32)).astype(BF16)
    logits_t = _dot_nt(rw_hi, h_hi) + _dot_nt(rw_lo, h_hi) + _dot_nt(rw_hi, h_lo)
    gates, group = _route(logits_t, rb_ref[...])

    sub8 = lax.broadcasted_iota(jnp.int32, (SUBLANES, tm), 0)
    onehot = jnp.where(sub8 == group, 1.0, 0.0)
    before = _dot(onehot.astype(BF16), before_ref[...])
    counts = [jnp.sum(onehot[g:g + 1, :]).astype(jnp.int32) for g in range(N_GROUPS)]
    ends, acc = [], jnp.int32(0)
    for g in range(N_GROUPS):
        acc = acc + ((counts[g] + (blk - 1)) // blk) * blk
        ends.append(acc)
    starts = [jnp.int32(0)] + ends[:-1]
    pos = sum(onehot[g:g + 1, :] * (before[g:g + 1, :] + starts[g].astype(F32)) for g in range(N_GROUPS))

    sub = lax.broadcasted_iota(jnp.int32, (LANES, tm), 0)
    stack = jnp.where(sub == EXPERTS_PER_GROUP, pos, 0.0)
    for j in range(EXPERTS_PER_GROUP):
        stack = jnp.where(sub == j, gates[j], stack)
    cols = stack.T
    g_hi = cols.astype(BF16)
    g_lo = (cols - g_hi.astype(F32)).astype(BF16)
    pos_row = pos.astype(jnp.int32)
    pos_col = cols[:, EXPERTS_PER_GROUP:EXPERTS_PER_GROUP + 1].astype(jnp.int32)

    f_shared = _swiglu(h_hi, sg_ref[...], su_ref[...], sd_ref[...])

    for s in range(n_blocks):
        r0 = s * blk
        rows = pl.ds(r0, blk)
        grp = sum((r0 >= ends[g]).astype(jnp.int32) for g in range(N_GROUPS - 1))

        @pl.when(r0 < ends[-1])
        def _():
            r_i = lax.broadcasted_iota(jnp.int32, (blk, tm), 0) + r0
            take = jnp.where(r_i == pos_row, 1.0, 0.0).astype(BF16)
            h_s = _dot(take, h_hi).astype(BF16)
            g_s = _dot(take, g_hi) + _dot(take, g_lo)
            out = None
            for j in range(EXPERTS_PER_GROUP):
                e = grp * EXPERTS_PER_GROUP + j
                y = _swiglu(h_s, wg_ref[e], wu_ref[e], wd_ref[e], g_s[:, j:j + 1])
                out = y if out is None else out + y
            sorted_scr[rows, :] = out.astype(BF16)

        @pl.when(r0 >= ends[-1])
        def _():
            sorted_scr[rows, :] = jnp.zeros((blk, D_MODEL), BF16)

    main, total = MOE_MAIN_ROWS, n_blocks * blk

    def put_back(c0, c1):
        c_i = lax.broadcasted_iota(jnp.int32, (tm, c1 - c0), 1) + c0
        put = jnp.where(c_i == pos_col, 1.0, 0.0).astype(BF16)
        return _dot(put, sorted_scr[c0:c1, :])

    f_scr[...] = f_shared + put_back(0, main)

    @pl.when(ends[-1] > main)
    def _():
        f_scr[...] += put_back(main, total)

    o_ref[...] = x + mod_ref[0, 5:6, :] * _rms(f_scr[...], post_ref[...])


def _moe(x_all, mods, pre_g, post_g, router_wt, router_b, before_m, experts, shared, *, rows, n_lat, seq):
    tm = MOE_TILE
    lat_tiles = n_lat // tm
    n_batch = n_lat // seq
    n_e = experts[0].shape[0]

    def mod_idx(i):
        return (jnp.where(i < lat_tiles, (i * tm) // seq, n_batch), 0, 0)

    const = lambda i: (0, 0)
    const3 = lambda i: (0, 0, 0)
    row = lambda i: (i, 0)
    resident = pl.Buffered(1)
    return pl.pallas_call(
        _moe_kernel,
        out_shape=jax.ShapeDtypeStruct((rows, D_MODEL), F32),
        grid=(rows // tm,),
        in_specs=[
            pl.BlockSpec((tm, D_MODEL), row),
            pl.BlockSpec((1, N_MOD, D_MODEL), mod_idx),
            pl.BlockSpec((1, D_MODEL), const),
            pl.BlockSpec((1, D_MODEL), const),
            pl.BlockSpec((LANES, D_MODEL), const),
            pl.BlockSpec((N_EXPERTS, 1), const),
            pl.BlockSpec((tm, tm), const),
            pl.BlockSpec((n_e, D_MODEL, EXPERT_HIDDEN), const3, pipeline_mode=resident),
            pl.BlockSpec((n_e, D_MODEL, EXPERT_HIDDEN), const3, pipeline_mode=resident),
            pl.BlockSpec((n_e, EXPERT_HIDDEN, D_MODEL), const3, pipeline_mode=resident),
            pl.BlockSpec((D_MODEL, EXPERT_HIDDEN), const, pipeline_mode=resident),
            pl.BlockSpec((D_MODEL, EXPERT_HIDDEN), const, pipeline_mode=resident),
            pl.BlockSpec((EXPERT_HIDDEN, D_MODEL), const, pipeline_mode=resident),
        ],
        out_specs=pl.BlockSpec((tm, D_MODEL), row),
        scratch_shapes=[pltpu.VMEM((tm + N_GROUPS * MOE_BLOCK, D_MODEL), BF16), pltpu.VMEM((tm, D_MODEL), F32)],
        compiler_params=_params(("parallel",)),
        name="moe",
    )(x_all, mods, pre_g, post_g, router_wt, router_b, before_m, *experts, *shared)


def _pack_in_proj(w_in, shift_mu):
    d = w_in.shape[0]
    z = lambda n: jnp.zeros((d, n), w_in.dtype)
    kr = w_in[:, Q_RANK + KV_RANK:MLA_COLS]
    half = MLA_ROPE // 2
    kr_sw = jnp.concatenate([kr[:, half:], kr[:, :half]], axis=1)
    pad = LANES - MLA_NOPE - MLA_ROPE
    packed = jnp.concatenate([
        w_in[:, :Q_RANK + KV_RANK],
        z(MLA_NOPE), kr, z(pad),
        z(MLA_NOPE), kr_sw, z(pad),
        w_in[:, MLA_COLS:], z(RWKV_PAD - RWKV_COLS),
    ], axis=1)
    mu = jnp.pad(shift_mu, ((0, 0), (0, RWKV_PAD - RWKV_COLS)))
    return packed.astype(BF16), mu


def _pack_mla(w_uq, w_ukv):
    half = MLA_ROPE // 2
    dq = MLA_NOPE + MLA_ROPE
    q3 = w_uq.reshape(Q_RANK, HEADS, dq)
    zq = lambda n: jnp.zeros((Q_RANK, HEADS, n), w_uq.dtype)
    pad = HEAD_SLOT - dq
    wq1 = jnp.concatenate([q3, zq(pad)], axis=2).reshape(Q_RANK, HEADS * HEAD_SLOT)
    wq2 = jnp.concatenate([zq(MLA_NOPE), q3[:, :, MLA_NOPE + half:], q3[:, :, MLA_NOPE:MLA_NOPE + half], zq(pad)],
                          axis=2).reshape(Q_RANK, HEADS * HEAD_SLOT)
    kv3 = w_ukv.reshape(KV_RANK, HEADS, MLA_NOPE + MLA_V)
    wuk = jnp.concatenate([kv3[:, :, :MLA_NOPE], jnp.zeros((KV_RANK, HEADS, HEAD_SLOT - MLA_NOPE), w_ukv.dtype)],
                          axis=2).reshape(KV_RANK, HEADS * HEAD_SLOT)
    wuv = kv3[:, :, MLA_NOPE:].reshape(KV_RANK, HEADS * MLA_V).T
    return wq1.astype(BF16), wq2.astype(BF16), wuk.astype(BF16), wuv.astype(BF16)


def _block_diag2(m):
    r, c = m.shape[1], m.shape[2]
    z = jnp.zeros((r, c), m.dtype)
    return jnp.concatenate([jnp.concatenate([m[0], z], axis=1), jnp.concatenate([z, m[1]], axis=1)], axis=0)


def _rope_tables(seq, ctx_len):
    axis_dim = MLA_ROPE // 2
    t = jnp.arange(seq, dtype=jnp.int32)
    row = (t // GRID_W).astype(F32)
    col = (t % GRID_W).astype(F32)
    inv_freq = ROPE_BASE ** (-jnp.arange(0, axis_dim, 2, dtype=F32) / axis_dim)
    ang = jnp.concatenate([row[:, None] * inv_freq, col[:, None] * inv_freq], axis=-1)
    cos = jnp.concatenate([jnp.cos(ang), jnp.ones((ctx_len, axis_dim), F32)], axis=0)
    sin = jnp.concatenate([jnp.sin(ang), jnp.zeros((ctx_len, axis_dim), F32)], axis=0)
    n = seq + ctx_len
    ones = jnp.ones((n, MLA_NOPE), F32)
    z_nope = jnp.zeros((n, MLA_NOPE), F32)
    z_pad = jnp.zeros((n, HEAD_SLOT - MLA_NOPE - MLA_ROPE), F32)
    cq = jnp.concatenate([ones, cos, cos, z_pad], axis=1)
    sq = jnp.concatenate([z_nope, -sin, sin, z_pad], axis=1)
    ck = jnp.concatenate([z_nope, cos, cos, z_pad], axis=1)
    return cq, sq, ck


def kernel(x, c, ctx, c_ctx, ada_w, ada_b, mix_pre_g, mix_post_g, ffn_pre_g, ffn_post_g, w_in, q_norm_g, kv_norm_g,
           w_uq, w_ukv, shift_mu, decay_w0, decay_w2, iclr_a0, iclr_a2, gate_g2, k_k, k_a, r_k, lnx_g, lnx_b, w_out,
           router_w, router_bias, exp_w_gate, exp_w_up, exp_w_down, sh_w_gate, sh_w_up, sh_w_down):
    n_batch, seq, d = x.shape
    ctx_len = ctx.shape[1]
    depth = ada_w.shape[0]
    assert d == D_MODEL and seq % TOKEN_TILE == 0 and ctx_len % TOKEN_TILE == 0 and seq % ctx_len == 0
    assert seq % GRID_W == 0 and n_batch + 1 <= SUBLANES
    n_lat = n_batch * seq
    n_ctx = n_batch * ctx_len
    assert seq % MOE_TILE == 0 and n_ctx % MOE_TILE == 0 and seq % PROJ_TILE == 0 and n_ctx % PROJ_TILE == 0

    x_lat, x_ctx = x.reshape(n_lat, d), ctx.reshape(n_ctx, d)
    cond_rows = jnp.concatenate([c, c_ctx[None], jnp.zeros((SUBLANES - n_batch - 1, d), F32)], axis=0)
    mods_all = _ada_modulation(cond_rows, ada_w, ada_b).reshape(depth, SUBLANES, N_MOD, d)

    cq_t, sq_t, ck_t = _rope_tables(seq, max(ctx_len, PROJ_TILE))
    w = RWKV_WIDTH
    ones_bd = (jnp.arange(w)[:, None] // RWKV_HEAD == jnp.arange(w)[None, :] // RWKV_HEAD).astype(BF16)
    ti = jnp.arange(TOKEN_TILE)
    same_chunk = (ti[:, None] // CHUNK) == (ti[None, :] // CHUNK)
    ones_c = same_chunk.astype(BF16)
    tri_f = jnp.logical_and(same_chunk, ti[None, :] <= ti[:, None]).astype(BF16)
    tri_r = jnp.logical_and(same_chunk, ti[None, :] >= ti[:, None]).astype(BF16)
    router_wt = jnp.pad(router_w.T, ((0, LANES - N_EXPERTS), (0, 0)))
    router_b = router_bias.reshape(N_EXPERTS, 1)
    tj = jnp.arange(MOE_TILE)
    before_m = (tj[:, None] < tj[None, :]).astype(BF16)
    row1 = lambda a: a.reshape(1, -1)

    for l in range(depth):
        ctx_out = l < depth - 1
        mods = mods_all[l]
        rows = n_lat + n_ctx if ctx_out else n_lat

        w_in_p, mu = _pack_in_proj(w_in[l], shift_mu[l])
        wq1, wq2, wuk, wuv = _pack_mla(w_uq[l], w_ukv[l])
        q, k, v, p_rw = _in_proj(x_lat, x_ctx, mods, row1(mix_pre_g[l]), w_in_p, row1(q_norm_g[l]),
                                 row1(kv_norm_g[l]), wq1, wq2, wuk, wuv, cq_t, sq_t, ck_t, n_lat=n_lat, n_ctx=n_ctx,
                                 seq=seq, ctx_len=ctx_len)
        att = _attention(q, k, v, n_batch=n_batch, seq=seq, ctx_len=ctx_len, latent=True)
        att_c = _attention(q, k, v, n_batch=n_batch, seq=seq, ctx_len=ctx_len, latent=False) if ctx_out else None

        g2p = jnp.pad(gate_g2[l], ((0, GATE_PAD - GATE_LORA), (0, 0))).astype(BF16)
        prep = _rwkv_prep(
            p_rw, mu, _block_diag2(decay_w2[l]).astype(BF16), decay_w0[l].reshape(1, 2 * w),
            _block_diag2(iclr_a2[l]).astype(BF16), iclr_a0[l].reshape(1, 2 * w), g2p,
            row1(k_k[l]), row1(k_a[l]), row1(r_k[l]), ones_bd, tri_f, tri_r, ones_c,
            n_lat=n_lat, seq=seq, ctx_len=ctx_len)
        vv, g, bon = prep[0], prep[11], prep[12]
        yf, yb = _rwkv_scan(vv, prep[1:6], prep[6:11], n_batch=n_batch, seq=seq, ctx_len=ctx_len)

        wo = w_out[l].astype(BF16)
        x_all = _mix_out(yf, yb, att, att_c, g, bon, row1(lnx_g[l]), row1(lnx_b[l]), ones_bd, wo[:w], wo[w:], x_lat,
                         x_ctx, mods, row1(mix_post_g[l]), rows=rows, n_lat=n_lat, seq=seq)

        experts = tuple(a[l].astype(BF16) for a in (exp_w_gate, exp_w_up, exp_w_down))
        shared = tuple(a[l].astype(BF16) for a in (sh_w_gate, sh_w_up, sh_w_down))
        x_all = _moe(x_all, mods, row1(ffn_pre_g[l]), row1(ffn_post_g[l]), router_wt, router_b, before_m, experts,
                     shared, rows=rows, n_lat=n_lat, seq=seq)
        x_lat = x_ctx = x_all

    return x_all.reshape(n_batch, seq, d)
```
